```python
import math
import jax, jax.numpy as jnp
from jax import lax
import numpy as np

D_MODEL = 2048
BATCH = 8
SEQ = 4096
DEPTH = 2
DEC_BATCH = 1
DEC_SEQ = 8192
PAST_LEN = 128

HEAD_DIM = 128
GRID_W = 64
ROPE_THETA = 10000.0
EPS = 1e-6
Q_BLOCK = 128
NEG_INF = -1e30

DIL_GROUPS = ((128, 1), (512, 4), (2048, 16))
A_HEADS_PER_GROUP = 2
A_HEADS = A_HEADS_PER_GROUP * len(DIL_GROUPS)
A_W = A_HEADS * HEAD_DIM
A_OUT = A_HEADS_PER_GROUP * HEAD_DIM

B_Q_HEADS = 6
B_KV_HEADS = 2
B_QW = B_Q_HEADS * HEAD_DIM
B_KVW = B_KV_HEADS * HEAD_DIM
AXIAL_DIM = HEAD_DIM // 2

C_HEADS = 4
C_QK_DIM = 64
C_V_DIM = 128
C_QW = C_HEADS * 2 * C_QK_DIM
C_VW = C_HEADS * C_V_DIM

N_BRANCH = 3
IN_COLS = 3 * A_W + B_QW + 2 * B_KVW + 2 * C_QW + C_VW + N_BRANCH * D_MODEL
D_FF = 4 * D_MODEL

kernel_name = "hybrid_dilated_axial_diff_encoder"


def _rmsnorm(x, g):
    xf = x.astype(jnp.float32)
    y = xf * lax.rsqrt(jnp.mean(xf * xf, axis=-1, keepdims=True) + EPS)
    return (y * g.astype(jnp.float32)).astype(x.dtype)


def _rope_tables(pos, dim):
    inv = 1.0 / (ROPE_THETA ** (jnp.arange(0, dim, 2, dtype=jnp.float32) / dim))
    ang = pos.astype(jnp.float32)[:, None] * inv[None, :]
    return jnp.cos(ang), jnp.sin(ang)


def _apply_rope(x, cos, sin):
    half = x.shape[-1] // 2
    shape = (cos.shape[0],) + (1,) * (x.ndim - 3) + (half,)
    c = cos.reshape(shape)
    s_ = sin.reshape(shape)
    xf = x.astype(jnp.float32)
    x1, x2 = xf[..., :half], xf[..., half:]
    return jnp.concatenate([x1 * c - x2 * s_, x1 * s_ + x2 * c], axis=-1).astype(x.dtype)


def _apply_axial_rope(x, rope_row, rope_col):
    xr = _apply_rope(x[..., :AXIAL_DIM], rope_row[0], rope_row[1])
    xc = _apply_rope(x[..., AXIAL_DIM:], rope_col[0], rope_col[1])
    return jnp.concatenate([xr, xc], axis=-1)


def _dilated_group(q, k, v, dil, half):
    b, s, h, dh = q.shape
    blk = half
    L = s // dil
    nblk = -(-L // blk)
    lp = nblk * blk

    def streams(t, lo, hi):
        t = t.reshape(b, L, dil, h, dh)
        return jnp.pad(t, ((0, 0), (lo, hi), (0, 0), (0, 0), (0, 0)))

    qs = streams(q, 0, lp - L).reshape(b, nblk, blk, dil, h, dh)

    def windows(t):
        tb = streams(t, blk, lp - L + blk).reshape(b, nblk + 2, blk, dil, h, dh)
        return jnp.concatenate([tb[:, :-2], tb[:, 1:-1], tb[:, 2:]], axis=2)

    kw, vw = windows(k), windows(v)
    sc = jnp.einsum('bnqrhd,bnkrhd->bnrhqk', qs, kw,
                    preferred_element_type=jnp.float32) * (dh ** -0.5)
    i = jnp.arange(blk)[:, None]
    j = jnp.arange(3 * blk)[None, :]
    uk = jnp.arange(nblk)[:, None, None] * blk - blk + j
    valid = (jnp.abs(i + blk - j) <= half) & (uk >= 0) & (uk < L)
    sc = jnp.where(valid[None, :, None, None], sc, NEG_INF)
    m = jnp.max(sc, axis=-1, keepdims=True)
    e = jnp.exp(sc - m)
    den = jnp.sum(e, axis=-1, keepdims=True)
    o = jnp.einsum('bnrhqk,bnkrhd->bnqrhd', (e / den).astype(v.dtype), vw)
    o = o.reshape(b, lp, dil, h, dh)[:, :L].reshape(b, s, h, dh)
    lse = (m + jnp.log(den))[..., 0]
    lse = lse.transpose(0, 1, 4, 2, 3).reshape(b, lp, dil, h)[:, :L].reshape(b, s, h)
    return o, lse


def _dense_gqa(q, k, v):
    b, s, h, dq = q.shape
    kh = k.shape[2]
    g = h // kh
    nblk = s // Q_BLOCK
    qb = q.reshape(b, nblk, Q_BLOCK, kh, g, dq).swapaxes(0, 1)
    scale = dq ** -0.5

    def one_block(qblk):
        sc = jnp.einsum('bqkgd,bskd->bkgqs', qblk, k, preferred_element_type=jnp.float32) * scale
        p = jax.nn.softmax(sc, axis=-1)
        return jnp.einsum('bkgqs,bskd->bqkgd', p.astype(v.dtype), v)

    out = lax.map(one_block, qb)
    return out.swapaxes(0, 1).reshape(b, s, h, v.shape[-1])


def _diff_attention(q, k, v, lam):
    b, s, h, _, dq = q.shape
    nblk = s // Q_BLOCK
    qb = q.reshape(b, nblk, Q_BLOCK, h, 2, dq).swapaxes(0, 1)
    scale = dq ** -0.5

    def one_block(qblk):
        sc = jnp.einsum('bqhmd,bshmd->bhmqs', qblk, k, preferred_element_type=jnp.float32) * scale
        p = jax.nn.softmax(sc, axis=-1)
        w = p[:, :, 0] - lam * p[:, :, 1]
        return jnp.einsum('bhqs,bshd->bqhd', w.astype(v.dtype), v)

    out = lax.map(one_block, qb)
    return out.swapaxes(0, 1).reshape(b, s, h, v.shape[-1])


def _layer(x, lam_init, rope_a, rope_c, rope_row, rope_col,
           g_mix_pre, w_in, b_q_norm, b_k_norm, c_lambda, c_head_norm,
           w_branch_a, w_branch_b, w_branch_c, w_out, g_mix_post,
           g_mlp_pre, w_mlp_in, w_mlp_out, g_mlp_post):
    b, s, _ = x.shape
    h = _rmsnorm(x, g_mix_pre)
    proj = h @ w_in
    widths = (A_W, A_W, A_W, B_QW, B_KVW, B_KVW, C_QW, C_QW, C_VW, N_BRANCH * D_MODEL)
    parts = []
    off = 0
    for wd in widths:
        parts.append(proj[..., off:off + wd])
        off += wd
    a_q, a_k, a_v, b_q, b_k, b_v, c_q, c_k, c_v, gates = parts

    aq = _apply_rope(a_q.reshape(b, s, A_HEADS, HEAD_DIM), rope_a[0], rope_a[1])
    ak = _apply_rope(a_k.reshape(b, s, A_HEADS, HEAD_DIM), rope_a[0], rope_a[1])
    av = a_v.reshape(b, s, A_HEADS, HEAD_DIM)
    outs, lses = [], []
    for gi, (win, dil) in enumerate(DIL_GROUPS):
        sl = slice(gi * A_HEADS_PER_GROUP, (gi + 1) * A_HEADS_PER_GROUP)
        o, l = _dilated_group(aq[:, :, sl], ak[:, :, sl], av[:, :, sl], dil, win // (2 * dil))
        outs.append(o)
        lses.append(l)
    alpha = jax.nn.softmax(jnp.stack(lses, axis=0), axis=0)
    o_a = jnp.sum(alpha[..., None] * jnp.stack(outs, axis=0).astype(jnp.float32), axis=0)
    o_a = o_a.astype(x.dtype).reshape(b, s, A_OUT)

    bq = _rmsnorm(b_q.reshape(b, s, B_Q_HEADS, HEAD_DIM), b_q_norm)
    bk = _rmsnorm(b_k.reshape(b, s, B_KV_HEADS, HEAD_DIM), b_k_norm)
    bq = _apply_axial_rope(bq, rope_row, rope_col)
    bk = _apply_axial_rope(bk, rope_row, rope_col)
    o_b = _dense_gqa(bq, bk, b_v.reshape(b, s, B_KV_HEADS, HEAD_DIM)).reshape(b, s, B_QW)

    cq = _apply_rope(c_q.reshape(b, s, C_HEADS, 2, C_QK_DIM), rope_c[0], rope_c[1])
    ck = _apply_rope(c_k.reshape(b, s, C_HEADS, 2, C_QK_DIM), rope_c[0], rope_c[1])
    lamv = c_lambda.astype(jnp.float32)
    lam = jnp.exp(jnp.sum(lamv[0] * lamv[1])) - jnp.exp(jnp.sum(lamv[2] * lamv[3])) + lam_init
    oc = _diff_attention(cq, ck, c_v.reshape(b, s, C_HEADS, C_V_DIM), lam)
    o_c = (_rmsnorm(oc, c_head_norm) * (1.0 - lam_init)).reshape(b, s, C_VW)

    gt = jax.nn.sigmoid(gates.astype(jnp.float32).reshape(b, s, N_BRANCH, D_MODEL)).astype(x.dtype)
    merged = (gt[:, :, 0] * (o_a @ w_branch_a)
              + gt[:, :, 1] * (o_b @ w_branch_b)
              + gt[:, :, 2] * (o_c @ w_branch_c))
    x = x + _rmsnorm(merged @ w_out, g_mix_post)

    hm = _rmsnorm(x, g_mlp_pre)
    u = jnp.square(jax.nn.relu(hm @ w_mlp_in))
    x = x + _rmsnorm(u @ w_mlp_out, g_mlp_post)
    return x


def _trunk(x, g_mix_pre, w_in, b_q_norm, b_k_norm, c_lambda, c_head_norm,
           w_branch_a, w_branch_b, w_branch_c, w_out, g_mix_post,
           g_mlp_pre, w_mlp_in, w_mlp_out, g_mlp_post):
    s = x.shape[1]
    rows = s // GRID_W
    pos = jnp.arange(s)
    row = jnp.repeat(jnp.arange(rows), GRID_W)
    col = pos % GRID_W
    rope_a = _rope_tables(pos, HEAD_DIM)
    rope_c = _rope_tables(pos, C_QK_DIM)
    rope_row = _rope_tables(row, AXIAL_DIM)
    rope_col = _rope_tables(col, AXIAL_DIM)
    for li in range(DEPTH):
        lam_init = 0.8 - 0.6 * math.exp(-0.3 * li)
        x = _layer(x, lam_init, rope_a, rope_c, rope_row, rope_col,
                   g_mix_pre[li], w_in[li], b_q_norm[li], b_k_norm[li], c_lambda[li], c_head_norm[li],
                   w_branch_a[li], w_branch_b[li], w_branch_c[li], w_out[li], g_mix_post[li],
                   g_mlp_pre[li], w_mlp_in[li], w_mlp_out[li], g_mlp_post[li])
    return x


def setup_inputs(seed: int = 0) -> dict:
    key = jax.random.key(seed)
    ks = jax.random.split(key, 18)
    f32 = jnp.float32

    def nrm(k, shape, scale):
        return jax.random.normal(k, shape, f32) * scale

    def gain(k, shape):
        return 1.0 + 0.02 * jax.random.normal(k, shape, f32)

    return {
        "x_prompt": nrm(ks[0], (BATCH, SEQ, D_MODEL), 1.0),
        "x_sample": nrm(ks[1], (DEC_BATCH, DEC_SEQ, D_MODEL), 1.0),
        "g_mix_pre": gain(ks[2], (DEPTH, D_MODEL)),
        "w_in": nrm(ks[3], (DEPTH, D_MODEL, IN_COLS), D_MODEL ** -0.5),
        "b_q_norm": gain(ks[4], (DEPTH, HEAD_DIM)),
        "b_k_norm": gain(ks[5], (DEPTH, HEAD_DIM)),
        "c_lambda": nrm(ks[6], (DEPTH, 4, C_QK_DIM), 0.1),
        "c_head_norm": gain(ks[7], (DEPTH, C_V_DIM)),
        "w_branch_a": nrm(ks[8], (DEPTH, A_OUT, D_MODEL), A_OUT ** -0.5),
        "w_branch_b": nrm(ks[9], (DEPTH, B_QW, D_MODEL), B_QW ** -0.5),
        "w_branch_c": nrm(ks[10], (DEPTH, C_VW, D_MODEL), C_VW ** -0.5),
        "w_out": nrm(ks[11], (DEPTH, D_MODEL, D_MODEL), D_MODEL ** -0.5),
        "g_mix_post": gain(ks[12], (DEPTH, D_MODEL)),
        "g_mlp_pre": gain(ks[13], (DEPTH, D_MODEL)),
        "w_mlp_in": nrm(ks[14], (DEPTH, D_MODEL, D_FF), D_MODEL ** -0.5),
        "w_mlp_out": nrm(ks[15], (DEPTH, D_FF, D_MODEL), D_FF ** -0.5),
        "g_mlp_post": gain(ks[16], (DEPTH, D_MODEL)),
    }


def reference(x_prompt, x_sample, g_mix_pre, w_in, b_q_norm, b_k_norm, c_lambda, c_head_norm,
              w_branch_a, w_branch_b, w_branch_c, w_out, g_mix_post,
              g_mlp_pre, w_mlp_in, w_mlp_out, g_mlp_post):
    y_prompt = _trunk(x_prompt, g_mix_pre, w_in, b_q_norm, b_k_norm, c_lambda, c_head_norm,
                      w_branch_a, w_branch_b, w_branch_c, w_out, g_mix_post,
                      g_mlp_pre, w_mlp_in, w_mlp_out, g_mlp_post)
    y_sample = _trunk(x_sample, g_mix_pre, w_in, b_q_norm, b_k_norm, c_lambda, c_head_norm,
                      w_branch_a, w_branch_b, w_branch_c, w_out, g_mix_post,
                      g_mlp_pre, w_mlp_in, w_mlp_out, g_mlp_post)
    return (y_prompt, y_sample)
```

```python
import functools
import math

import jax
import jax.numpy as jnp
from jax import lax
from jax.experimental import pallas as pl
from jax.experimental.pallas import tpu as pltpu

F32 = jnp.float32
BF16 = jnp.bfloat16

D_MODEL = 2048
DEPTH = 2
HEAD_DIM = 128
GRID_W = 64
ROPE_THETA = 10000.0
EPS = 1e-6
NEG_INF = -1e30

DIL_GROUPS = ((128, 1), (512, 4), (2048, 16))
A_HALF = 64
A_W = 768
B_Q_HEADS = 6
B_KV_HEADS = 2
B_GROUP = B_Q_HEADS // B_KV_HEADS
B_QW = 768
B_KVW = 256
AXIAL_DIM = 64
C_HEADS = 4
C_QK_DIM = 64
C_QW = 512
C_VW = 512
N_BRANCH = 3
QKV_COLS = 3 * A_W + B_QW + 2 * B_KVW + 2 * C_QW + C_VW
GATE_COLS = N_BRANCH * D_MODEL
D_FF = 4 * D_MODEL

OFF_AQ, OFF_AK, OFF_AV = 0, 768, 1536
OFF_BQ, OFF_BK, OFF_BV = 2304, 3072, 3328
OFF_CQ, OFF_CK, OFF_CV = 3584, 4096, 4608

SUB = 256
_QKV_KINDS = (("aq",) * 3 + ("ak",) * 3 + ("v",) * 3 + ("bq",) * 3 + ("bk",) + ("v",)
              + ("cq",) * 2 + ("ck",) * 2 + ("v",) * 2)

VMEM_LIMIT = 56 * 1024 * 1024


def _cparams(sem):
    return pltpu.CompilerParams(dimension_semantics=sem, vmem_limit_bytes=VMEM_LIMIT)


def _rms_rows(xf, g):
    ms = jnp.mean(xf * xf, axis=-1, keepdims=True)
    return xf * lax.rsqrt(ms + EPS) * g


def _rope_half64(x, cos, sin_signed):
    return x * cos + pltpu.roll(x, 64, axis=1) * sin_signed


def _rope_half32(x, cos, sin_lo, sin_hi):
    return x * cos + pltpu.roll(x, 96, axis=1) * sin_lo + pltpu.roll(x, 32, axis=1) * sin_hi


def _norm_kernel(x_ref, g_ref, h_ref):
    h_ref[...] = _rms_rows(x_ref[...], g_ref[...]).astype(h_ref.dtype)


def _prenorm(x, g, tm=1024):
    t = x.shape[0]
    return pl.pallas_call(
        _norm_kernel,
        grid=(t // tm,),
        in_specs=[pl.BlockSpec((tm, D_MODEL), lambda i: (i, 0)),
                  pl.BlockSpec((1, D_MODEL), lambda i: (0, 0))],
        out_specs=pl.BlockSpec((tm, D_MODEL), lambda i: (i, 0)),
        out_shape=jax.ShapeDtypeStruct((t, D_MODEL), BF16),
        compiler_params=_cparams(("parallel",)),
        name="prenorm",
    )(x, g.reshape(1, D_MODEL))


def _qkv_kernel(h_ref, w_ref, tab_ref, gq_ref, gk_ref, o_ref, *, nsub):
    j = pl.program_id(1)
    acc = jnp.dot(h_ref[...], w_ref[...], preferred_element_type=F32)

    def head_epilogue(x, kind):
        if kind in ("aq", "ak"):
            y = _rope_half64(x, tab_ref[0], tab_ref[1])
            return y * (HEAD_DIM ** -0.5) if kind == "aq" else y
        if kind in ("bq", "bk"):
            g = gq_ref[...] if kind == "bq" else gk_ref[...]
            y = _rope_half32(_rms_rows(x, g), tab_ref[2], tab_ref[3], tab_ref[4])
            return y * (HEAD_DIM ** -0.5) if kind == "bq" else y
        if kind in ("cq", "ck"):
            y = _rope_half32(x, tab_ref[5], tab_ref[6], tab_ref[7])
            return y * (C_QK_DIM ** -0.5) if kind == "cq" else y
        return x

    nj = len(_QKV_KINDS) // nsub
    for jj in range(nj):
        @pl.when(j == jj)
        def _():
            for s in range(nsub):
                kind = _QKV_KINDS[jj * nsub + s]
                for hh in range(SUB // 128):
                    lo = s * SUB + hh * 128
                    o_ref[:, lo:lo + 128] = head_epilogue(acc[:, lo:lo + 128], kind).astype(o_ref.dtype)


def _qkv_proj(h, w_qkv, tabs, gq, gk, segments, tm=1024, nsub=1):
    t = h.shape[0]
    tn = nsub * SUB

    def pos_block(i):
        blk = jnp.int32(0)
        for (row0, _, seq) in segments:
            blk = jnp.where(i >= row0 // tm, (i - row0 // tm) % (seq // tm), blk)
        return blk

    return pl.pallas_call(
        functools.partial(_qkv_kernel, nsub=nsub),
        grid=(t // tm, QKV_COLS // tn),
        in_specs=[pl.BlockSpec((tm, D_MODEL), lambda i, j: (i, 0)),
                  pl.BlockSpec((D_MODEL, tn), lambda i, j: (0, j)),
                  pl.BlockSpec((8, tm, 128), lambda i, j: (0, pos_block(i), 0)),
                  pl.BlockSpec((1, 128), lambda i, j: (0, 0)),
                  pl.BlockSpec((1, 128), lambda i, j: (0, 0))],
        out_specs=pl.BlockSpec((tm, tn), lambda i, j: (i, j)),
        out_shape=jax.ShapeDtypeStruct((t, QKV_COLS), BF16),
        compiler_params=_cparams(("parallel", "arbitrary")),
        name="qkv_proj",
    )(h, w_qkv, tabs, gq.reshape(1, 128), gk.reshape(1, 128))


def _gate_kernel(h_ref, w_ref, o_ref):
    acc = jnp.dot(h_ref[...], w_ref[...], preferred_element_type=F32)
    o_ref[...] = jax.nn.sigmoid(acc).astype(o_ref.dtype)


def _gate_proj(h, w_gate, tm=1024, tn=1024):
    t = h.shape[0]
    return pl.pallas_call(
        _gate_kernel,
        grid=(t // tm, GATE_COLS // tn),
        in_specs=[pl.BlockSpec((tm, D_MODEL), lambda i, j: (i, 0)),
                  pl.BlockSpec((D_MODEL, tn), lambda i, j: (0, j))],
        out_specs=pl.BlockSpec((tm, tn), lambda i, j: (i, j)),
        out_shape=jax.ShapeDtypeStruct((t, GATE_COLS), BF16),
        compiler_params=_cparams(("parallel", "arbitrary")),
        name="gate_proj",
    )(h, w_gate)


A_TQ = 128
A_TK = A_TQ + 2 * A_HALF


def _attn_a_kernel(q_ref, k_ref, v_ref, o_ref, lse_ref, *, length):
    nt = length // A_TQ
    rows = lax.broadcasted_iota(jnp.int32, (A_TQ, A_TK), 0)
    cols = lax.broadcasted_iota(jnp.int32, (A_TQ, A_TK), 1)

    def body(i, carry):
        q0 = pl.multiple_of(i * A_TQ, A_TQ)
        k0 = pl.multiple_of(jnp.clip(q0 - A_HALF, 0, length - A_TK), 16)
        valid = jnp.abs(rows - cols + (q0 - k0)) <= A_HALF
        for hh in range(2):
            sl = slice(hh * 128, (hh + 1) * 128)
            q = q_ref[pl.ds(q0, A_TQ), sl]
            k = k_ref[pl.ds(k0, A_TK), sl]
            v = v_ref[pl.ds(k0, A_TK), sl]
            s = lax.dot_general(q, k, (((1,), (1,)), ((), ())), preferred_element_type=F32)
            s = jnp.where(valid, s, NEG_INF)
            m = jnp.max(s, axis=1, keepdims=True)
            e = jnp.exp(s - m)
            den = jnp.sum(e, axis=1, keepdims=True)
            o = jnp.dot(e.astype(BF16), v, preferred_element_type=F32) / den
            o_ref[pl.ds(q0, A_TQ), sl] = o.astype(o_ref.dtype)
            lse_ref[pl.ds(q0, A_TQ), sl] = jnp.broadcast_to(m + jnp.log(den), (A_TQ, 128))
        return carry

    lax.fori_loop(0, nt, body, 0)


def _attn_a_group(qkv, gi, dil, row0, batch, seq):
    t_all = qkv.shape[0]
    length = seq // dil
    view = qkv.reshape(t_all // dil, dil * QKV_COLS)
    rb0 = row0 // dil // length
    cb = QKV_COLS // SUB
    qb, kb, vb = OFF_AQ // SUB + gi, OFF_AK // SUB + gi, OFF_AV // SUB + gi

    def spec(colblk):
        return pl.BlockSpec((length, SUB), lambda b, r: (rb0 + b, r * cb + colblk))

    out_spec = pl.BlockSpec((length, SUB), lambda b, r: (b, r))
    o, lse = pl.pallas_call(
        functools.partial(_attn_a_kernel, length=length),
        grid=(batch, dil),
        in_specs=[spec(qb), spec(kb), spec(vb)],
        out_specs=[out_spec, out_spec],
        out_shape=[jax.ShapeDtypeStruct((batch * length, dil * SUB), BF16),
                   jax.ShapeDtypeStruct((batch * length, dil * SUB), F32)],
        compiler_params=_cparams(("parallel", "parallel")),
        name=f"attn_a{gi}",
    )(view, view, view)
    return o.reshape(batch * seq, SUB), lse.reshape(batch * seq, SUB)


def _flash_rows(q, k_ref, v_ref, seq, tk):
    rows = q.shape[0]

    def body(t, carry):
        m, l, acc = carry
        k0 = pl.multiple_of(t * tk, tk)
        k = k_ref[pl.ds(k0, tk), :]
        v = v_ref[pl.ds(k0, tk), :]
        s = lax.dot_general(q, k, (((1,), (1,)), ((), ())), preferred_element_type=F32)
        m_new = jnp.maximum(m, jnp.max(s, axis=1, keepdims=True))
        alpha = jnp.exp(m - m_new)
        p = jnp.exp(s - m_new)
        l = alpha * l + jnp.sum(p, axis=1, keepdims=True)
        acc = alpha * acc + jnp.dot(p.astype(BF16), v, preferred_element_type=F32)
        return m_new, l, acc

    init = (jnp.full((rows, 1), NEG_INF, F32), jnp.zeros((rows, 1), F32),
            jnp.zeros((rows, HEAD_DIM), F32))
    _, l, acc = lax.fori_loop(0, seq // tk, body, init)
    return acc, l


def _attn_b_kernel(q_ref, k_ref, v_ref, o_ref, *, seq, tq, tk):
    q = jnp.concatenate([q_ref[:, h * 128:(h + 1) * 128] for h in range(B_GROUP)], axis=0)
    acc, l = _flash_rows(q, k_ref, v_ref, seq, tk)
    o = acc / l
    for h in range(B_GROUP):
        o_ref[:, h * 128:(h + 1) * 128] = o[h * tq:(h + 1) * tq].astype(o_ref.dtype)


def _attn_b(qkv, row0, batch, seq, tq=256, tk=512):
    nq = seq // tq
    gw = B_GROUP * 128
    qspec = pl.BlockSpec((tq, gw), lambda b, kh, i: (row0 // tq + b * nq + i, OFF_BQ // gw + kh))
    kspec = pl.BlockSpec((seq, 128), lambda b, kh, i: (row0 // seq + b, OFF_BK // 128 + kh))
    vspec = pl.BlockSpec((seq, 128), lambda b, kh, i: (row0 // seq + b, OFF_BV // 128 + kh))
    return pl.pallas_call(
        functools.partial(_attn_b_kernel, seq=seq, tq=tq, tk=tk),
        grid=(batch, B_KV_HEADS, nq),
        in_specs=[qspec, kspec, vspec],
        out_specs=pl.BlockSpec((tq, gw), lambda b, kh, i: (b * nq + i, kh)),
        out_shape=jax.ShapeDtypeStruct((batch * seq, B_QW), BF16),
        compiler_params=_cparams(("parallel", "parallel", "arbitrary")),
        name="attn_b",
    )(qkv, qkv, qkv)


def _attn_c_kernel(lam_ref, g_ref, q_ref, k_ref, v_ref, o_ref, *, seq, tq, tk, lam_init):
    lamv = lam_ref[...]
    lam = (jnp.exp(jnp.sum(lamv[0:1] * lamv[1:2], axis=1, keepdims=True))
           - jnp.exp(jnp.sum(lamv[2:3] * lamv[3:4], axis=1, keepdims=True)) + lam_init)
    q = q_ref[...]
    lane = lax.broadcasted_iota(jnp.int32, q.shape, 1)
    zero = jnp.zeros_like(q)
    q2 = jnp.concatenate([jnp.where(lane < C_QK_DIM, q, zero), jnp.where(lane >= C_QK_DIM, q, zero)], axis=0)
    acc, l = _flash_rows(q2, k_ref, v_ref, seq, tk)
    o = acc / l
    oc = o[:tq] - lam * o[tq:]
    o_ref[...] = (_rms_rows(oc, g_ref[...]) * (1.0 - lam_init)).astype(o_ref.dtype)


def _attn_c(qkv, c_lambda, c_head_norm, lam_init, row0, batch, seq, tq=256, tk=512):
    nq = seq // tq
    qspec = pl.BlockSpec((tq, 128), lambda b, h, i: (row0 // tq + b * nq + i, OFF_CQ // 128 + h))
    kspec = pl.BlockSpec((seq, 128), lambda b, h, i: (row0 // seq + b, OFF_CK // 128 + h))
    vspec = pl.BlockSpec((seq, 128), lambda b, h, i: (row0 // seq + b, OFF_CV // 128 + h))
    return pl.pallas_call(
        functools.partial(_attn_c_kernel, seq=seq, tq=tq, tk=tk, lam_init=lam_init),
        grid=(batch, C_HEADS, nq),
        in_specs=[pl.BlockSpec((4, C_QK_DIM), lambda b, h, i: (0, 0)),
                  pl.BlockSpec((1, 128), lambda b, h, i: (0, 0)),
                  qspec, kspec, vspec],
        out_specs=pl.BlockSpec((tq, 128), lambda b, h, i: (b * nq + i, h)),
        out_shape=jax.ShapeDtypeStruct((batch * seq, C_VW), BF16),
        compiler_params=_cparams(("parallel", "parallel", "arbitrary")),
        name="attn_c",
    )(c_lambda.astype(F32), c_head_norm.reshape(1, 128).astype(F32), qkv, qkv, qkv)


def _merge_kernel(oa0, oa1, oa2, ls0, ls1, ls2, ob, oc, g0, g1, g2, wa, wb, wc, out_ref, oa_scr):
    @pl.when(pl.program_id(1) == 0)
    def _():
        l0, l1, l2 = ls0[...], ls1[...], ls2[...]
        m = jnp.maximum(jnp.maximum(l0, l1), l2)
        e0, e1, e2 = jnp.exp(l0 - m), jnp.exp(l1 - m), jnp.exp(l2 - m)
        num = e0 * oa0[...].astype(F32) + e1 * oa1[...].astype(F32) + e2 * oa2[...].astype(F32)
        oa_scr[...] = (num / (e0 + e1 + e2)).astype(oa_scr.dtype)

    pa = jnp.dot(oa_scr[...], wa[...], preferred_element_type=F32)
    pb = jnp.dot(ob[...], wb[...], preferred_element_type=F32)
    pc = jnp.dot(oc[...], wc[...], preferred_element_type=F32)
    merged = g0[...].astype(F32) * pa + g1[...].astype(F32) * pb + g2[...].astype(F32) * pc
    out_ref[...] = merged.astype(out_ref.dtype)


def _merge(oa, lse, ob, oc, gates, wa, wb, wc, tm=1024, tn=512):
    t = ob.shape[0]
    nj = D_MODEL // tn

    def row(w):
        return pl.BlockSpec((tm, w), lambda i, j: (i, 0))

    def gate(br):
        return pl.BlockSpec((tm, tn), lambda i, j: (i, br * nj + j))

    def wspec(k):
        return pl.BlockSpec((k, tn), lambda i, j: (0, j))

    return pl.pallas_call(
        _merge_kernel,
        grid=(t // tm, nj),
        in_specs=[row(SUB)] * 6 + [row(B_QW), row(C_VW), gate(0), gate(1), gate(2),
                                   wspec(SUB), wspec(B_QW), wspec(C_VW)],
        out_specs=pl.BlockSpec((tm, tn), lambda i, j: (i, j)),
        out_shape=jax.ShapeDtypeStruct((t, D_MODEL), BF16),
        scratch_shapes=[pltpu.VMEM((tm, SUB), BF16)],
        compiler_params=_cparams(("parallel", "arbitrary")),
        name="merge",
    )(*oa, *lse, ob, oc, gates, gates, gates, wa, wb, wc)


def _out_kernel(m_ref, w_ref, x_ref, gpost_ref, gnext_ref, x1_ref, h_ref):
    y = jnp.dot(m_ref[...], w_ref[...], preferred_element_type=F32)
    x1 = x_ref[...] + _rms_rows(y, gpost_ref[...])
    x1_ref[...] = x1
    h_ref[...] = _rms_rows(x1, gnext_ref[...]).astype(h_ref.dtype)


def _out_proj(merged, w_out, x, g_post, g_next, tm=512):
    t = x.shape[0]
    row = lambda i: (i, 0)
    const = lambda i: (0, 0)
    return pl.pallas_call(
        _out_kernel,
        grid=(t // tm,),
        in_specs=[pl.BlockSpec((tm, D_MODEL), row),
                  pl.BlockSpec((D_MODEL, D_MODEL), const),
                  pl.BlockSpec((tm, D_MODEL), row),
                  pl.BlockSpec((1, D_MODEL), const),
                  pl.BlockSpec((1, D_MODEL), const)],
        out_specs=[pl.BlockSpec((tm, D_MODEL), row), pl.BlockSpec((tm, D_MODEL), row)],
        out_shape=[jax.ShapeDtypeStruct((t, D_MODEL), F32), jax.ShapeDtypeStruct((t, D_MODEL), BF16)],
        compiler_params=_cparams(("parallel",)),
        name="out_proj",
    )(merged, w_out, x, g_post.reshape(1, D_MODEL), g_next.reshape(1, D_MODEL))


def _mlp_kernel(h_ref, w1_ref, w2_ref, x_ref, gpost_ref, *rest, with_next):
    if with_next:
        gnext_ref, x2_ref, hn_ref, acc_ref = rest
    else:
        x2_ref, acc_ref = rest
    f = pl.program_id(1)
    u = jnp.dot(h_ref[...], w1_ref[...], preferred_element_type=F32)
    u = jnp.square(jnp.maximum(u, 0.0)).astype(BF16)
    part = jnp.dot(u, w2_ref[...], preferred_element_type=F32)

    @pl.when(f == 0)
    def _():
        acc_ref[...] = part

    @pl.when(f > 0)
    def _():
        acc_ref[...] += part

    @pl.when(f == pl.num_programs(1) - 1)
    def _():
        x2 = x_ref[...] + _rms_rows(acc_ref[...], gpost_ref[...])
        x2_ref[...] = x2
        if with_next:
            hn_ref[...] = _rms_rows(x2, gnext_ref[...]).astype(hn_ref.dtype)


def _mlp(h, w1, w2, x, g_post, g_next, tm=512, tf=512):
    t = x.shape[0]
    with_next = g_next is not None
    row = lambda i, f: (i, 0)
    const = lambda i, f: (0, 0)
    in_specs = [pl.BlockSpec((tm, D_MODEL), row),
                pl.BlockSpec((D_MODEL, tf), lambda i, f: (0, f)),
                pl.BlockSpec((tf, D_MODEL), lambda i, f: (f, 0)),
                pl.BlockSpec((tm, D_MODEL), row),
                pl.BlockSpec((1, D_MODEL), const)]
    args = [h, w1, w2, x, g_post.reshape(1, D_MODEL)]
    out_specs = [pl.BlockSpec((tm, D_MODEL), row)]
    out_shape = [jax.ShapeDtypeStruct((t, D_MODEL), F32)]
    if with_next:
        in_specs.append(pl.BlockSpec((1, D_MODEL), const))
        args.append(g_next.reshape(1, D_MODEL))
        out_specs.append(pl.BlockSpec((tm, D_MODEL), row))
        out_shape.append(jax.ShapeDtypeStruct((t, D_MODEL), BF16))
    res = pl.pallas_call(
        functools.partial(_mlp_kernel, with_next=with_next),
        grid=(t // tm, D_FF // tf),
        in_specs=in_specs,
        out_specs=out_specs,
        out_shape=out_shape,
        scratch_shapes=[pltpu.VMEM((tm, D_MODEL), F32)],
        compiler_params=_cparams(("parallel", "arbitrary")),
        name="mlp",
    )(*args)
    return (res[0], res[1]) if with_next else (res[0], None)


def _rope_tables(pos, dim):
    inv = 1.0 / (ROPE_THETA ** (jnp.arange(0, dim, 2, dtype=F32) / dim))
    ang = pos.astype(F32)[:, None] * inv[None, :]
    return jnp.cos(ang), jnp.sin(ang)


def _tables(seq):
    pos = jnp.arange(seq)
    ca, sa = _rope_tables(pos, HEAD_DIM)
    cos_a = jnp.concatenate([ca, ca], axis=1)
    sin_a = jnp.concatenate([-sa, sa], axis=1)
    z32 = jnp.zeros((seq, 32), F32)

    def half32(c, s):
        return jnp.concatenate([c, c], 1), jnp.concatenate([-s, z32], 1), jnp.concatenate([z32, s], 1)

    cr, sr = _rope_tables(pos // GRID_W, AXIAL_DIM)
    cc, sc = _rope_tables(pos % GRID_W, AXIAL_DIM)
    r_c, r_lo, r_hi = half32(cr, sr)
    c_c, c_lo, c_hi = half32(cc, sc)
    cos_b = jnp.concatenate([r_c, c_c], 1)
    sin_b_lo = jnp.concatenate([r_lo, c_lo], 1)
    sin_b_hi = jnp.concatenate([r_hi, c_hi], 1)
    c3, s3 = _rope_tables(pos, C_QK_DIM)
    m_c, m_lo, m_hi = half32(c3, s3)
    cos_c = jnp.concatenate([m_c, m_c], 1)
    sin_c_lo = jnp.concatenate([m_lo, m_lo], 1)
    sin_c_hi = jnp.concatenate([m_hi, m_hi], 1)
    return jnp.stack([cos_a, sin_a, cos_b, sin_b_lo, sin_b_hi, cos_c, sin_c_lo, sin_c_hi], axis=0)


def _trunk(x, segments, p):
    tabs = _tables(max(seq for (_, _, seq) in segments))
    h = _prenorm(x, p["g_mix_pre"][0])
    for li in range(DEPTH):
        lam_init = 0.8 - 0.6 * math.exp(-0.3 * li)
        w_in = p["w_in"][li]
        qkv = _qkv_proj(h, w_in[:, :QKV_COLS], tabs, p["b_q_norm"][li], p["b_k_norm"][li], segments)
        gates = _gate_proj(h, w_in[:, QKV_COLS:])

        oa = [[], [], []]
        lse = [[], [], []]
        ob, oc = [], []
        for (row0, batch, seq) in segments:
            for gi, (_, dil) in enumerate(DIL_GROUPS):
                o, l = _attn_a_group(qkv, gi, dil, row0, batch, seq)
                oa[gi].append(o)
                lse[gi].append(l)
            ob.append(_attn_b(qkv, row0, batch, seq))
            oc.append(_attn_c(qkv, p["c_lambda"][li], p["c_head_norm"][li], lam_init, row0, batch, seq))
        cat = lambda parts: jnp.concatenate(parts, axis=0)
        merged = _merge([cat(a) for a in oa], [cat(l) for l in lse], cat(ob), cat(oc), gates,
                        p["w_branch_a"][li], p["w_branch_b"][li], p["w_branch_c"][li])
        x, hm = _out_proj(merged, p["w_out"][li], x, p["g_mix_post"][li], p["g_mlp_pre"][li])
        g_next = p["g_mix_pre"][li + 1] if li + 1 < DEPTH else None
        x, h = _mlp(hm, p["w_mlp_in"][li], p["w_mlp_out"][li], x, p["g_mlp_post"][li], g_next)
    return x


def kernel(x_prompt, x_sample, g_mix_pre, w_in, b_q_norm, b_k_norm, c_lambda, c_head_norm,
           w_branch_a, w_branch_b, w_branch_c, w_out, g_mix_post,
           g_mlp_pre, w_mlp_in, w_mlp_out, g_mlp_post):
    bp, sp, _ = x_prompt.shape
    bs, ss, _ = x_sample.shape
    segments = ((0, bp, sp), (bp * sp, bs, ss))
    x = jnp.concatenate([x_prompt.reshape(bp * sp, D_MODEL), x_sample.reshape(bs * ss, D_MODEL)], axis=0)
    p = dict(g_mix_pre=g_mix_pre.astype(F32), w_in=w_in.astype(BF16), b_q_norm=b_q_norm, b_k_norm=b_k_norm,
             c_lambda=c_lambda, c_head_norm=c_head_norm,
             w_branch_a=w_branch_a.astype(BF16), w_branch_b=w_branch_b.astype(BF16),
             w_branch_c=w_branch_c.astype(BF16), w_out=w_out.astype(BF16), g_mix_post=g_mix_post,
             g_mlp_pre=g_mlp_pre, w_mlp_in=w_mlp_in.astype(BF16), w_mlp_out=w_mlp_out.astype(BF16),
             g_mlp_post=g_mlp_post)
    y = _trunk(x, segments, p)
    return (y[:bp * sp].reshape(bp, sp, D_MODEL), y[bp * sp:].reshape(bs, ss, D_MODEL))
```

```python
import functools
import math

import jax
import jax.numpy as jnp
from jax import lax
from jax.experimental import pallas as pl
from jax.experimental.pallas import tpu as pltpu

F32 = jnp.float32
BF16 = jnp.bfloat16

D_MODEL = 2048
DEPTH = 2
HEAD_DIM = 128
GRID_W = 64
ROPE_THETA = 10000.0
EPS = 1e-6
NEG_INF = -1e30
LOG2E = math.log2(math.e)

DIL_GROUPS = ((128, 1), (512, 4), (2048, 16))
A_HALF = 64
A_W = 768
B_Q_HEADS = 6
B_KV_HEADS = 2
B_GROUP = B_Q_HEADS // B_KV_HEADS
B_QW = 768
B_KVW = 256
AXIAL_DIM = 64
C_HEADS = 4
C_QK_DIM = 64
C_QW = 512
C_VW = 512
N_BRANCH = 3
QKV_COLS = 3 * A_W + B_QW + 2 * B_KVW + 2 * C_QW + C_VW
GATE_COLS = N_BRANCH * D_MODEL
D_FF = 4 * D_MODEL

OFF_AQ, OFF_AK, OFF_AV = 0, 768, 1536
OFF_BQ, OFF_BK, OFF_BV = 2304, 3072, 3328
OFF_CQ, OFF_CK, OFF_CV = 3584, 4096, 4608

SUB = 256
_QKV_KINDS = (("aq",) * 3 + ("ak",) * 3 + ("v",) * 3 + ("bq",) * 3 + ("bk",) + ("v",)
              + ("cq",) * 2 + ("ck",) * 2 + ("v",) * 2)

VMEM_LIMIT = 56 * 1024 * 1024


def _cparams(sem):
    return pltpu.CompilerParams(dimension_semantics=sem, vmem_limit_bytes=VMEM_LIMIT)


def _rms_rows(xf, g):
    ms = jnp.mean(xf * xf, axis=-1, keepdims=True)
    return xf * lax.rsqrt(ms + EPS) * g


def _rope_half64(x, cos, sin_signed):
    return x * cos + pltpu.roll(x, 64, axis=1) * sin_signed


def _rope_half32(x, cos, sin_lo, sin_hi):
    return x * cos + pltpu.roll(x, 96, axis=1) * sin_lo + pltpu.roll(x, 32, axis=1) * sin_hi


def _norm_kernel(x_ref, g_ref, h_ref):
    h_ref[...] = _rms_rows(x_ref[...], g_ref[...]).astype(h_ref.dtype)


def _prenorm(x, g, tm=1024):
    t = x.shape[0]
    return pl.pallas_call(
        _norm_kernel,
        grid=(t // tm,),
        in_specs=[pl.BlockSpec((tm, D_MODEL), lambda i: (i, 0)),
                  pl.BlockSpec((1, D_MODEL), lambda i: (0, 0))],
        out_specs=pl.BlockSpec((tm, D_MODEL), lambda i: (i, 0)),
        out_shape=jax.ShapeDtypeStruct((t, D_MODEL), BF16),
        compiler_params=_cparams(("parallel",)),
        name="prenorm",
    )(x, g.reshape(1, D_MODEL))


_STREAM_TILES = {1: (1, 4, 7), 2: (2, 5, 8)}


def _qkv_kernel(h_ref, w_ref, tab_ref, gq_ref, gk_ref, o_ref, s1_ref, s2_ref, epi_scr, *, tm):
    j = pl.program_id(1)
    acc = jnp.dot(h_ref[...], w_ref[...], preferred_element_type=F32)

    def head_epilogue(x, kind):
        if kind in ("aq", "ak"):
            y = _rope_half64(x, tab_ref[0], tab_ref[1])
            return y * (HEAD_DIM ** -0.5 * LOG2E) if kind == "aq" else y
        if kind in ("bq", "bk"):
            g = gq_ref[...] if kind == "bq" else gk_ref[...]
            y = _rope_half32(_rms_rows(x, g), tab_ref[2], tab_ref[3], tab_ref[4])
            return y * (HEAD_DIM ** -0.5 * LOG2E) if kind == "bq" else y
        if kind in ("cq", "ck"):
            y = _rope_half32(x, tab_ref[5], tab_ref[6], tab_ref[7])
            return y * (C_QK_DIM ** -0.5 * LOG2E) if kind == "cq" else y
        return x

    for jj, kind in enumerate(_QKV_KINDS):
        @pl.when(j == jj)
        def _():
            stream = [(g, ref) for g, ref in ((1, s1_ref), (2, s2_ref)) if jj in _STREAM_TILES[g]]
            for hh in range(SUB // 128):
                sl = slice(hh * 128, (hh + 1) * 128)
                y = head_epilogue(acc[:, sl], kind)
                o_ref[:, sl] = y.astype(o_ref.dtype)
                if stream:
                    epi_scr[hh] = y
            for g, ref in stream:
                dil = DIL_GROUPS[g][1]
                for r in range(dil):
                    for hh in range(SUB // 128):
                        rows_r = epi_scr[hh, pl.ds(r, tm // dil, stride=dil), :]
                        ref[r, :, hh * 128:(hh + 1) * 128] = rows_r.astype(ref.dtype)


def _qkv_proj(h, w_qkv, tabs, gq, gk, batch, seq, tm=1024):
    t = h.shape[0]
    nseq = seq // tm
    d1, d2 = DIL_GROUPS[1][1], DIL_GROUPS[2][1]

    def stream_spec(g, dil):
        q_t, k_t, _ = _STREAM_TILES[g]
        col = lambda j: jnp.where(j <= q_t, 0, jnp.where(j <= k_t, 1, 2))
        return pl.BlockSpec((None, dil, tm // dil, SUB), lambda i, j: (i // nseq, 0, i % nseq, col(j)))

    return pl.pallas_call(
        functools.partial(_qkv_kernel, tm=tm),
        grid=(t // tm, QKV_COLS // SUB),
        in_specs=[pl.BlockSpec((tm, D_MODEL), lambda i, j: (i, 0)),
                  pl.BlockSpec((D_MODEL, SUB), lambda i, j: (0, j)),
                  pl.BlockSpec((8, tm, 128), lambda i, j: (0, i % nseq, 0)),
                  pl.BlockSpec((1, 128), lambda i, j: (0, 0)),
                  pl.BlockSpec((1, 128), lambda i, j: (0, 0))],
        out_specs=[pl.BlockSpec((tm, SUB), lambda i, j: (i, j)), stream_spec(1, d1), stream_spec(2, d2)],
        out_shape=[jax.ShapeDtypeStruct((t, QKV_COLS), BF16),
                   jax.ShapeDtypeStruct((batch, d1, seq // d1, 3 * SUB), BF16),
                   jax.ShapeDtypeStruct((batch, d2, seq // d2, 3 * SUB), BF16)],
        scratch_shapes=[pltpu.VMEM((SUB // 128, tm, 128), F32)],
        compiler_params=_cparams(("parallel", "arbitrary")),
        name="qkv_proj",
    )(h, w_qkv, tabs, gq.reshape(1, 128), gk.reshape(1, 128))


def _gate_kernel(h_ref, w_ref, o_ref):
    acc = jnp.dot(h_ref[...], w_ref[...], preferred_element_type=F32)
    o_ref[...] = jax.nn.sigmoid(acc).astype(o_ref.dtype)


def _gate_proj(h, w_in, tm=1024, tn=1024):
    t = h.shape[0]
    return pl.pallas_call(
        _gate_kernel,
        grid=(t // tm, GATE_COLS // tn),
        in_specs=[pl.BlockSpec((tm, D_MODEL), lambda i, j: (i, 0)),
                  pl.BlockSpec((D_MODEL, tn), lambda i, j: (0, QKV_COLS // tn + j))],
        out_specs=pl.BlockSpec((tm, tn), lambda i, j: (i, j)),
        out_shape=jax.ShapeDtypeStruct((t, GATE_COLS), BF16),
        compiler_params=_cparams(("parallel", "arbitrary")),
        name="gate_proj",
    )(h, w_in)


A_TQ = 128
A_TK = A_TQ + 2 * A_HALF


def _attn_a_kernel(q_ref, k_ref, v_ref, o_ref, lse_ref, *, length):
    nt = length // A_TQ
    rows = lax.broadcasted_iota(jnp.int32, (A_TQ, A_TK), 0)
    cols = lax.broadcasted_iota(jnp.int32, (A_TQ, A_TK), 1)

    def body(i, carry):
        q0 = pl.multiple_of(i * A_TQ, A_TQ)
        k0 = pl.multiple_of(jnp.clip(q0 - A_HALF, 0, length - A_TK), 16)
        valid = jnp.abs(rows - cols + (q0 - k0)) <= A_HALF
        for hh in range(2):
            sl = slice(hh * 128, (hh + 1) * 128)
            q = q_ref[pl.ds(q0, A_TQ), sl]
            k = k_ref[pl.ds(k0, A_TK), sl]
            v = v_ref[pl.ds(k0, A_TK), sl]
            s = lax.dot_general(q, k, (((1,), (1,)), ((), ())), preferred_element_type=F32)
            s = jnp.where(valid, s, NEG_INF)
            m = jnp.max(s, axis=1, keepdims=True)
            e = jnp.exp2(s - m)
            den = jnp.sum(e, axis=1, keepdims=True)
            o = jnp.dot(e.astype(BF16), v, preferred_element_type=F32) / den
            o_ref[pl.ds(q0, A_TQ), sl] = o.astype(o_ref.dtype)
            lse_ref[pl.ds(q0, A_TQ), sl] = jnp.broadcast_to(m + jnp.log2(den), (A_TQ, 128))
        return carry

    lax.fori_loop(0, nt, body, 0)


def _attn_a_group(src, gi, batch, seq):
    dil = DIL_GROUPS[gi][1]
    length = seq // dil
    kern = functools.partial(_attn_a_kernel, length=length)
    if dil == 1:
        spec = lambda c: pl.BlockSpec((seq, SUB), lambda b: (b, c))
        out_spec = pl.BlockSpec((seq, SUB), lambda b: (b, 0))
        return pl.pallas_call(
            kern, grid=(batch,),
            in_specs=[spec(OFF_AQ // SUB + gi), spec(OFF_AK // SUB + gi), spec(OFF_AV // SUB + gi)],
            out_specs=[out_spec, out_spec],
            out_shape=[jax.ShapeDtypeStruct((batch * seq, SUB), BF16),
                       jax.ShapeDtypeStruct((batch * seq, SUB), F32)],
            compiler_params=_cparams(("parallel",)),
            name=f"attn_a{gi}",
        )(src, src, src)
    spec = lambda c: pl.BlockSpec((None, None, length, SUB), lambda b, r: (b, r, 0, c))
    out_spec = pl.BlockSpec((None, None, length, SUB), lambda b, r: (b, r, 0, 0))
    return pl.pallas_call(
        kern, grid=(batch, dil),
        in_specs=[spec(0), spec(1), spec(2)],
        out_specs=[out_spec, out_spec],
        out_shape=[jax.ShapeDtypeStruct((batch, dil, length, SUB), BF16),
                   jax.ShapeDtypeStruct((batch, dil, length, SUB), F32)],
        compiler_params=_cparams(("parallel", "parallel")),
        name=f"attn_a{gi}",
    )(src, src, src)


def _transpose_values(v_ref, vt_scr, tk):
    for t in range(vt_scr.shape[0]):
        vt_scr[t] = v_ref[t * tk:(t + 1) * tk, :].astype(F32).T.astype(vt_scr.dtype)


def _flash_keymajor(q, k_ref, vt_scr, s0, s1, tk):
    rows = q.shape[0]
    nk = vt_scr.shape[0]

    def scores(t):
        k = k_ref[t * tk:(t + 1) * tk, :]
        return lax.dot_general(k, q, (((1,), (1,)), ((), ())), preferred_element_type=F32)

    def update(carry, s, t):
        m, l, acc = carry
        m_new = jnp.maximum(m, jnp.max(s, axis=0, keepdims=True))
        alpha = jnp.exp2(m - m_new)
        p = jnp.exp2(s - m_new)
        l = alpha * l + jnp.sum(p, axis=0, keepdims=True)
        acc = alpha * acc + jnp.dot(vt_scr[t], p.astype(BF16), preferred_element_type=F32)
        return m_new, l, acc

    carry = (jnp.full((1, rows), NEG_INF, F32), jnp.zeros((1, rows), F32),
             jnp.zeros((HEAD_DIM, rows), F32))
    bufs = (s0, s1)
    s0[...] = scores(0)
    for t in range(nk):
        if t + 1 < nk:
            bufs[(t + 1) % 2][...] = scores(t + 1)
        carry = update(carry, bufs[t % 2][...], t)
    _, l, acc = carry
    return acc, l


def _attn_b_kernel(q_ref, k_ref, v_ref, o_ref, vt_scr, s0, s1, *, tq, tk):
    @pl.when(pl.program_id(2) == 0)
    def _():
        _transpose_values(v_ref, vt_scr, tk)

    q = jnp.concatenate([q_ref[:, h * 128:(h + 1) * 128] for h in range(B_GROUP)], axis=0)
    acc, l = _flash_keymajor(q, k_ref, vt_scr, s0, s1, tk)
    o = acc * (1.0 / l)
    for h in range(B_GROUP):
        o_ref[:, h * 128:(h + 1) * 128] = o[:, h * tq:(h + 1) * tq].T.astype(o_ref.dtype)


def _attn_b(qkv, batch, seq, tq=256, tk=1024):
    nq = seq // tq
    gw = B_GROUP * 128
    rows = B_GROUP * tq
    return pl.pallas_call(
        functools.partial(_attn_b_kernel, tq=tq, tk=tk),
        grid=(batch, B_KV_HEADS, nq),
        in_specs=[pl.BlockSpec((tq, gw), lambda b, kh, i: (b * nq + i, OFF_BQ // gw + kh)),
                  pl.BlockSpec((seq, 128), lambda b, kh, i: (b, OFF_BK // 128 + kh)),
                  pl.BlockSpec((seq, 128), lambda b, kh, i: (b, OFF_BV // 128 + kh))],
        out_specs=pl.BlockSpec((tq, gw), lambda b, kh, i: (b * nq + i, kh)),
        out_shape=jax.ShapeDtypeStruct((batch * seq, B_QW), BF16),
        scratch_shapes=[pltpu.VMEM((seq // tk, 128, tk), BF16),
                        pltpu.VMEM((tk, rows), F32), pltpu.VMEM((tk, rows), F32)],
        compiler_params=_cparams(("parallel", "parallel", "arbitrary")),
        name="attn_b",
    )(qkv, qkv, qkv)


def _attn_c_kernel(lam_ref, g_ref, q_ref, k_ref, v_ref, o_ref, vt_scr, s0, s1, *, tq, tk, lam_init):
    @pl.when(pl.program_id(2) == 0)
    def _():
        _transpose_values(v_ref, vt_scr, tk)

    lamv = lam_ref[...]
    lam = (jnp.exp(jnp.sum(lamv[0:1] * lamv[1:2], axis=1, keepdims=True))
           - jnp.exp(jnp.sum(lamv[2:3] * lamv[3:4], axis=1, keepdims=True)) + lam_init)
    q = q_ref[...]
    lane = lax.broadcasted_iota(jnp.int32, q.shape, 1)
    zero = jnp.zeros_like(q)
    q2 = jnp.concatenate([jnp.where(lane < C_QK_DIM, q, zero), jnp.where(lane >= C_QK_DIM, q, zero)], axis=0)
    acc, l = _flash_keymajor(q2, k_ref, vt_scr, s0, s1, tk)
    o = acc * (1.0 / l)
    oc = (o[:, :tq] - lam * o[:, tq:]).T
    o_ref[...] = (_rms_rows(oc, g_ref[...]) * (1.0 - lam_init)).astype(o_ref.dtype)


def _attn_c(qkv, c_lambda, c_head_norm, lam_init, batch, seq, tq=256, tk=1024):
    nq = seq // tq
    rows = 2 * tq
    return pl.pallas_call(
        functools.partial(_attn_c_kernel, tq=tq, tk=tk, lam_init=lam_init),
        grid=(batch, C_HEADS, nq),
        in_specs=[pl.BlockSpec((4, C_QK_DIM), lambda b, h, i: (0, 0)),
                  pl.BlockSpec((1, 128), lambda b, h, i: (0, 0)),
                  pl.BlockSpec((tq, 128), lambda b, h, i: (b * nq + i, OFF_CQ // 128 + h)),
                  pl.BlockSpec((seq, 128), lambda b, h, i: (b, OFF_CK // 128 + h)),
                  pl.BlockSpec((seq, 128), lambda b, h, i: (b, OFF_CV // 128 + h))],
        out_specs=pl.BlockSpec((tq, 128), lambda b, h, i: (b * nq + i, h)),
        out_shape=jax.ShapeDtypeStruct((batch * seq, C_VW), BF16),
        scratch_shapes=[pltpu.VMEM((seq // tk, 128, tk), BF16),
                        pltpu.VMEM((tk, rows), F32), pltpu.VMEM((tk, rows), F32)],
        compiler_params=_cparams(("parallel", "parallel", "arbitrary")),
        name="attn_c",
    )(c_lambda, c_head_norm.reshape(1, 128), qkv, qkv, qkv)


def _merge_kernel(oa0, oa1, oa2, ls0, ls1, ls2, ob, oc, g0, g1, g2, wa, wb, wc, out_ref,
                  oa_scr, o1_scr, o2_scr, l1_scr, l2_scr, *, tm):
    @pl.when(pl.program_id(1) == 0)
    def _():
        for g, o_src, l_src, o_dst, l_dst in ((1, oa1, ls1, o1_scr, l1_scr), (2, oa2, ls2, o2_scr, l2_scr)):
            dil = DIL_GROUPS[g][1]
            for r in range(dil):
                for hh in range(SUB // 128):
                    sl = slice(hh * 128, (hh + 1) * 128)
                    o_dst[hh, pl.ds(r, tm // dil, stride=dil), :] = o_src[r, :, sl].astype(F32)
                    l_dst[hh, pl.ds(r, tm // dil, stride=dil), :] = l_src[r, :, sl]
        for hh in range(SUB // 128):
            sl = slice(hh * 128, (hh + 1) * 128)
            l0, l1, l2 = ls0[:, sl], l1_scr[hh], l2_scr[hh]
            m = jnp.maximum(jnp.maximum(l0, l1), l2)
            e0, e1, e2 = jnp.exp2(l0 - m), jnp.exp2(l1 - m), jnp.exp2(l2 - m)
            num = e0 * oa0[:, sl].astype(F32) + e1 * o1_scr[hh] + e2 * o2_scr[hh]
            oa_scr[:, sl] = (num / (e0 + e1 + e2)).astype(oa_scr.dtype)

    pa = jnp.dot(oa_scr[...], wa[...], preferred_element_type=F32)
    pb = jnp.dot(ob[...], wb[...], preferred_element_type=F32)
    pc = jnp.dot(oc[...], wc[...], preferred_element_type=F32)
    merged = g0[...].astype(F32) * pa + g1[...].astype(F32) * pb + g2[...].astype(F32) * pc
    out_ref[...] = merged.astype(out_ref.dtype)


def _merge(oa, lse, ob, oc, gates, wa, wb, wc, seq, tm=1024, tn=512):
    t = ob.shape[0]
    nj = D_MODEL // tn
    nseq = seq // tm

    def row(w):
        return pl.BlockSpec((tm, w), lambda i, j: (i, 0))

    def stream(g):
        dil = DIL_GROUPS[g][1]
        return pl.BlockSpec((None, dil, tm // dil, SUB), lambda i, j: (i // nseq, 0, i % nseq, 0))

    def gate(br):
        return pl.BlockSpec((tm, tn), lambda i, j: (i, br * nj + j))

    def wspec(k):
        return pl.BlockSpec((k, tn), lambda i, j: (0, j))

    return pl.pallas_call(
        functools.partial(_merge_kernel, tm=tm),
        grid=(t // tm, nj),
        in_specs=[row(SUB), stream(1), stream(2), row(SUB), stream(1), stream(2),
                  row(B_QW), row(C_VW), gate(0), gate(1), gate(2),
                  wspec(SUB), wspec(B_QW), wspec(C_VW)],
        out_specs=pl.BlockSpec((tm, tn), lambda i, j: (i, j)),
        out_shape=jax.ShapeDtypeStruct((t, D_MODEL), BF16),
        scratch_shapes=[pltpu.VMEM((tm, SUB), BF16)] + [pltpu.VMEM((SUB // 128, tm, 128), F32)] * 4,
        compiler_params=_cparams(("parallel", "arbitrary")),
        name="merge",
    )(*oa, *lse, ob, oc, gates, gates, gates, wa, wb, wc)


def _out_kernel(m_ref, w_ref, x_ref, gpost_ref, gnext_ref, x1_ref, h_ref):
    y = jnp.dot(m_ref[...], w_ref[...], preferred_element_type=F32)
    x1 = x_ref[...] + _rms_rows(y, gpost_ref[...])
    x1_ref[...] = x1
    h_ref[...] = _rms_rows(x1, gnext_ref[...]).astype(h_ref.dtype)


def _out_proj(merged, w_out, x, g_post, g_next, tm=512):
    t = x.shape[0]
    row = lambda i: (i, 0)
    const = lambda i: (0, 0)
    return pl.pallas_call(
        _out_kernel,
        grid=(t // tm,),
        in_specs=[pl.BlockSpec((tm, D_MODEL), row),
                  pl.BlockSpec((D_MODEL, D_MODEL), const),
                  pl.BlockSpec((tm, D_MODEL), row),
                  pl.BlockSpec((1, D_MODEL), const),
                  pl.BlockSpec((1, D_MODEL), const)],
        out_specs=[pl.BlockSpec((tm, D_MODEL), row), pl.BlockSpec((tm, D_MODEL), row)],
        out_shape=[jax.ShapeDtypeStruct((t, D_MODEL), F32), jax.ShapeDtypeStruct((t, D_MODEL), BF16)],
        compiler_params=_cparams(("parallel",)),
        name="out_proj",
    )(merged, w_out, x, g_post.reshape(1, D_MODEL), g_next.reshape(1, D_MODEL))


def _mlp_kernel(h_ref, w1_ref, w2_ref, x_ref, gpost_ref, *rest, with_next):
    if with_next:
        gnext_ref, x2_ref, hn_ref, acc_ref = rest
    else:
        x2_ref, acc_ref = rest
    f = pl.program_id(1)
    u = jnp.dot(h_ref[...], w1_ref[...], preferred_element_type=F32)
    u = jnp.square(jnp.maximum(u, 0.0)).astype(BF16)
    part = jnp.dot(u, w2_ref[...], preferred_element_type=F32)

    @pl.when(f == 0)
    def _():
        acc_ref[...] = part

    @pl.when(f > 0)
    def _():
        acc_ref[...] += part

    @pl.when(f == pl.num_programs(1) - 1)
    def _():
        x2 = x_ref[...] + _rms_rows(acc_ref[...], gpost_ref[...])
        x2_ref[...] = x2
        if with_next:
            hn_ref[...] = _rms_rows(x2, gnext_ref[...]).astype(hn_ref.dtype)


def _mlp(h, w1, w2, x, g_post, g_next, tm=512, tf=512):
    t = x.shape[0]
    with_next = g_next is not None
    row = lambda i, f: (i, 0)
    const = lambda i, f: (0, 0)
    in_specs = [pl.BlockSpec((tm, D_MODEL), row),
                pl.BlockSpec((D_MODEL, tf), lambda i, f: (0, f)),
                pl.BlockSpec((tf, D_MODEL), lambda i, f: (f, 0)),
                pl.BlockSpec((tm, D_MODEL), row),
                pl.BlockSpec((1, D_MODEL), const)]
    args = [h, w1, w2, x, g_post.reshape(1, D_MODEL)]
    out_specs = [pl.BlockSpec((tm, D_MODEL), row)]
    out_shape = [jax.ShapeDtypeStruct((t, D_MODEL), F32)]
    if with_next:
        in_specs.append(pl.BlockSpec((1, D_MODEL), const))
        args.append(g_next.reshape(1, D_MODEL))
        out_specs.append(pl.BlockSpec((tm, D_MODEL), row))
        out_shape.append(jax.ShapeDtypeStruct((t, D_MODEL), BF16))
    res = pl.pallas_call(
        functools.partial(_mlp_kernel, with_next=with_next),
        grid=(t // tm, D_FF // tf),
        in_specs=in_specs,
        out_specs=out_specs,
        out_shape=out_shape,
        scratch_shapes=[pltpu.VMEM((tm, D_MODEL), F32)],
        compiler_params=_cparams(("parallel", "arbitrary")),
        name="mlp",
    )(*args)
    return (res[0], res[1]) if with_next else (res[0], None)


def _rope_tables(pos, dim):
    inv = 1.0 / (ROPE_THETA ** (jnp.arange(0, dim, 2, dtype=F32) / dim))
    ang = pos.astype(F32)[:, None] * inv[None, :]
    return jnp.cos(ang), jnp.sin(ang)


def _tables(seq):
    pos = jnp.arange(seq)
    ca, sa = _rope_tables(pos, HEAD_DIM)
    cos_a = jnp.concatenate([ca, ca], axis=1)
    sin_a = jnp.concatenate([-sa, sa], axis=1)
    z32 = jnp.zeros((seq, 32), F32)

    def half32(c, s):
        return jnp.concatenate([c, c], 1), jnp.concatenate([-s, z32], 1), jnp.concatenate([z32, s], 1)

    cr, sr = _rope_tables(pos // GRID_W, AXIAL_DIM)
    cc, sc = _rope_tables(pos % GRID_W, AXIAL_DIM)
    r_c, r_lo, r_hi = half32(cr, sr)
    c_c, c_lo, c_hi = half32(cc, sc)
    cos_b = jnp.concatenate([r_c, c_c], 1)
    sin_b_lo = jnp.concatenate([r_lo, c_lo], 1)
    sin_b_hi = jnp.concatenate([r_hi, c_hi], 1)
    c3, s3 = _rope_tables(pos, C_QK_DIM)
    m_c, m_lo, m_hi = half32(c3, s3)
    cos_c = jnp.concatenate([m_c, m_c], 1)
    sin_c_lo = jnp.concatenate([m_lo, m_lo], 1)
    sin_c_hi = jnp.concatenate([m_hi, m_hi], 1)
    return jnp.stack([cos_a, sin_a, cos_b, sin_b_lo, sin_b_hi, cos_c, sin_c_lo, sin_c_hi], axis=0)


def _trunk(x3, p):
    batch, seq, _ = x3.shape
    x = x3.reshape(batch * seq, D_MODEL)
    tabs = _tables(seq)
    h = _prenorm(x, p["g_mix_pre"][0])
    for li in range(DEPTH):
        lam_init = 0.8 - 0.6 * math.exp(-0.3 * li)
        w_in = p["w_in"][li]
        qkv, st1, st2 = _qkv_proj(h, w_in, tabs, p["b_q_norm"][li], p["b_k_norm"][li], batch, seq)
        gates = _gate_proj(h, w_in)
        oa, lse = zip(*[_attn_a_group(src, gi, batch, seq) for gi, src in enumerate((qkv, st1, st2))])
        ob = _attn_b(qkv, batch, seq)
        oc = _attn_c(qkv, p["c_lambda"][li], p["c_head_norm"][li], lam_init, batch, seq)
        merged = _merge(oa, lse, ob, oc, gates,
                        p["w_branch_a"][li], p["w_branch_b"][li], p["w_branch_c"][li], seq)
        x, hm = _out_proj(merged, p["w_out"][li], x, p["g_mix_post"][li], p["g_mlp_pre"][li])
        g_next = p["g_mix_pre"][li + 1] if li + 1 < DEPTH else None
        x, h = _mlp(hm, p["w_mlp_in"][li], p["w_mlp_out"][li], x, p["g_mlp_post"][li], g_next)
    return x.reshape(batch, seq, D_MODEL)


def kernel(x_prompt, x_sample, g_mix_pre, w_in, b_q_norm, b_k_norm, c_lambda, c_head_norm,
           w_branch_a, w_branch_b, w_branch_c, w_out, g_mix_post,
           g_mlp_pre, w_mlp_in, w_mlp_out, g_mlp_post):
    p = dict(g_mix_pre=g_mix_pre, w_in=w_in.astype(BF16), b_q_norm=b_q_norm, b_k_norm=b_k_norm,
             c_lambda=c_lambda, c_head_norm=c_head_norm,
             w_branch_a=w_branch_a.astype(BF16), w_branch_b=w_branch_b.astype(BF16),
             w_branch_c=w_branch_c.astype(BF16), w_out=w_out.astype(BF16), g_mix_post=g_mix_post,
             g_mlp_pre=g_mlp_pre, w_mlp_in=w_mlp_in.astype(BF16), w_mlp_out=w_mlp_out.astype(BF16),
             g_mlp_post=g_mlp_post)
    return (_trunk(x_prompt, p), _trunk(x_sample, p))
```

```python
import functools
import math

import jax
import jax.numpy as jnp
from jax import lax
from jax.experimental import pallas as pl
from jax.experimental.pallas import tpu as pltpu

F32 = jnp.float32
BF16 = jnp.bfloat16

D_MODEL = 2048
DEPTH = 2
HEAD_DIM = 128
GRID_W = 64
ROPE_THETA = 10000.0
EPS = 1e-6
NEG_INF = -1e30
LOG2E = math.log2(math.e)

DIL_GROUPS = ((128, 1), (512, 4), (2048, 16))
A_HALF = 64
A_W = 768
B_Q_HEADS = 6
B_KV_HEADS = 2
B_GROUP = B_Q_HEADS // B_KV_HEADS
B_QW = 768
B_KVW = 256
AXIAL_DIM = 64
C_HEADS = 4
C_QK_DIM = 64
C_QW = 512
C_VW = 512
N_BRANCH = 3
QKV_COLS = 3 * A_W + B_QW + 2 * B_KVW + 2 * C_QW + C_VW
GATE_COLS = N_BRANCH * D_MODEL
D_FF = 4 * D_MODEL

OFF_AQ, OFF_AK, OFF_AV = 0, 768, 1536
OFF_BQ, OFF_BK, OFF_BV = 2304, 3072, 3328
OFF_CQ, OFF_CK, OFF_CV = 3584, 4096, 4608

SUB = 256
_QKV_KINDS = (("aq",) * 3 + ("ak",) * 3 + ("v",) * 3 + ("bq",) * 3 + ("bk",) + ("v",)
              + ("cq",) * 2 + ("ck",) * 2 + ("v",) * 2)

VMEM_LIMIT = 56 * 1024 * 1024


def _cparams(sem):
    return pltpu.CompilerParams(dimension_semantics=sem, vmem_limit_bytes=VMEM_LIMIT)


def _rms_rows(xf, g):
    ms = jnp.mean(xf * xf, axis=-1, keepdims=True)
    return xf * lax.rsqrt(ms + EPS) * g


def _rope_half64(x, cos, sin_signed):
    return x * cos + pltpu.roll(x, 64, axis=1) * sin_signed


def _rope_half32(x, cos, sin_lo, sin_hi):
    return x * cos + pltpu.roll(x, 96, axis=1) * sin_lo + pltpu.roll(x, 32, axis=1) * sin_hi


def _norm_kernel(x_ref, g_ref, h_ref):
    h_ref[...] = _rms_rows(x_ref[...], g_ref[...]).astype(h_ref.dtype)


def _prenorm(x, g, tm=1024):
    t = x.shape[0]
    return pl.pallas_call(
        _norm_kernel,
        grid=(t // tm,),
        in_specs=[pl.BlockSpec((tm, D_MODEL), lambda i: (i, 0)),
                  pl.BlockSpec((1, D_MODEL), lambda i: (0, 0))],
        out_specs=pl.BlockSpec((tm, D_MODEL), lambda i: (i, 0)),
        out_shape=jax.ShapeDtypeStruct((t, D_MODEL), BF16),
        compiler_params=_cparams(("parallel",)),
        name="prenorm",
    )(x, g.reshape(1, D_MODEL))


_STREAM_TILES = {1: (1, 4, 7), 2: (2, 5, 8)}


ROW_CHUNK = 256


def _row_chunks(n):
    return [slice(r, r + ROW_CHUNK) for r in range(0, n, ROW_CHUNK)]


def _qkv_kernel(h_ref, w_ref, tab_ref, gq_ref, gk_ref, o_ref, s1_ref, s2_ref, epi_scr, *, tm):
    j = pl.program_id(1)

    def head_epilogue(x, kind, rows):
        if kind in ("aq", "ak"):
            y = _rope_half64(x, tab_ref[0, rows, :], tab_ref[1, rows, :])
            return y * (HEAD_DIM ** -0.5 * LOG2E) if kind == "aq" else y
        if kind in ("bq", "bk"):
            g = gq_ref[...] * (HEAD_DIM ** -0.5 * LOG2E) if kind == "bq" else gk_ref[...]
            return _rope_half32(_rms_rows(x, g), tab_ref[2, rows, :], tab_ref[3, rows, :], tab_ref[4, rows, :])
        if kind in ("cq", "ck"):
            y = _rope_half32(x, tab_ref[5, rows, :], tab_ref[6, rows, :], tab_ref[7, rows, :])
            return y * (C_QK_DIM ** -0.5 * LOG2E) if kind == "cq" else y
        return x

    def run(kind, stream):
        for c, rows in enumerate(_row_chunks(tm)):
            r0 = rows.start
            acc = jnp.dot(h_ref[rows, :], w_ref[...], preferred_element_type=F32)
            for hh in range(SUB // 128):
                sl = slice(hh * 128, (hh + 1) * 128)
                y = head_epilogue(acc[:, sl], kind, rows)
                o_ref[rows, sl] = y.astype(o_ref.dtype)
                if stream is not None:
                    g, ref = stream
                    dil = DIL_GROUPS[g][1]
                    nu = ROW_CHUNK // dil
                    epi_scr[hh, rows, :] = y
                    for r in range(dil):
                        rows_r = epi_scr[hh, pl.ds(r0 + r, nu, stride=dil), :]
                        ref[r, c * nu:(c + 1) * nu, sl] = rows_r.astype(ref.dtype)

    variants = {}
    for jj, kind in enumerate(_QKV_KINDS):
        g = next((g for g in (1, 2) if jj in _STREAM_TILES[g]), None)
        variants.setdefault((kind, g), []).append(jj)
    for (kind, g), tiles in variants.items():
        cond = functools.reduce(jnp.logical_or, [j == jj for jj in tiles])
        stream = None if g is None else (g, (s1_ref, s2_ref)[g - 1])
        pl.when(cond)(functools.partial(run, kind, stream))


def _qkv_proj(h, w_qkv, tabs, gq, gk, batch, seq, tm=1024):
    t = h.shape[0]
    nseq = seq // tm
    d1, d2 = DIL_GROUPS[1][1], DIL_GROUPS[2][1]

    def stream_spec(g, dil):
        q_t, k_t, _ = _STREAM_TILES[g]
        col = lambda j: jnp.where(j <= q_t, 0, jnp.where(j <= k_t, 1, 2))
        return pl.BlockSpec((None, dil, tm // dil, SUB), lambda i, j: (i // nseq, 0, i % nseq, col(j)))

    return pl.pallas_call(
        functools.partial(_qkv_kernel, tm=tm),
        grid=(t // tm, QKV_COLS // SUB),
        in_specs=[pl.BlockSpec((tm, D_MODEL), lambda i, j: (i, 0)),
                  pl.BlockSpec((D_MODEL, SUB), lambda i, j: (0, j)),
                  pl.BlockSpec((8, tm, 128), lambda i, j: (0, i % nseq, 0)),
                  pl.BlockSpec((1, 128), lambda i, j: (0, 0)),
                  pl.BlockSpec((1, 128), lambda i, j: (0, 0))],
        out_specs=[pl.BlockSpec((tm, SUB), lambda i, j: (i, j)), stream_spec(1, d1), stream_spec(2, d2)],
        out_shape=[jax.ShapeDtypeStruct((t, QKV_COLS), BF16),
                   jax.ShapeDtypeStruct((batch, d1, seq // d1, 3 * SUB), BF16),
                   jax.ShapeDtypeStruct((batch, d2, seq // d2, 3 * SUB), BF16)],
        scratch_shapes=[pltpu.VMEM((SUB // 128, tm, 128), F32)],
        compiler_params=_cparams(("parallel", "arbitrary")),
        name="qkv_proj",
    )(h, w_qkv, tabs, gq.reshape(1, 128), gk.reshape(1, 128))


def _gate_kernel(h_ref, w_ref, o_ref):
    for rows in _row_chunks(h_ref.shape[0]):
        acc = jnp.dot(h_ref[rows, :], w_ref[...], preferred_element_type=F32)
        o_ref[rows, :] = jax.nn.sigmoid(acc).astype(o_ref.dtype)


def _gate_proj(h, w_in, tm=1024, tn=1024):
    t = h.shape[0]
    return pl.pallas_call(
        _gate_kernel,
        grid=(t // tm, GATE_COLS // tn),
        in_specs=[pl.BlockSpec((tm, D_MODEL), lambda i, j: (i, 0)),
                  pl.BlockSpec((D_MODEL, tn), lambda i, j: (0, QKV_COLS // tn + j))],
        out_specs=pl.BlockSpec((tm, tn), lambda i, j: (i, j)),
        out_shape=jax.ShapeDtypeStruct((t, GATE_COLS), BF16),
        compiler_params=_cparams(("parallel", "arbitrary")),
        name="gate_proj",
    )(h, w_in)


A_TQ = 128
A_TK = A_TQ + 2 * A_HALF


def _attn_a_kernel(q_ref, k_ref, v_ref, o_ref, lse_ref, *, length):
    nt = length // A_TQ
    rows = lax.broadcasted_iota(jnp.int32, (A_TQ, A_TK), 0)
    cols = lax.broadcasted_iota(jnp.int32, (A_TQ, A_TK), 1)

    def body(i, carry):
        q0 = pl.multiple_of(i * A_TQ, A_TQ)
        k0 = pl.multiple_of(jnp.clip(q0 - A_HALF, 0, length - A_TK), 16)
        valid = jnp.abs(rows - cols + (q0 - k0)) <= A_HALF
        for hh in range(2):
            sl = slice(hh * 128, (hh + 1) * 128)
            q = q_ref[pl.ds(q0, A_TQ), sl]
            k = k_ref[pl.ds(k0, A_TK), sl]
            v = v_ref[pl.ds(k0, A_TK), sl]
            s = lax.dot_general(q, k, (((1,), (1,)), ((), ())), preferred_element_type=F32)
            s = jnp.where(valid, s, NEG_INF)
            m = jnp.max(s, axis=1, keepdims=True)
            e = jnp.exp2(s - m)
            den = jnp.sum(e, axis=1, keepdims=True)
            o = jnp.dot(e.astype(BF16), v, preferred_element_type=F32) / den
            o_ref[pl.ds(q0, A_TQ), sl] = o.astype(o_ref.dtype)
            lse_ref[pl.ds(q0, A_TQ), sl] = jnp.broadcast_to(m + jnp.log2(den), (A_TQ, 128))
        return carry

    lax.fori_loop(0, nt, body, 0, unroll=min(nt, 4))


def _attn_a_group(src, gi, batch, seq):
    dil = DIL_GROUPS[gi][1]
    length = seq // dil
    kern = functools.partial(_attn_a_kernel, length=length)
    if dil == 1:
        spec = lambda c: pl.BlockSpec((seq, SUB), lambda b: (b, c))
        out_spec = pl.BlockSpec((seq, SUB), lambda b: (b, 0))
        return pl.pallas_call(
            kern, grid=(batch,),
            in_specs=[spec(OFF_AQ // SUB + gi), spec(OFF_AK // SUB + gi), spec(OFF_AV // SUB + gi)],
            out_specs=[out_spec, out_spec],
            out_shape=[jax.ShapeDtypeStruct((batch * seq, SUB), BF16),
                       jax.ShapeDtypeStruct((batch * seq, SUB), F32)],
            compiler_params=_cparams(("parallel",)),
            name=f"attn_a{gi}",
        )(src, src, src)
    spec = lambda c: pl.BlockSpec((None, None, length, SUB), lambda b, r: (b, r, 0, c))
    out_spec = pl.BlockSpec((None, None, length, SUB), lambda b, r: (b, r, 0, 0))
    return pl.pallas_call(
        kern, grid=(batch, dil),
        in_specs=[spec(0), spec(1), spec(2)],
        out_specs=[out_spec, out_spec],
        out_shape=[jax.ShapeDtypeStruct((batch, dil, length, SUB), BF16),
                   jax.ShapeDtypeStruct((batch, dil, length, SUB), F32)],
        compiler_params=_cparams(("parallel", "parallel")),
        name=f"attn_a{gi}",
    )(src, src, src)


def _transpose_values(v_ref, vt_scr, tk):
    for t in range(vt_scr.shape[0]):
        vt_scr[t] = v_ref[t * tk:(t + 1) * tk, :].astype(F32).T.astype(vt_scr.dtype)


def _flash_keymajor(q, k_ref, vt_scr, s0, s1, acc_scr, tk):
    rows = q.shape[0]
    nk = vt_scr.shape[0]

    def scores(t):
        k = k_ref[t * tk:(t + 1) * tk, :]
        return lax.dot_general(k, q, (((1,), (1,)), ((), ())), preferred_element_type=F32)

    def update(carry, s, t):
        m, l = carry
        m_new = jnp.maximum(m, jnp.max(s, axis=0, keepdims=True))
        alpha = jnp.exp2(m - m_new)
        p = jnp.exp2(s - m_new)
        l = alpha * l + jnp.sum(p, axis=0, keepdims=True)
        pv = jnp.dot(vt_scr[t], p.astype(BF16), preferred_element_type=F32)
        acc_scr[...] = pv if t == 0 else alpha * acc_scr[...] + pv
        return m_new, l

    carry = (jnp.full((1, rows), NEG_INF, F32), jnp.zeros((1, rows), F32))
    bufs = (s0, s1)
    s0[...] = scores(0)
    for t in range(nk):
        if t + 1 < nk:
            bufs[(t + 1) % 2][...] = scores(t + 1)
        carry = update(carry, bufs[t % 2][...], t)
    return acc_scr[...], carry[1]


def _attn_b_kernel(q_ref, k_ref, v_ref, o_ref, vt_scr, s0, s1, acc_scr, *, tq, tk):
    @pl.when(pl.program_id(2) == 0)
    def _():
        _transpose_values(v_ref, vt_scr, tk)

    q = jnp.concatenate([q_ref[:, h * 128:(h + 1) * 128] for h in range(B_GROUP)], axis=0)
    acc, l = _flash_keymajor(q, k_ref, vt_scr, s0, s1, acc_scr, tk)
    o = acc * (1.0 / l)
    for h in range(B_GROUP):
        o_ref[:, h * 128:(h + 1) * 128] = o[:, h * tq:(h + 1) * tq].T.astype(o_ref.dtype)


def _attn_b(qkv, batch, seq, tq=256, tk=1024):
    nq = seq // tq
    gw = B_GROUP * 128
    rows = B_GROUP * tq
    return pl.pallas_call(
        functools.partial(_attn_b_kernel, tq=tq, tk=tk),
        grid=(batch, B_KV_HEADS, nq),
        in_specs=[pl.BlockSpec((tq, gw), lambda b, kh, i: (b * nq + i, OFF_BQ // gw + kh)),
                  pl.BlockSpec((seq, 128), lambda b, kh, i: (b, OFF_BK // 128 + kh)),
                  pl.BlockSpec((seq, 128), lambda b, kh, i: (b, OFF_BV // 128 + kh))],
        out_specs=pl.BlockSpec((tq, gw), lambda b, kh, i: (b * nq + i, kh)),
        out_shape=jax.ShapeDtypeStruct((batch * seq, B_QW), BF16),
        scratch_shapes=[pltpu.VMEM((seq // tk, 128, tk), BF16),
                        pltpu.VMEM((tk, rows), F32), pltpu.VMEM((tk, rows), F32),
                        pltpu.VMEM((HEAD_DIM, rows), F32)],
        compiler_params=_cparams(("parallel", "parallel", "arbitrary")),
        name="attn_b",
    )(qkv, qkv, qkv)


def _attn_c_kernel(lam_ref, g_ref, q_ref, k_ref, v_ref, o_ref, vt_scr, s0, s1, acc_scr, *, tq, tk, lam_init):
    @pl.when(pl.program_id(2) == 0)
    def _():
        _transpose_values(v_ref, vt_scr, tk)

    lamv = lam_ref[...]
    lam = (jnp.exp(jnp.sum(lamv[0:1] * lamv[1:2], axis=1, keepdims=True))
           - jnp.exp(jnp.sum(lamv[2:3] * lamv[3:4], axis=1, keepdims=True)) + lam_init)
    q = q_ref[...]
    lane = lax.broadcasted_iota(jnp.int32, q.shape, 1)
    zero = jnp.zeros_like(q)
    q2 = jnp.concatenate([jnp.where(lane < C_QK_DIM, q, zero), jnp.where(lane >= C_QK_DIM, q, zero)], axis=0)
    acc, l = _flash_keymajor(q2, k_ref, vt_scr, s0, s1, acc_scr, tk)
    o = acc * (1.0 / l)
    oc = (o[:, :tq] - lam * o[:, tq:]).T
    o_ref[...] = (_rms_rows(oc, g_ref[...]) * (1.0 - lam_init)).astype(o_ref.dtype)


def _attn_c(qkv, c_lambda, c_head_norm, lam_init, batch, seq, tq=256, tk=1024):
    nq = seq // tq
    rows = 2 * tq
    return pl.pallas_call(
        functools.partial(_attn_c_kernel, tq=tq, tk=tk, lam_init=lam_init),
        grid=(batch, C_HEADS, nq),
        in_specs=[pl.BlockSpec((4, C_QK_DIM), lambda b, h, i: (0, 0)),
                  pl.BlockSpec((1, 128), lambda b, h, i: (0, 0)),
                  pl.BlockSpec((tq, 128), lambda b, h, i: (b * nq + i, OFF_CQ // 128 + h)),
                  pl.BlockSpec((seq, 128), lambda b, h, i: (b, OFF_CK // 128 + h)),
                  pl.BlockSpec((seq, 128), lambda b, h, i: (b, OFF_CV // 128 + h))],
        out_specs=pl.BlockSpec((tq, 128), lambda b, h, i: (b * nq + i, h)),
        out_shape=jax.ShapeDtypeStruct((batch * seq, C_VW), BF16),
        scratch_shapes=[pltpu.VMEM((seq // tk, 128, tk), BF16),
                        pltpu.VMEM((tk, rows), F32), pltpu.VMEM((tk, rows), F32),
                        pltpu.VMEM((HEAD_DIM, rows), F32)],
        compiler_params=_cparams(("parallel", "parallel", "arbitrary")),
        name="attn_c",
    )(c_lambda, c_head_norm.reshape(1, 128), qkv, qkv, qkv)


def _merge_kernel(oa0, oa1, oa2, ls0, ls1, ls2, ob, oc, g0, g1, g2, wa, wb, wc, out_ref,
                  oa_scr, o1_scr, o2_scr, l1_scr, l2_scr, *, tm):
    @pl.when(pl.program_id(1) == 0)
    def _():
        for g, o_src, l_src, o_dst, l_dst in ((1, oa1, ls1, o1_scr, l1_scr), (2, oa2, ls2, o2_scr, l2_scr)):
            dil = DIL_GROUPS[g][1]
            for r in range(dil):
                for hh in range(SUB // 128):
                    sl = slice(hh * 128, (hh + 1) * 128)
                    o_dst[hh, pl.ds(r, tm // dil, stride=dil), :] = o_src[r, :, sl].astype(F32)
                    l_dst[hh, pl.ds(r, tm // dil, stride=dil), :] = l_src[r, :, sl]
        for hh in range(SUB // 128):
            sl = slice(hh * 128, (hh + 1) * 128)
            l0, l1, l2 = ls0[:, sl], l1_scr[hh], l2_scr[hh]
            m = jnp.maximum(jnp.maximum(l0, l1), l2)
            e0, e1, e2 = jnp.exp2(l0 - m), jnp.exp2(l1 - m), jnp.exp2(l2 - m)
            num = e0 * oa0[:, sl].astype(F32) + e1 * o1_scr[hh] + e2 * o2_scr[hh]
            oa_scr[:, sl] = (num / (e0 + e1 + e2)).astype(oa_scr.dtype)

    for rows in _row_chunks(tm):
        pa = jnp.dot(oa_scr[rows, :], wa[...], preferred_element_type=F32)
        pb = jnp.dot(ob[rows, :], wb[...], preferred_element_type=F32)
        pc = jnp.dot(oc[rows, :], wc[...], preferred_element_type=F32)
        merged = (g0[rows, :].astype(F32) * pa + g1[rows, :].astype(F32) * pb
                  + g2[rows, :].astype(F32) * pc)
        out_ref[rows, :] = merged.astype(out_ref.dtype)


def _merge(oa, lse, ob, oc, gates, wa, wb, wc, seq, tm=1024, tn=512):
    t = ob.shape[0]
    nj = D_MODEL // tn
    nseq = seq // tm

    def row(w):
        return pl.BlockSpec((tm, w), lambda i, j: (i, 0))

    def stream(g):
        dil = DIL_GROUPS[g][1]
        return pl.BlockSpec((None, dil, tm // dil, SUB), lambda i, j: (i // nseq, 0, i % nseq, 0))

    def gate(br):
        return pl.BlockSpec((tm, tn), lambda i, j: (i, br * nj + j))

    def wspec(k):
        return pl.BlockSpec((k, tn), lambda i, j: (0, j))

    return pl.pallas_call(
        functools.partial(_merge_kernel, tm=tm),
        grid=(t // tm, nj),
        in_specs=[row(SUB), stream(1), stream(2), row(SUB), stream(1), stream(2),
                  row(B_QW), row(C_VW), gate(0), gate(1), gate(2),
                  wspec(SUB), wspec(B_QW), wspec(C_VW)],
        out_specs=pl.BlockSpec((tm, tn), lambda i, j: (i, j)),
        out_shape=jax.ShapeDtypeStruct((t, D_MODEL), BF16),
        scratch_shapes=[pltpu.VMEM((tm, SUB), BF16)] + [pltpu.VMEM((SUB // 128, tm, 128), F32)] * 4,
        compiler_params=_cparams(("parallel", "arbitrary")),
        name="merge",
    )(*oa, *lse, ob, oc, gates, gates, gates, wa, wb, wc)


def _out_kernel(m_ref, w_ref, x_ref, gpost_ref, gnext_ref, x1_ref, h_ref):
    for rows in _row_chunks(m_ref.shape[0]):
        y = jnp.dot(m_ref[rows, :], w_ref[...], preferred_element_type=F32)
        x1 = x_ref[rows, :] + _rms_rows(y, gpost_ref[...])
        x1_ref[rows, :] = x1
        h_ref[rows, :] = _rms_rows(x1, gnext_ref[...]).astype(h_ref.dtype)


def _out_proj(merged, w_out, x, g_post, g_next, tm=512):
    t = x.shape[0]
    row = lambda i: (i, 0)
    const = lambda i: (0, 0)
    return pl.pallas_call(
        _out_kernel,
        grid=(t // tm,),
        in_specs=[pl.BlockSpec((tm, D_MODEL), row),
                  pl.BlockSpec((D_MODEL, D_MODEL), const),
                  pl.BlockSpec((tm, D_MODEL), row),
                  pl.BlockSpec((1, D_MODEL), const),
                  pl.BlockSpec((1, D_MODEL), const)],
        out_specs=[pl.BlockSpec((tm, D_MODEL), row), pl.BlockSpec((tm, D_MODEL), row)],
        out_shape=[jax.ShapeDtypeStruct((t, D_MODEL), F32), jax.ShapeDtypeStruct((t, D_MODEL), BF16)],
        compiler_params=_cparams(("parallel",)),
        name="out_proj",
    )(merged, w_out, x, g_post.reshape(1, D_MODEL), g_next.reshape(1, D_MODEL))


def _mlp_kernel(h_ref, w1_ref, w2_ref, x_ref, gpost_ref, *rest, with_next):
    if with_next:
        gnext_ref, x2_ref, hn_ref, acc_ref = rest
    else:
        x2_ref, acc_ref = rest
    f = pl.program_id(1)

    @pl.when(f == 0)
    def _():
        acc_ref[...] = jnp.zeros_like(acc_ref)

    u = jnp.dot(h_ref[...], w1_ref[...], preferred_element_type=F32)
    u = jnp.square(jnp.maximum(u, 0.0)).astype(BF16)
    acc_ref[...] += jnp.dot(u, w2_ref[...], preferred_element_type=F32)

    @pl.when(f == pl.num_programs(1) - 1)
    def _():
        x2 = x_ref[...] + _rms_rows(acc_ref[...], gpost_ref[...])
        x2_ref[...] = x2
        if with_next:
            hn_ref[...] = _rms_rows(x2, gnext_ref[...]).astype(hn_ref.dtype)


def _mlp(h, w1, w2, x, g_post, g_next, tm=512, tf=1024):
    t = x.shape[0]
    with_next = g_next is not None
    row = lambda i, f: (i, 0)
    const = lambda i, f: (0, 0)
    in_specs = [pl.BlockSpec((tm, D_MODEL), row),
                pl.BlockSpec((D_MODEL, tf), lambda i, f: (0, f)),
                pl.BlockSpec((tf, D_MODEL), lambda i, f: (f, 0)),
                pl.BlockSpec((tm, D_MODEL), row),
                pl.BlockSpec((1, D_MODEL), const)]
    args = [h, w1, w2, x, g_post.reshape(1, D_MODEL)]
    out_specs = [pl.BlockSpec((tm, D_MODEL), row)]
    out_shape = [jax.ShapeDtypeStruct((t, D_MODEL), F32)]
    if with_next:
        in_specs.append(pl.BlockSpec((1, D_MODEL), const))
        args.append(g_next.reshape(1, D_MODEL))
        out_specs.append(pl.BlockSpec((tm, D_MODEL), row))
        out_shape.append(jax.ShapeDtypeStruct((t, D_MODEL), BF16))
    res = pl.pallas_call(
        functools.partial(_mlp_kernel, with_next=with_next),
        grid=(t // tm, D_FF // tf),
        in_specs=in_specs,
        out_specs=out_specs,
        out_shape=out_shape,
        scratch_shapes=[pltpu.VMEM((tm, D_MODEL), F32)],
        compiler_params=_cparams(("parallel", "arbitrary")),
        name="mlp",
    )(*args)
    return (res[0], res[1]) if with_next else (res[0], None)


def _rope_tables(pos, dim):
    inv = 1.0 / (ROPE_THETA ** (jnp.arange(0, dim, 2, dtype=F32) / dim))
    ang = pos.astype(F32)[:, None] * inv[None, :]
    return jnp.cos(ang), jnp.sin(ang)


def _tables(seq):
    pos = jnp.arange(seq)
    ca, sa = _rope_tables(pos, HEAD_DIM)
    cos_a = jnp.concatenate([ca, ca], axis=1)
    sin_a = jnp.concatenate([-sa, sa], axis=1)
    z32 = jnp.zeros((seq, 32), F32)

    def half32(c, s):
        return jnp.concatenate([c, c], 1), jnp.concatenate([-s, z32], 1), jnp.concatenate([z32, s], 1)

    cr, sr = _rope_tables(pos // GRID_W, AXIAL_DIM)
    cc, sc = _rope_tables(pos % GRID_W, AXIAL_DIM)
    r_c, r_lo, r_hi = half32(cr, sr)
    c_c, c_lo, c_hi = half32(cc, sc)
    cos_b = jnp.concatenate([r_c, c_c], 1)
    sin_b_lo = jnp.concatenate([r_lo, c_lo], 1)
    sin_b_hi = jnp.concatenate([r_hi, c_hi], 1)
    c3, s3 = _rope_tables(pos, C_QK_DIM)
    m_c, m_lo, m_hi = half32(c3, s3)
    cos_c = jnp.concatenate([m_c, m_c], 1)
    sin_c_lo = jnp.concatenate([m_lo, m_lo], 1)
    sin_c_hi = jnp.concatenate([m_hi, m_hi], 1)
    return jnp.stack([cos_a, sin_a, cos_b, sin_b_lo, sin_b_hi, cos_c, sin_c_lo, sin_c_hi], axis=0)


def _trunk(x3, p):
    batch, seq, _ = x3.shape
    x = x3.reshape(batch * seq, D_MODEL)
    tabs = _tables(seq)
    h = _prenorm(x, p["g_mix_pre"][0])
    for li in range(DEPTH):
        lam_init = 0.8 - 0.6 * math.exp(-0.3 * li)
        w_in = p["w_in"][li]
        qkv, st1, st2 = _qkv_proj(h, w_in, tabs, p["b_q_norm"][li], p["b_k_norm"][li], batch, seq)
        gates = _gate_proj(h, w_in)
        oa, lse = zip(*[_attn_a_group(src, gi, batch, seq) for gi, src in enumerate((qkv, st1, st2))])
        ob = _attn_b(qkv, batch, seq)
        oc = _attn_c(qkv, p["c_lambda"][li], p["c_head_norm"][li], lam_init, batch, seq)
        merged = _merge(oa, lse, ob, oc, gates,
                        p["w_branch_a"][li], p["w_branch_b"][li], p["w_branch_c"][li], seq)
        x, hm = _out_proj(merged, p["w_out"][li], x, p["g_mix_post"][li], p["g_mlp_pre"][li])
        g_next = p["g_mix_pre"][li + 1] if li + 1 < DEPTH else None
        x, h = _mlp(hm, p["w_mlp_in"][li], p["w_mlp_out"][li], x, p["g_mlp_post"][li], g_next)
    return x.reshape(batch, seq, D_MODEL)


def kernel(x_prompt, x_sample, g_mix_pre, w_in, b_q_norm, b_k_norm, c_lambda, c_head_norm,
           w_branch_a, w_branch_b, w_branch_c, w_out, g_mix_post,
           g_mlp_pre, w_mlp_in, w_mlp_out, g_mlp_post):
    p = dict(g_mix_pre=g_mix_pre, w_in=w_in.astype(BF16), b_q_norm=b_q_norm, b_k_norm=b_k_norm,
             c_lambda=c_lambda, c_head_norm=c_head_norm,
             w_branch_a=w_branch_a.astype(BF16), w_branch_b=w_branch_b.astype(BF16),
             w_branch_c=w_branch_c.astype(BF16), w_out=w_out.astype(BF16), g_mix_post=g_mix_post,
             g_mlp_pre=g_mlp_pre, w_mlp_in=w_mlp_in.astype(BF16), w_mlp_out=w_mlp_out.astype(BF16),
             g_mlp_post=g_mlp_post)
    return (_trunk(x_prompt, p), _trunk(x_sample, p))
```

```python
import functools
import math

import jax
import jax.numpy as jnp
from jax import lax
from jax.experimental import pallas as pl
from jax.experimental.pallas import tpu as pltpu

F32 = jnp.float32
BF16 = jnp.bfloat16

D_MODEL = 2048
DEPTH = 2
HEAD_DIM = 128
GRID_W = 64
ROPE_THETA = 10000.0
EPS = 1e-6
NEG_INF = -1e30
LOG2E = math.log2(math.e)

DIL_GROUPS = ((128, 1), (512, 4), (2048, 16))
A_HALF = 64
A_W = 768
B_Q_HEADS = 6
B_KV_HEADS = 2
B_GROUP = B_Q_HEADS // B_KV_HEADS
B_QW = 768
B_KVW = 256
AXIAL_DIM = 64
C_HEADS = 4
C_QK_DIM = 64
C_QW = 512
C_VW = 512
N_BRANCH = 3
QKV_COLS = 3 * A_W + B_QW + 2 * B_KVW + 2 * C_QW + C_VW
GATE_COLS = N_BRANCH * D_MODEL
D_FF = 4 * D_MODEL

OFF_AQ, OFF_AK, OFF_AV = 0, 768, 1536
OFF_BQ, OFF_BK, OFF_BV = 2304, 3072, 3328
OFF_CQ, OFF_CK, OFF_CV = 3584, 4096, 4608

SUB = 256
_QKV_KINDS = (("aq",) * 3 + ("ak",) * 3 + ("v",) * 3 + ("bq",) * 3 + ("bk",) + ("v",)
              + ("cq",) * 2 + ("ck",) * 2 + ("v",) * 2)

VMEM_LIMIT = 56 * 1024 * 1024


def _cparams(sem):
    return pltpu.CompilerParams(dimension_semantics=sem, vmem_limit_bytes=VMEM_LIMIT)


def _rms_rows(xf, g):
    ms = jnp.mean(xf * xf, axis=-1, keepdims=True)
    return xf * lax.rsqrt(ms + EPS) * g


def _rope_half64(x, cos, sin_signed):
    return x * cos + pltpu.roll(x, 64, axis=1) * sin_signed


def _rope_half32(x, cos, sin_lo, sin_hi):
    return x * cos + pltpu.roll(x, 96, axis=1) * sin_lo + pltpu.roll(x, 32, axis=1) * sin_hi


def _norm_kernel(x_ref, g_ref, h_ref):
    h_ref[...] = _rms_rows(x_ref[...], g_ref[...]).astype(h_ref.dtype)


def _prenorm(x, g, tm=1024):
    t = x.shape[0]
    return pl.pallas_call(
        _norm_kernel,
        grid=(t // tm,),
        in_specs=[pl.BlockSpec((tm, D_MODEL), lambda i: (i, 0)),
                  pl.BlockSpec((1, D_MODEL), lambda i: (0, 0))],
        out_specs=pl.BlockSpec((tm, D_MODEL), lambda i: (i, 0)),
        out_shape=jax.ShapeDtypeStruct((t, D_MODEL), BF16),
        compiler_params=_cparams(("parallel",)),
        name="prenorm",
    )(x, g.reshape(1, D_MODEL))


_STREAM_TILES = {1: (1, 4, 7), 2: (2, 5, 8)}


ROW_CHUNK = 256


def _row_chunks(n):
    return [slice(r, r + ROW_CHUNK) for r in range(0, n, ROW_CHUNK)]


def _qkv_kernel(h_ref, w_ref, tab_ref, gq_ref, gk_ref, o_ref, s1_ref, s2_ref, epi_scr, *, tm):
    def head_epilogue(x, kind, rows):
        if kind in ("aq", "ak"):
            y = _rope_half64(x, tab_ref[0, rows, :], tab_ref[1, rows, :])
            return y * (HEAD_DIM ** -0.5 * LOG2E) if kind == "aq" else y
        if kind in ("bq", "bk"):
            g = gq_ref[...] * (HEAD_DIM ** -0.5 * LOG2E) if kind == "bq" else gk_ref[...]
            return _rope_half32(_rms_rows(x, g), tab_ref[2, rows, :], tab_ref[3, rows, :], tab_ref[4, rows, :])
        if kind in ("cq", "ck"):
            y = _rope_half32(x, tab_ref[5, rows, :], tab_ref[6, rows, :], tab_ref[7, rows, :])
            return y * (C_QK_DIM ** -0.5 * LOG2E) if kind == "cq" else y
        return x

    n_heads = SUB // 128
    for jj, kind in enumerate(_QKV_KINDS):
        g = next((g for g in (1, 2) if jj in _STREAM_TILES[g]), None)
        for c, rows in enumerate(_row_chunks(tm)):
            acc = jnp.dot(h_ref[rows, :], w_ref[:, jj * SUB:(jj + 1) * SUB], preferred_element_type=F32)
            for hh in range(n_heads):
                sl = slice(hh * 128, (hh + 1) * 128)
                y = head_epilogue(acc[:, sl], kind, rows)
                o_ref[rows, jj * SUB + hh * 128:jj * SUB + (hh + 1) * 128] = y.astype(o_ref.dtype)
                if g is not None:
                    ref = (s1_ref, s2_ref)[g - 1]
                    dil = DIL_GROUPS[g][1]
                    nu = ROW_CHUNK // dil
                    slot = (jj % 2) * n_heads + hh
                    col0 = _STREAM_TILES[g].index(jj) * SUB + hh * 128
                    epi_scr[slot, rows, :] = y
                    for r in range(dil):
                        rows_r = epi_scr[slot, pl.ds(rows.start + r, nu, stride=dil), :]
                        ref[r, c * nu:(c + 1) * nu, col0:col0 + 128] = rows_r.astype(ref.dtype)


def _qkv_proj(h, w_in, tabs, gq, gk, batch, seq, tm=512):
    t = h.shape[0]
    nseq = seq // tm
    d1, d2 = DIL_GROUPS[1][1], DIL_GROUPS[2][1]

    def stream_spec(dil):
        return pl.BlockSpec((None, dil, tm // dil, 3 * SUB), lambda i: (i // nseq, 0, i % nseq, 0))

    return pl.pallas_call(
        functools.partial(_qkv_kernel, tm=tm),
        grid=(t // tm,),
        in_specs=[pl.BlockSpec((tm, D_MODEL), lambda i: (i, 0)),
                  pl.BlockSpec((D_MODEL, QKV_COLS), lambda i: (0, 0), pipeline_mode=pl.Buffered(1)),
                  pl.BlockSpec((8, tm, 128), lambda i: (0, i % nseq, 0)),
                  pl.BlockSpec((1, 128), lambda i: (0, 0)),
                  pl.BlockSpec((1, 128), lambda i: (0, 0))],
        out_specs=[pl.BlockSpec((tm, QKV_COLS), lambda i: (i, 0)), stream_spec(d1), stream_spec(d2)],
        out_shape=[jax.ShapeDtypeStruct((t, QKV_COLS), BF16),
                   jax.ShapeDtypeStruct((batch, d1, seq // d1, 3 * SUB), BF16),
                   jax.ShapeDtypeStruct((batch, d2, seq // d2, 3 * SUB), BF16)],
        scratch_shapes=[pltpu.VMEM((2 * (SUB // 128), tm, 128), F32)],
        compiler_params=_cparams(("parallel",)),
        name="qkv_proj",
    )(h, w_in, tabs, gq.reshape(1, 128), gk.reshape(1, 128))


A_TQ = 128
A_TK = A_TQ + 2 * A_HALF


def _attn_a_kernel(q_ref, k_ref, v_ref, o_ref, lse_ref, *, length):
    nt = length // A_TQ
    rows = lax.broadcasted_iota(jnp.int32, (A_TQ, A_TK), 0)
    cols = lax.broadcasted_iota(jnp.int32, (A_TQ, A_TK), 1)

    def body(i, carry):
        q0 = pl.multiple_of(i * A_TQ, A_TQ)
        k0 = pl.multiple_of(jnp.clip(q0 - A_HALF, 0, length - A_TK), 16)
        valid = jnp.abs(rows - cols + (q0 - k0)) <= A_HALF
        for hh in range(2):
            sl = slice(hh * 128, (hh + 1) * 128)
            q = q_ref[pl.ds(q0, A_TQ), sl]
            k = k_ref[pl.ds(k0, A_TK), sl]
            v = v_ref[pl.ds(k0, A_TK), sl]
            s = lax.dot_general(q, k, (((1,), (1,)), ((), ())), preferred_element_type=F32)
            s = jnp.where(valid, s, NEG_INF)
            m = jnp.max(s, axis=1, keepdims=True)
            e = jnp.exp2(s - m)
            den = jnp.sum(e, axis=1, keepdims=True)
            o = jnp.dot(e.astype(BF16), v, preferred_element_type=F32) / den
            o_ref[pl.ds(q0, A_TQ), sl] = o.astype(o_ref.dtype)
            lse_ref[pl.ds(q0, A_TQ), sl] = jnp.broadcast_to(m + jnp.log2(den), (A_TQ, 128))
        return carry

    lax.fori_loop(0, nt, body, 0, unroll=min(nt, 4))


def _attn_a_group(src, gi, batch, seq):
    dil = DIL_GROUPS[gi][1]
    length = seq // dil
    kern = functools.partial(_attn_a_kernel, length=length)
    if dil == 1:
        spec = lambda c: pl.BlockSpec((seq, SUB), lambda b: (b, c))
        out_spec = pl.BlockSpec((seq, SUB), lambda b: (b, 0))
        return pl.pallas_call(
            kern, grid=(batch,),
            in_specs=[spec(OFF_AQ // SUB + gi), spec(OFF_AK // SUB + gi), spec(OFF_AV // SUB + gi)],
            out_specs=[out_spec, out_spec],
            out_shape=[jax.ShapeDtypeStruct((batch * seq, SUB), BF16),
                       jax.ShapeDtypeStruct((batch * seq, SUB), F32)],
            compiler_params=_cparams(("parallel",)),
            name=f"attn_a{gi}",
        )(src, src, src)
    spec = lambda c: pl.BlockSpec((None, None, length, SUB), lambda b, r: (b, r, 0, c))
    out_spec = pl.BlockSpec((None, None, length, SUB), lambda b, r: (b, r, 0, 0))
    return pl.pallas_call(
        kern, grid=(batch, dil),
        in_specs=[spec(0), spec(1), spec(2)],
        out_specs=[out_spec, out_spec],
        out_shape=[jax.ShapeDtypeStruct((batch, dil, length, SUB), BF16),
                   jax.ShapeDtypeStruct((batch, dil, length, SUB), F32)],
        compiler_params=_cparams(("parallel", "parallel")),
        name=f"attn_a{gi}",
    )(src, src, src)


def _transpose_values(v_ref, vt_scr, tk):
    for t in range(vt_scr.shape[0]):
        vt_scr[t] = v_ref[t * tk:(t + 1) * tk, :].astype(F32).T.astype(vt_scr.dtype)


def _flash_keymajor(q, k_ref, vt_scr, s0, s1, acc_scr, tk):
    rows = q.shape[0]
    nk = vt_scr.shape[0]

    def scores(t):
        k = k_ref[t * tk:(t + 1) * tk, :]
        return lax.dot_general(k, q, (((1,), (1,)), ((), ())), preferred_element_type=F32)

    def update(carry, s, t):
        m, l = carry
        m_new = jnp.maximum(m, jnp.max(s, axis=0, keepdims=True))
        alpha = jnp.exp2(m - m_new)
        p = jnp.exp2(s - m_new)
        l = alpha * l + jnp.sum(p, axis=0, keepdims=True)
        pv = jnp.dot(vt_scr[t], p.astype(BF16), preferred_element_type=F32)
        acc_scr[...] = pv if t == 0 else alpha * acc_scr[...] + pv
        return m_new, l

    carry = (jnp.full((1, rows), NEG_INF, F32), jnp.zeros((1, rows), F32))
    bufs = (s0, s1)
    s0[...] = scores(0)
    for t in range(nk):
        if t + 1 < nk:
            bufs[(t + 1) % 2][...] = scores(t + 1)
        carry = update(carry, bufs[t % 2][...], t)
    return acc_scr[...], carry[1]


def _attn_b_kernel(q_ref, k_ref, v_ref, o_ref, vt_scr, s0, s1, acc_scr, *, tq, tk):
    @pl.when(pl.program_id(2) == 0)
    def _():
        _transpose_values(v_ref, vt_scr, tk)

    q = jnp.concatenate([q_ref[:, h * 128:(h + 1) * 128] for h in range(B_GROUP)], axis=0)
    acc, l = _flash_keymajor(q, k_ref, vt_scr, s0, s1, acc_scr, tk)
    o = acc * (1.0 / l)
    for h in range(B_GROUP):
        o_ref[:, h * 128:(h + 1) * 128] = o[:, h * tq:(h + 1) * tq].T.astype(o_ref.dtype)


def _attn_b(qkv, batch, seq, tq=256, tk=1024):
    nq = seq // tq
    gw = B_GROUP * 128
    rows = B_GROUP * tq
    return pl.pallas_call(
        functools.partial(_attn_b_kernel, tq=tq, tk=tk),
        grid=(batch, B_KV_HEADS, nq),
        in_specs=[pl.BlockSpec((tq, gw), lambda b, kh, i: (b * nq + i, OFF_BQ // gw + kh)),
                  pl.BlockSpec((seq, 128), lambda b, kh, i: (b, OFF_BK // 128 + kh)),
                  pl.BlockSpec((seq, 128), lambda b, kh, i: (b, OFF_BV // 128 + kh))],
        out_specs=pl.BlockSpec((tq, gw), lambda b, kh, i: (b * nq + i, kh)),
        out_shape=jax.ShapeDtypeStruct((batch * seq, B_QW), BF16),
        scratch_shapes=[pltpu.VMEM((seq // tk, 128, tk), BF16),
                        pltpu.VMEM((tk, rows), F32), pltpu.VMEM((tk, rows), F32),
                        pltpu.VMEM((HEAD_DIM, rows), F32)],
        compiler_params=_cparams(("parallel", "parallel", "arbitrary")),
        name="attn_b",
    )(qkv, qkv, qkv)


def _attn_c_kernel(lam_ref, g_ref, q_ref, k_ref, v_ref, o_ref, vt_scr, s0, s1, acc_scr, *, tq, tk, lam_init):
    @pl.when(pl.program_id(2) == 0)
    def _():
        _transpose_values(v_ref, vt_scr, tk)

    lamv = lam_ref[...]
    lam = (jnp.exp(jnp.sum(lamv[0:1] * lamv[1:2], axis=1, keepdims=True))
           - jnp.exp(jnp.sum(lamv[2:3] * lamv[3:4], axis=1, keepdims=True)) + lam_init)
    q = q_ref[...]
    lane = lax.broadcasted_iota(jnp.int32, q.shape, 1)
    zero = jnp.zeros_like(q)
    q2 = jnp.concatenate([jnp.where(lane < C_QK_DIM, q, zero), jnp.where(lane >= C_QK_DIM, q, zero)], axis=0)
    acc, l = _flash_keymajor(q2, k_ref, vt_scr, s0, s1, acc_scr, tk)
    o = acc * (1.0 / l)
    oc = (o[:, :tq] - lam * o[:, tq:]).T
    o_ref[...] = (_rms_rows(oc, g_ref[...]) * (1.0 - lam_init)).astype(o_ref.dtype)


def _attn_c(qkv, c_lambda, c_head_norm, lam_init, batch, seq, tq=256, tk=1024):
    nq = seq // tq
    rows = 2 * tq
    return pl.pallas_call(
        functools.partial(_attn_c_kernel, tq=tq, tk=tk, lam_init=lam_init),
        grid=(batch, C_HEADS, nq),
        in_specs=[pl.BlockSpec((4, C_QK_DIM), lambda b, h, i: (0, 0)),
                  pl.BlockSpec((1, 128), lambda b, h, i: (0, 0)),
                  pl.BlockSpec((tq, 128), lambda b, h, i: (b * nq + i, OFF_CQ // 128 + h)),
                  pl.BlockSpec((seq, 128), lambda b, h, i: (b, OFF_CK // 128 + h)),
                  pl.BlockSpec((seq, 128), lambda b, h, i: (b, OFF_CV // 128 + h))],
        out_specs=pl.BlockSpec((tq, 128), lambda b, h, i: (b * nq + i, h)),
        out_shape=jax.ShapeDtypeStruct((batch * seq, C_VW), BF16),
        scratch_shapes=[pltpu.VMEM((seq // tk, 128, tk), BF16),
                        pltpu.VMEM((tk, rows), F32), pltpu.VMEM((tk, rows), F32),
                        pltpu.VMEM((HEAD_DIM, rows), F32)],
        compiler_params=_cparams(("parallel", "parallel", "arbitrary")),
        name="attn_c",
    )(c_lambda, c_head_norm.reshape(1, 128), qkv, qkv, qkv)


def _merge_kernel(h_ref, wg0, wg1, wg2, oa0, oa1, oa2, ls0, ls1, ls2, ob, oc, wa, wb, wc, out_ref,
                  oa_scr, o1_scr, o2_scr, l1_scr, l2_scr, *, tm):
    @pl.when(pl.program_id(1) == 0)
    def _():
        for g, o_src, l_src, o_dst, l_dst in ((1, oa1, ls1, o1_scr, l1_scr), (2, oa2, ls2, o2_scr, l2_scr)):
            dil = DIL_GROUPS[g][1]
            for r in range(dil):
                for hh in range(SUB // 128):
                    sl = slice(hh * 128, (hh + 1) * 128)
                    o_dst[hh, pl.ds(r, tm // dil, stride=dil), :] = o_src[r, :, sl].astype(F32)
                    l_dst[hh, pl.ds(r, tm // dil, stride=dil), :] = l_src[r, :, sl]
        for hh in range(SUB // 128):
            sl = slice(hh * 128, (hh + 1) * 128)
            l0, l1, l2 = ls0[:, sl], l1_scr[hh], l2_scr[hh]
            m = jnp.maximum(jnp.maximum(l0, l1), l2)
            e0, e1, e2 = jnp.exp2(l0 - m), jnp.exp2(l1 - m), jnp.exp2(l2 - m)
            num = e0 * oa0[:, sl].astype(F32) + e1 * o1_scr[hh] + e2 * o2_scr[hh]
            oa_scr[:, sl] = (num / (e0 + e1 + e2)).astype(oa_scr.dtype)

    for rows in _row_chunks(tm):
        hr = h_ref[rows, :]
        merged = None
        for wg, o_br, w_br in ((wg0, oa_scr, wa), (wg1, ob, wb), (wg2, oc, wc)):
            gate = jax.nn.sigmoid(jnp.dot(hr, wg[...], preferred_element_type=F32))
            term = gate * jnp.dot(o_br[rows, :], w_br[...], preferred_element_type=F32)
            merged = term if merged is None else merged + term
        out_ref[rows, :] = merged.astype(out_ref.dtype)


def _merge(h, w_in, oa, lse, ob, oc, wa, wb, wc, seq, tm=1024, tn=512):
    t = ob.shape[0]
    nj = D_MODEL // tn
    nseq = seq // tm

    def row(w):
        return pl.BlockSpec((tm, w), lambda i, j: (i, 0))

    def stream(g):
        dil = DIL_GROUPS[g][1]
        return pl.BlockSpec((None, dil, tm // dil, SUB), lambda i, j: (i // nseq, 0, i % nseq, 0))

    def gate_w(br):
        return pl.BlockSpec((D_MODEL, tn), lambda i, j: (0, (QKV_COLS + br * D_MODEL) // tn + j))

    def wspec(k):
        return pl.BlockSpec((k, tn), lambda i, j: (0, j))

    return pl.pallas_call(
        functools.partial(_merge_kernel, tm=tm),
        grid=(t // tm, nj),
        in_specs=[row(D_MODEL), gate_w(0), gate_w(1), gate_w(2),
                  row(SUB), stream(1), stream(2), row(SUB), stream(1), stream(2),
                  row(B_QW), row(C_VW), wspec(SUB), wspec(B_QW), wspec(C_VW)],
        out_specs=pl.BlockSpec((tm, tn), lambda i, j: (i, j)),
        out_shape=jax.ShapeDtypeStruct((t, D_MODEL), BF16),
        scratch_shapes=[pltpu.VMEM((tm, SUB), BF16)] + [pltpu.VMEM((SUB // 128, tm, 128), F32)] * 4,
        compiler_params=_cparams(("parallel", "arbitrary")),
        name="merge",
    )(h, w_in, w_in, w_in, *oa, *lse, ob, oc, wa, wb, wc)


def _out_kernel(m_ref, w_ref, x_ref, gpost_ref, gnext_ref, x1_ref, h_ref):
    for rows in _row_chunks(m_ref.shape[0]):
        y = jnp.dot(m_ref[rows, :], w_ref[...], preferred_element_type=F32)
        x1 = x_ref[rows, :] + _rms_rows(y, gpost_ref[...])
        x1_ref[rows, :] = x1
        h_ref[rows, :] = _rms_rows(x1, gnext_ref[...]).astype(h_ref.dtype)


def _out_proj(merged, w_out, x, g_post, g_next, tm=512):
    t = x.shape[0]
    row = lambda i: (i, 0)
    const = lambda i: (0, 0)
    return pl.pallas_call(
        _out_kernel,
        grid=(t // tm,),
        in_specs=[pl.BlockSpec((tm, D_MODEL), row),
                  pl.BlockSpec((D_MODEL, D_MODEL), const),
                  pl.BlockSpec((tm, D_MODEL), row),
                  pl.BlockSpec((1, D_MODEL), const),
                  pl.BlockSpec((1, D_MODEL), const)],
        out_specs=[pl.BlockSpec((tm, D_MODEL), row), pl.BlockSpec((tm, D_MODEL), row)],
        out_shape=[jax.ShapeDtypeStruct((t, D_MODEL), F32), jax.ShapeDtypeStruct((t, D_MODEL), BF16)],
        compiler_params=_cparams(("parallel",)),
        name="out_proj",
    )(merged, w_out, x, g_post.reshape(1, D_MODEL), g_next.reshape(1, D_MODEL))


def _mlp_kernel(h_ref, w1_ref, w2_ref, x_ref, gpost_ref, *rest, with_next):
    if with_next:
        gnext_ref, x2_ref, hn_ref, acc_ref = rest
    else:
        x2_ref, acc_ref = rest
    f = pl.program_id(1)

    @pl.when(f == 0)
    def _():
        acc_ref[...] = jnp.zeros_like(acc_ref)

    u = jnp.dot(h_ref[...], w1_ref[...], preferred_element_type=F32)
    u = jnp.square(jnp.maximum(u, 0.0)).astype(BF16)
    acc_ref[...] += jnp.dot(u, w2_ref[...], preferred_element_type=F32)

    @pl.when(f == pl.num_programs(1) - 1)
    def _():
        x2 = x_ref[...] + _rms_rows(acc_ref[...], gpost_ref[...])
        x2_ref[...] = x2
        if with_next:
            hn_ref[...] = _rms_rows(x2, gnext_ref[...]).astype(hn_ref.dtype)


def _mlp(h, w1, w2, x, g_post, g_next, tm=512, tf=1024):
    t = x.shape[0]
    with_next = g_next is not None
    row = lambda i, f: (i, 0)
    const = lambda i, f: (0, 0)
    in_specs = [pl.BlockSpec((tm, D_MODEL), row),
                pl.BlockSpec((D_MODEL, tf), lambda i, f: (0, f)),
                pl.BlockSpec((tf, D_MODEL), lambda i, f: (f, 0)),
                pl.BlockSpec((tm, D_MODEL), row),
                pl.BlockSpec((1, D_MODEL), const)]
    args = [h, w1, w2, x, g_post.reshape(1, D_MODEL)]
    out_specs = [pl.BlockSpec((tm, D_MODEL), row)]
    out_shape = [jax.ShapeDtypeStruct((t, D_MODEL), F32)]
    if with_next:
        in_specs.append(pl.BlockSpec((1, D_MODEL), const))
        args.append(g_next.reshape(1, D_MODEL))
        out_specs.append(pl.BlockSpec((tm, D_MODEL), row))
        out_shape.append(jax.ShapeDtypeStruct((t, D_MODEL), BF16))
    res = pl.pallas_call(
        functools.partial(_mlp_kernel, with_next=with_next),
        grid=(t // tm, D_FF // tf),
        in_specs=in_specs,
        out_specs=out_specs,
        out_shape=out_shape,
        scratch_shapes=[pltpu.VMEM((tm, D_MODEL), F32)],
        compiler_params=_cparams(("parallel", "arbitrary")),
        name="mlp",
    )(*args)
    return (res[0], res[1]) if with_next else (res[0], None)


def _rope_tables(pos, dim):
    inv = 1.0 / (ROPE_THETA ** (jnp.arange(0, dim, 2, dtype=F32) / dim))
    ang = pos.astype(F32)[:, None] * inv[None, :]
    return jnp.cos(ang), jnp.sin(ang)


def _tables(seq):
    pos = jnp.arange(seq)
    ca, sa = _rope_tables(pos, HEAD_DIM)
    cos_a = jnp.concatenate([ca, ca], axis=1)
    sin_a = jnp.concatenate([-sa, sa], axis=1)
    z32 = jnp.zeros((seq, 32), F32)

    def half32(c, s):
        return jnp.concatenate([c, c], 1), jnp.concatenate([-s, z32], 1), jnp.concatenate([z32, s], 1)

    cr, sr = _rope_tables(pos // GRID_W, AXIAL_DIM)
    cc, sc = _rope_tables(pos % GRID_W, AXIAL_DIM)
    r_c, r_lo, r_hi = half32(cr, sr)
    c_c, c_lo, c_hi = half32(cc, sc)
    cos_b = jnp.concatenate([r_c, c_c], 1)
    sin_b_lo = jnp.concatenate([r_lo, c_lo], 1)
    sin_b_hi = jnp.concatenate([r_hi, c_hi], 1)
    c3, s3 = _rope_tables(pos, C_QK_DIM)
    m_c, m_lo, m_hi = half32(c3, s3)
    cos_c = jnp.concatenate([m_c, m_c], 1)
    sin_c_lo = jnp.concatenate([m_lo, m_lo], 1)
    sin_c_hi = jnp.concatenate([m_hi, m_hi], 1)
    return jnp.stack([cos_a, sin_a, cos_b, sin_b_lo, sin_b_hi, cos_c, sin_c_lo, sin_c_hi], axis=0)


def _trunk(x3, p):
    batch, seq, _ = x3.shape
    x = x3.reshape(batch * seq, D_MODEL)
    tabs = _tables(seq)
    h = _prenorm(x, p["g_mix_pre"][0])
    for li in range(DEPTH):
        lam_init = 0.8 - 0.6 * math.exp(-0.3 * li)
        w_in = p["w_in"][li]
        qkv, st1, st2 = _qkv_proj(h, w_in, tabs, p["b_q_norm"][li], p["b_k_norm"][li], batch, seq)
        oa, lse = zip(*[_attn_a_group(src, gi, batch, seq) for gi, src in enumerate((qkv, st1, st2))])
        ob = _attn_b(qkv, batch, seq)
        oc = _attn_c(qkv, p["c_lambda"][li], p["c_head_norm"][li], lam_init, batch, seq)
        merged = _merge(h, w_in, oa, lse, ob, oc,
                        p["w_branch_a"][li], p["w_branch_b"][li], p["w_branch_c"][li], seq)
        x, hm = _out_proj(merged, p["w_out"][li], x, p["g_mix_post"][li], p["g_mlp_pre"][li])
        g_next = p["g_mix_pre"][li + 1] if li + 1 < DEPTH else None
        x, h = _mlp(hm, p["w_mlp_in"][li], p["w_mlp_out"][li], x, p["g_mlp_post"][li], g_next)
    return x.reshape(batch, seq, D_MODEL)


def kernel(x_prompt, x_sample, g_mix_pre, w_in, b_q_norm, b_k_norm, c_lambda, c_head_norm,
           w_branch_a, w_branch_b, w_branch_c, w_out, g_mix_post,
           g_mlp_pre, w_mlp_in, w_mlp_out, g_mlp_post):
    p = dict(g_mix_pre=g_mix_pre, w_in=w_in.astype(BF16), b_q_norm=b_q_norm, b_k_norm=b_k_norm,
             c_lambda=c_lambda, c_head_norm=c_head_norm,
             w_branch_a=w_branch_a.astype(BF16), w_branch_b=w_branch_b.astype(BF16),
             w_branch_c=w_branch_c.astype(BF16), w_out=w_out.astype(BF16), g_mix_post=g_mix_post,
             g_mlp_pre=g_mlp_pre, w_mlp_in=w_mlp_in.astype(BF16), w_mlp_out=w_mlp_out.astype(BF16),
             g_mlp_post=g_mlp_post)
    return (_trunk(x_prompt, p), _trunk(x_sample, p))
```

```python
import functools
import math

import jax
import jax.numpy as jnp
from jax import lax
from jax.experimental import pallas as pl
from jax.experimental.pallas import tpu as pltpu

F32 = jnp.float32
BF16 = jnp.bfloat16

D_MODEL = 2048
DEPTH = 2
HEAD_DIM = 128
GRID_W = 64
ROPE_THETA = 10000.0
EPS = 1e-6
NEG_INF = -1e30
LOG2E = math.log2(math.e)

DIL_GROUPS = ((128, 1), (512, 4), (2048, 16))
A_HALF = 64
A_W = 768
B_Q_HEADS = 6
B_KV_HEADS = 2
B_GROUP = B_Q_HEADS // B_KV_HEADS
B_QW = 768
B_KVW = 256
AXIAL_DIM = 64
C_HEADS = 4
C_QK_DIM = 64
C_QW = 512
C_VW = 512
N_BRANCH = 3
QKV_COLS = 3 * A_W + B_QW + 2 * B_KVW + 2 * C_QW + C_VW
GATE_COLS = N_BRANCH * D_MODEL
D_FF = 4 * D_MODEL

OFF_AQ, OFF_AK, OFF_AV = 0, 768, 1536
OFF_BQ, OFF_BK, OFF_BV = 2304, 3072, 3328
OFF_CQ, OFF_CK, OFF_CV = 3584, 4096, 4608

SUB = 256
_QKV_KINDS = (("aq",) * 3 + ("ak",) * 3 + ("v",) * 3 + ("bq",) * 3 + ("bk",) + ("v",)
              + ("cq",) * 2 + ("ck",) * 2 + ("v",) * 2)

VMEM_LIMIT = 56 * 1024 * 1024


def _cparams(sem):
    return pltpu.CompilerParams(dimension_semantics=sem, vmem_limit_bytes=VMEM_LIMIT)


def _rms_rows(xf, g):
    ms = jnp.mean(xf * xf, axis=-1, keepdims=True)
    return xf * lax.rsqrt(ms + EPS) * g


def _rope_half64(x, cos, sin_signed):
    return x * cos + pltpu.roll(x, 64, axis=1) * sin_signed


def _rope_half32(x, cos, sin_lo, sin_hi):
    return x * cos + pltpu.roll(x, 96, axis=1) * sin_lo + pltpu.roll(x, 32, axis=1) * sin_hi


def _norm_kernel(x_ref, g_ref, h_ref):
    h_ref[...] = _rms_rows(x_ref[...], g_ref[...]).astype(h_ref.dtype)


def _prenorm(x, g, tm=1024):
    t = x.shape[0]
    return pl.pallas_call(
        _norm_kernel,
        grid=(t // tm,),
        in_specs=[pl.BlockSpec((tm, D_MODEL), lambda i: (i, 0)),
                  pl.BlockSpec((1, D_MODEL), lambda i: (0, 0))],
        out_specs=pl.BlockSpec((tm, D_MODEL), lambda i: (i, 0)),
        out_shape=jax.ShapeDtypeStruct((t, D_MODEL), BF16),
        compiler_params=_cparams(("parallel",)),
        name="prenorm",
    )(x, g.reshape(1, D_MODEL))


_STREAM_TILES = {1: (1, 4, 7), 2: (2, 5, 8)}


ROW_CHUNK = 256


def _row_chunks(n):
    return [slice(r, r + ROW_CHUNK) for r in range(0, n, ROW_CHUNK)]


def _qkv_kernel(h_ref, w_ref, tab_ref, gq_ref, gk_ref, o_ref, s1_ref, s2_ref, epi_scr, *, tm):
    def head_epilogue(x, kind, rows):
        if kind in ("aq", "ak"):
            y = _rope_half64(x, tab_ref[0, rows, :], tab_ref[1, rows, :])
            return y * (HEAD_DIM ** -0.5 * LOG2E) if kind == "aq" else y
        if kind in ("bq", "bk"):
            g = gq_ref[...] * (HEAD_DIM ** -0.5 * LOG2E) if kind == "bq" else gk_ref[...]
            return _rope_half32(_rms_rows(x, g), tab_ref[2, rows, :], tab_ref[3, rows, :], tab_ref[4, rows, :])
        if kind in ("cq", "ck"):
            y = _rope_half32(x, tab_ref[5, rows, :], tab_ref[6, rows, :], tab_ref[7, rows, :])
            return y * (C_QK_DIM ** -0.5 * LOG2E) if kind == "cq" else y
        return x

    n_heads = SUB // 128
    for jj, kind in enumerate(_QKV_KINDS):
        g = next((g for g in (1, 2) if jj in _STREAM_TILES[g]), None)
        for c, rows in enumerate(_row_chunks(tm)):
            acc = jnp.dot(h_ref[rows, :], w_ref[:, jj * SUB:(jj + 1) * SUB], preferred_element_type=F32)
            for hh in range(n_heads):
                sl = slice(hh * 128, (hh + 1) * 128)
                y = head_epilogue(acc[:, sl], kind, rows)
                o_ref[rows, jj * SUB + hh * 128:jj * SUB + (hh + 1) * 128] = y.astype(o_ref.dtype)
                if g is not None:
                    ref = (s1_ref, s2_ref)[g - 1]
                    dil = DIL_GROUPS[g][1]
                    nu = ROW_CHUNK // dil
                    slot = (jj % 2) * n_heads + hh
                    col0 = _STREAM_TILES[g].index(jj) * SUB + hh * 128
                    epi_scr[slot, rows, :] = y
                    for r in range(dil):
                        rows_r = epi_scr[slot, pl.ds(rows.start + r, nu, stride=dil), :]
                        ref[r, c * nu:(c + 1) * nu, col0:col0 + 128] = rows_r.astype(ref.dtype)


def _qkv_proj(h, w_in, tabs, gq, gk, batch, seq, tm=512):
    t = h.shape[0]
    nseq = seq // tm
    d1, d2 = DIL_GROUPS[1][1], DIL_GROUPS[2][1]

    def stream_spec(dil):
        return pl.BlockSpec((None, dil, tm // dil, 3 * SUB), lambda i: (i // nseq, 0, i % nseq, 0))

    return pl.pallas_call(
        functools.partial(_qkv_kernel, tm=tm),
        grid=(t // tm,),
        in_specs=[pl.BlockSpec((tm, D_MODEL), lambda i: (i, 0)),
                  pl.BlockSpec((D_MODEL, QKV_COLS), lambda i: (0, 0), pipeline_mode=pl.Buffered(1)),
                  pl.BlockSpec((8, tm, 128), lambda i: (0, i % nseq, 0)),
                  pl.BlockSpec((1, 128), lambda i: (0, 0)),
                  pl.BlockSpec((1, 128), lambda i: (0, 0))],
        out_specs=[pl.BlockSpec((tm, QKV_COLS), lambda i: (i, 0)), stream_spec(d1), stream_spec(d2)],
        out_shape=[jax.ShapeDtypeStruct((t, QKV_COLS), BF16),
                   jax.ShapeDtypeStruct((batch, d1, seq // d1, 3 * SUB), BF16),
                   jax.ShapeDtypeStruct((batch, d2, seq // d2, 3 * SUB), BF16)],
        scratch_shapes=[pltpu.VMEM((2 * (SUB // 128), tm, 128), F32)],
        compiler_params=_cparams(("parallel",)),
        name="qkv_proj",
    )(h, w_in, tabs, gq.reshape(1, 128), gk.reshape(1, 128))


A_TQ = 128
A_TK = A_TQ + 2 * A_HALF


def _attn_a_kernel(q_ref, k_ref, v_ref, o_ref, lse_ref, *, length):
    nt = length // A_TQ
    rows = lax.broadcasted_iota(jnp.int32, (A_TQ, A_TK), 0)
    cols = lax.broadcasted_iota(jnp.int32, (A_TQ, A_TK), 1)

    def body(i, carry):
        q0 = pl.multiple_of(i * A_TQ, A_TQ)
        k0 = pl.multiple_of(jnp.clip(q0 - A_HALF, 0, length - A_TK), 16)
        valid = jnp.abs(rows - cols + (q0 - k0)) <= A_HALF
        for hh in range(2):
            sl = slice(hh * 128, (hh + 1) * 128)
            q = q_ref[pl.ds(q0, A_TQ), sl]
            k = k_ref[pl.ds(k0, A_TK), sl]
            v = v_ref[pl.ds(k0, A_TK), sl]
            s = lax.dot_general(q, k, (((1,), (1,)), ((), ())), preferred_element_type=F32)
            s = jnp.where(valid, s, NEG_INF)
            m = jnp.max(s, axis=1, keepdims=True)
            e = jnp.exp2(s - m)
            den = jnp.sum(e, axis=1, keepdims=True)
            o = jnp.dot(e.astype(BF16), v, preferred_element_type=F32) / den
            o_ref[pl.ds(q0, A_TQ), sl] = o.astype(o_ref.dtype)
            lse_ref[pl.ds(q0, A_TQ), sl] = jnp.broadcast_to(m + jnp.log2(den), (A_TQ, 128))
        return carry

    lax.fori_loop(0, nt, body, 0, unroll=min(nt, 4))


def _attn_a_group(src, gi, batch, seq):
    dil = DIL_GROUPS[gi][1]
    length = seq // dil
    kern = functools.partial(_attn_a_kernel, length=length)
    if dil == 1:
        spec = lambda c: pl.BlockSpec((seq, SUB), lambda b: (b, c))
        out_spec = pl.BlockSpec((seq, SUB), lambda b: (b, 0))
        return pl.pallas_call(
            kern, grid=(batch,),
            in_specs=[spec(OFF_AQ // SUB + gi), spec(OFF_AK // SUB + gi), spec(OFF_AV // SUB + gi)],
            out_specs=[out_spec, out_spec],
            out_shape=[jax.ShapeDtypeStruct((batch * seq, SUB), BF16),
                       jax.ShapeDtypeStruct((batch * seq, SUB), F32)],
            compiler_params=_cparams(("parallel",)),
            name=f"attn_a{gi}",
        )(src, src, src)
    spec = lambda c: pl.BlockSpec((None, None, length, SUB), lambda b, r: (b, r, 0, c))
    out_spec = pl.BlockSpec((None, None, length, SUB), lambda b, r: (b, r, 0, 0))
    return pl.pallas_call(
        kern, grid=(batch, dil),
        in_specs=[spec(0), spec(1), spec(2)],
        out_specs=[out_spec, out_spec],
        out_shape=[jax.ShapeDtypeStruct((batch, dil, length, SUB), BF16),
                   jax.ShapeDtypeStruct((batch, dil, length, SUB), F32)],
        compiler_params=_cparams(("parallel", "parallel")),
        name=f"attn_a{gi}",
    )(src, src, src)


FLASH_BLOCK = 256


def _transpose_values(v_ref, vt_scr, tk):
    for t in range(vt_scr.shape[0]):
        vt_scr[t] = v_ref[t * tk:(t + 1) * tk, :].astype(F32).T.astype(vt_scr.dtype)


N_SCORE_BUFS = 3


def _flash_scratch(seq, tk, rows):
    return ([pltpu.VMEM((seq // tk, HEAD_DIM, tk), BF16),
             pltpu.VMEM((N_SCORE_BUFS, tk, rows), F32),
             pltpu.VMEM((2, tk, rows), BF16),
             pltpu.VMEM((HEAD_DIM, rows), F32)])


def _flash_keymajor(q, k_ref, vt_scr, s_scr, p_scr, acc_scr, tk):
    rows = q.shape[0]
    nk = vt_scr.shape[0]
    q_blocks = [slice(c, c + FLASH_BLOCK) for c in range(0, rows, FLASH_BLOCK)]
    k_chunks = [slice(c, c + FLASH_BLOCK) for c in range(0, tk, FLASH_BLOCK)]
    bufs = [s_scr.at[b] for b in range(N_SCORE_BUFS)]

    def scores(t, qb, dst):
        k = k_ref[t * tk:(t + 1) * tk, :]
        dst[:, qb] = lax.dot_general(k, q[qb, :], (((1,), (1,)), ((), ())), preferred_element_type=F32)

    def update(m, l, src, t, qb):
        m_new = m
        for kc in k_chunks:
            m_new = jnp.maximum(m_new, jnp.max(src[kc, qb], axis=0, keepdims=True))
        alpha = jnp.exp2(m - m_new)
        l_tile = None
        p_buf = p_scr.at[t % 2]
        for kc in k_chunks:
            p = jnp.exp2(src[kc, qb] - m_new)
            p_sum = jnp.sum(p, axis=0, keepdims=True)
            l_tile = p_sum if l_tile is None else l_tile + p_sum
            p_buf[kc, qb] = p.astype(BF16)
        pv = jnp.dot(vt_scr[t], p_buf[:, qb], preferred_element_type=F32)
        acc_scr[:, qb] = pv if t == 0 else alpha * acc_scr[:, qb] + pv
        return m_new, alpha * l + l_tile

    m = [jnp.full((1, FLASH_BLOCK), NEG_INF, F32) for _ in q_blocks]
    l = [jnp.zeros((1, FLASH_BLOCK), F32) for _ in q_blocks]
    for qb in q_blocks:
        scores(0, qb, bufs[0])
    for t in range(nk):
        for bi, qb in enumerate(q_blocks):
            if t + 1 < nk:
                scores(t + 1, qb, bufs[(t + 1) % N_SCORE_BUFS])
            m[bi], l[bi] = update(m[bi], l[bi], bufs[t % N_SCORE_BUFS], t, qb)
    return acc_scr[...], jnp.concatenate(l, axis=1)


def _attn_b_kernel(q_ref, k_ref, v_ref, o_ref, vt_scr, s_scr, p_scr, acc_scr, *, tq, tk):
    @pl.when(pl.program_id(2) == 0)
    def _():
        _transpose_values(v_ref, vt_scr, tk)

    q = jnp.concatenate([q_ref[:, h * 128:(h + 1) * 128] for h in range(B_GROUP)], axis=0)
    acc, l = _flash_keymajor(q, k_ref, vt_scr, s_scr, p_scr, acc_scr, tk)
    o = acc * (1.0 / l)
    for h in range(B_GROUP):
        o_ref[:, h * 128:(h + 1) * 128] = o[:, h * tq:(h + 1) * tq].T.astype(o_ref.dtype)


def _attn_b(qkv, batch, seq, tq=512, tk=1024):
    nq = seq // tq
    gw = B_GROUP * 128
    rows = B_GROUP * tq
    return pl.pallas_call(
        functools.partial(_attn_b_kernel, tq=tq, tk=tk),
        grid=(batch, B_KV_HEADS, nq),
        in_specs=[pl.BlockSpec((tq, gw), lambda b, kh, i: (b * nq + i, OFF_BQ // gw + kh)),
                  pl.BlockSpec((seq, 128), lambda b, kh, i: (b, OFF_BK // 128 + kh)),
                  pl.BlockSpec((seq, 128), lambda b, kh, i: (b, OFF_BV // 128 + kh))],
        out_specs=pl.BlockSpec((tq, gw), lambda b, kh, i: (b * nq + i, kh)),
        out_shape=jax.ShapeDtypeStruct((batch * seq, B_QW), BF16),
        scratch_shapes=_flash_scratch(seq, tk, rows),
        compiler_params=_cparams(("parallel", "parallel", "arbitrary")),
        name="attn_b",
    )(qkv, qkv, qkv)


def _attn_c_kernel(lam_ref, g_ref, q_ref, k_ref, v_ref, o_ref, vt_scr, s_scr, p_scr, acc_scr,
                   *, tq, tk, lam_init):
    @pl.when(pl.program_id(2) == 0)
    def _():
        _transpose_values(v_ref, vt_scr, tk)

    lamv = lam_ref[...]
    lam = (jnp.exp(jnp.sum(lamv[0:1] * lamv[1:2], axis=1, keepdims=True))
           - jnp.exp(jnp.sum(lamv[2:3] * lamv[3:4], axis=1, keepdims=True)) + lam_init)
    q = q_ref[...]
    lane = lax.broadcasted_iota(jnp.int32, q.shape, 1)
    zero = jnp.zeros_like(q)
    q2 = jnp.concatenate([jnp.where(lane < C_QK_DIM, q, zero), jnp.where(lane >= C_QK_DIM, q, zero)], axis=0)
    acc, l = _flash_keymajor(q2, k_ref, vt_scr, s_scr, p_scr, acc_scr, tk)
    o = acc * (1.0 / l)
    oc = (o[:, :tq] - lam * o[:, tq:]).T
    o_ref[...] = (_rms_rows(oc, g_ref[...]) * (1.0 - lam_init)).astype(o_ref.dtype)


def _attn_c(qkv, c_lambda, c_head_norm, lam_init, batch, seq, tq=512, tk=1024):
    nq = seq // tq
    rows = 2 * tq
    return pl.pallas_call(
        functools.partial(_attn_c_kernel, tq=tq, tk=tk, lam_init=lam_init),
        grid=(batch, C_HEADS, nq),
        in_specs=[pl.BlockSpec((4, C_QK_DIM), lambda b, h, i: (0, 0)),
                  pl.BlockSpec((1, 128), lambda b, h, i: (0, 0)),
                  pl.BlockSpec((tq, 128), lambda b, h, i: (b * nq + i, OFF_CQ // 128 + h)),
                  pl.BlockSpec((seq, 128), lambda b, h, i: (b, OFF_CK // 128 + h)),
                  pl.BlockSpec((seq, 128), lambda b, h, i: (b, OFF_CV // 128 + h))],
        out_specs=pl.BlockSpec((tq, 128), lambda b, h, i: (b * nq + i, h)),
        out_shape=jax.ShapeDtypeStruct((batch * seq, C_VW), BF16),
        scratch_shapes=_flash_scratch(seq, tk, rows),
        compiler_params=_cparams(("parallel", "parallel", "arbitrary")),
        name="attn_c",
    )(c_lambda, c_head_norm.reshape(1, 128), qkv, qkv, qkv)


def _merge_kernel(h_ref, wg0, wg1, wg2, oa0, oa1, oa2, ls0, ls1, ls2, ob, oc, wa, wb, wc, out_ref,
                  oa_scr, o1_scr, o2_scr, l1_scr, l2_scr, *, tm):
    @pl.when(pl.program_id(1) == 0)
    def _():
        for g, o_src, l_src, o_dst, l_dst in ((1, oa1, ls1, o1_scr, l1_scr), (2, oa2, ls2, o2_scr, l2_scr)):
            dil = DIL_GROUPS[g][1]
            for r in range(dil):
                for hh in range(SUB // 128):
                    sl = slice(hh * 128, (hh + 1) * 128)
                    o_dst[hh, pl.ds(r, tm // dil, stride=dil), :] = o_src[r, :, sl].astype(F32)
                    l_dst[hh, pl.ds(r, tm // dil, stride=dil), :] = l_src[r, :, sl]
        for hh in range(SUB // 128):
            sl = slice(hh * 128, (hh + 1) * 128)
            l0, l1, l2 = ls0[:, sl], l1_scr[hh], l2_scr[hh]
            m = jnp.maximum(jnp.maximum(l0, l1), l2)
            e0, e1, e2 = jnp.exp2(l0 - m), jnp.exp2(l1 - m), jnp.exp2(l2 - m)
            num = e0 * oa0[:, sl].astype(F32) + e1 * o1_scr[hh] + e2 * o2_scr[hh]
            oa_scr[:, sl] = (num / (e0 + e1 + e2)).astype(oa_scr.dtype)

    for rows in _row_chunks(tm):
        hr = h_ref[rows, :]
        merged = None
        for wg, o_br, w_br in ((wg0, oa_scr, wa), (wg1, ob, wb), (wg2, oc, wc)):
            gate = jax.nn.sigmoid(jnp.dot(hr, wg[...], preferred_element_type=F32))
            term = gate * jnp.dot(o_br[rows, :], w_br[...], preferred_element_type=F32)
            merged = term if merged is None else merged + term
        out_ref[rows, :] = merged.astype(out_ref.dtype)


def _merge(h, w_in, oa, lse, ob, oc, wa, wb, wc, seq, tm=1024, tn=512):
    t = ob.shape[0]
    nj = D_MODEL // tn
    nseq = seq // tm

    def row(w):
        return pl.BlockSpec((tm, w), lambda i, j: (i, 0))

    def stream(g):
        dil = DIL_GROUPS[g][1]
        return pl.BlockSpec((None, dil, tm // dil, SUB), lambda i, j: (i // nseq, 0, i % nseq, 0))

    def gate_w(br):
        return pl.BlockSpec((D_MODEL, tn), lambda i, j: (0, (QKV_COLS + br * D_MODEL) // tn + j))

    def wspec(k):
        return pl.BlockSpec((k, tn), lambda i, j: (0, j))

    return pl.pallas_call(
        functools.partial(_merge_kernel, tm=tm),
        grid=(t // tm, nj),
        in_specs=[row(D_MODEL), gate_w(0), gate_w(1), gate_w(2),
                  row(SUB), stream(1), stream(2), row(SUB), stream(1), stream(2),
                  row(B_QW), row(C_VW), wspec(SUB), wspec(B_QW), wspec(C_VW)],
        out_specs=pl.BlockSpec((tm, tn), lambda i, j: (i, j)),
        out_shape=jax.ShapeDtypeStruct((t, D_MODEL), BF16),
        scratch_shapes=[pltpu.VMEM((tm, SUB), BF16)] + [pltpu.VMEM((SUB // 128, tm, 128), F32)] * 4,
        compiler_params=_cparams(("parallel", "arbitrary")),
        name="merge",
    )(h, w_in, w_in, w_in, *oa, *lse, ob, oc, wa, wb, wc)


def _out_kernel(m_ref, w_ref, x_ref, gpost_ref, gnext_ref, x1_ref, h_ref):
    for rows in _row_chunks(m_ref.shape[0]):
        y = jnp.dot(m_ref[rows, :], w_ref[...], preferred_element_type=F32)
        x1 = x_ref[rows, :] + _rms_rows(y, gpost_ref[...])
        x1_ref[rows, :] = x1
        h_ref[rows, :] = _rms_rows(x1, gnext_ref[...]).astype(h_ref.dtype)


def _out_proj(merged, w_out, x, g_post, g_next, tm=512):
    t = x.shape[0]
    row = lambda i: (i, 0)
    const = lambda i: (0, 0)
    return pl.pallas_call(
        _out_kernel,
        grid=(t // tm,),
        in_specs=[pl.BlockSpec((tm, D_MODEL), row),
                  pl.BlockSpec((D_MODEL, D_MODEL), const),
                  pl.BlockSpec((tm, D_MODEL), row),
                  pl.BlockSpec((1, D_MODEL), const),
                  pl.BlockSpec((1, D_MODEL), const)],
        out_specs=[pl.BlockSpec((tm, D_MODEL), row), pl.BlockSpec((tm, D_MODEL), row)],
        out_shape=[jax.ShapeDtypeStruct((t, D_MODEL), F32), jax.ShapeDtypeStruct((t, D_MODEL), BF16)],
        compiler_params=_cparams(("parallel",)),
        name="out_proj",
    )(merged, w_out, x, g_post.reshape(1, D_MODEL), g_next.reshape(1, D_MODEL))


def _mlp_kernel(h_ref, w1_ref, w2_ref, x_ref, gpost_ref, *rest, with_next):
    if with_next:
        gnext_ref, x2_ref, hn_ref, acc_ref = rest
    else:
        x2_ref, acc_ref = rest
    f = pl.program_id(1)

    @pl.when(f == 0)
    def _():
        acc_ref[...] = jnp.zeros_like(acc_ref)

    u = jnp.dot(h_ref[...], w1_ref[...], preferred_element_type=F32)
    u = jnp.square(jnp.maximum(u, 0.0)).astype(BF16)
    acc_ref[...] += jnp.dot(u, w2_ref[...], preferred_element_type=F32)

    @pl.when(f == pl.num_programs(1) - 1)
    def _():
        x2 = x_ref[...] + _rms_rows(acc_ref[...], gpost_ref[...])
        x2_ref[...] = x2
        if with_next:
            hn_ref[...] = _rms_rows(x2, gnext_ref[...]).astype(hn_ref.dtype)


def _mlp(h, w1, w2, x, g_post, g_next, tm=512, tf=1024):
    t = x.shape[0]
    with_next = g_next is not None
    row = lambda i, f: (i, 0)
    const = lambda i, f: (0, 0)
    in_specs = [pl.BlockSpec((tm, D_MODEL), row),
                pl.BlockSpec((D_MODEL, tf), lambda i, f: (0, f)),
                pl.BlockSpec((tf, D_MODEL), lambda i, f: (f, 0)),
                pl.BlockSpec((tm, D_MODEL), row),
                pl.BlockSpec((1, D_MODEL), const)]
    args = [h, w1, w2, x, g_post.reshape(1, D_MODEL)]
    out_specs = [pl.BlockSpec((tm, D_MODEL), row)]
    out_shape = [jax.ShapeDtypeStruct((t, D_MODEL), F32)]
    if with_next:
        in_specs.append(pl.BlockSpec((1, D_MODEL), const))
        args.append(g_next.reshape(1, D_MODEL))
        out_specs.append(pl.BlockSpec((tm, D_MODEL), row))
        out_shape.append(jax.ShapeDtypeStruct((t, D_MODEL), BF16))
    res = pl.pallas_call(
        functools.partial(_mlp_kernel, with_next=with_next),
        grid=(t // tm, D_FF // tf),
        in_specs=in_specs,
        out_specs=out_specs,
        out_shape=out_shape,
        scratch_shapes=[pltpu.VMEM((tm, D_MODEL), F32)],
        compiler_params=_cparams(("parallel", "arbitrary")),
        name="mlp",
    )(*args)
    return (res[0], res[1]) if with_next else (res[0], None)


def _rope_tables(pos, dim):
    inv = 1.0 / (ROPE_THETA ** (jnp.arange(0, dim, 2, dtype=F32) / dim))
    ang = pos.astype(F32)[:, None] * inv[None, :]
    return jnp.cos(ang), jnp.sin(ang)


def _tables(seq):
    pos = jnp.arange(seq)
    ca, sa = _rope_tables(pos, HEAD_DIM)
    cos_a = jnp.concatenate([ca, ca], axis=1)
    sin_a = jnp.concatenate([-sa, sa], axis=1)
    z32 = jnp.zeros((seq, 32), F32)

    def half32(c, s):
        return jnp.concatenate([c, c], 1), jnp.concatenate([-s, z32], 1), jnp.concatenate([z32, s], 1)

    cr, sr = _rope_tables(pos // GRID_W, AXIAL_DIM)
    cc, sc = _rope_tables(pos % GRID_W, AXIAL_DIM)
    r_c, r_lo, r_hi = half32(cr, sr)
    c_c, c_lo, c_hi = half32(cc, sc)
    cos_b = jnp.concatenate([r_c, c_c], 1)
    sin_b_lo = jnp.concatenate([r_lo, c_lo], 1)
    sin_b_hi = jnp.concatenate([r_hi, c_hi], 1)
    c3, s3 = _rope_tables(pos, C_QK_DIM)
    m_c, m_lo, m_hi = half32(c3, s3)
    cos_c = jnp.concatenate([m_c, m_c], 1)
    sin_c_lo = jnp.concatenate([m_lo, m_lo], 1)
    sin_c_hi = jnp.concatenate([m_hi, m_hi], 1)
    return jnp.stack([cos_a, sin_a, cos_b, sin_b_lo, sin_b_hi, cos_c, sin_c_lo, sin_c_hi], axis=0)


def _trunk(x3, p):
    batch, seq, _ = x3.shape
    x = x3.reshape(batch * seq, D_MODEL)
    tabs = _tables(seq)
    h = _prenorm(x, p["g_mix_pre"][0])
    for li in range(DEPTH):
        lam_init = 0.8 - 0.6 * math.exp(-0.3 * li)
        w_in = p["w_in"][li]
        qkv, st1, st2 = _qkv_proj(h, w_in, tabs, p["b_q_norm"][li], p["b_k_norm"][li], batch, seq)
        oa, lse = zip(*[_attn_a_group(src, gi, batch, seq) for gi, src in enumerate((qkv, st1, st2))])
        ob = _attn_b(qkv, batch, seq)
        oc = _attn_c(qkv, p["c_lambda"][li], p["c_head_norm"][li], lam_init, batch, seq)
        merged = _merge(h, w_in, oa, lse, ob, oc,
                        p["w_branch_a"][li], p["w_branch_b"][li], p["w_branch_c"][li], seq)
        x, hm = _out_proj(merged, p["w_out"][li], x, p["g_mix_post"][li], p["g_mlp_pre"][li])
        g_next = p["g_mix_pre"][li + 1] if li + 1 < DEPTH else None
        x, h = _mlp(hm, p["w_mlp_in"][li], p["w_mlp_out"][li], x, p["g_mlp_post"][li], g_next)
    return x.reshape(batch, seq, D_MODEL)


def kernel(x_prompt, x_sample, g_mix_pre, w_in, b_q_norm, b_k_norm, c_lambda, c_head_norm,
           w_branch_a, w_branch_b, w_branch_c, w_out, g_mix_post,
           g_mlp_pre, w_mlp_in, w_mlp_out, g_mlp_post):
    p = dict(g_mix_pre=g_mix_pre, w_in=w_in.astype(BF16), b_q_norm=b_q_norm, b_k_norm=b_k_norm,
             c_lambda=c_lambda, c_head_norm=c_head_norm,
             w_branch_a=w_branch_a.astype(BF16), w_branch_b=w_branch_b.astype(BF16),
             w_branch_c=w_branch_c.astype(BF16), w_out=w_out.astype(BF16), g_mix_post=g_mix_post,
             g_mlp_pre=g_mlp_pre, w_mlp_in=w_mlp_in.astype(BF16), w_mlp_out=w_mlp_out.astype(BF16),
             g_mlp_post=g_mlp_post)
    return (_trunk(x_prompt, p), _trunk(x_sample, p))
```

```python
import functools
import math

import jax
import jax.numpy as jnp
from jax import lax
from jax.experimental import pallas as pl
from jax.experimental.pallas import tpu as pltpu

F32 = jnp.float32
BF16 = jnp.bfloat16

D_MODEL = 2048
DEPTH = 2
HEAD_DIM = 128
GRID_W = 64
ROPE_THETA = 10000.0
EPS = 1e-6
NEG_INF = -1e30
LOG2E = math.log2(math.e)

DIL_GROUPS = ((128, 1), (512, 4), (2048, 16))
A_HALF = 64
A_W = 768
B_Q_HEADS = 6
B_KV_HEADS = 2
B_GROUP = B_Q_HEADS // B_KV_HEADS
B_QW = 768
B_KVW = 256
AXIAL_DIM = 64
C_HEADS = 4
C_QK_DIM = 64
C_QW = 512
C_VW = 512
N_BRANCH = 3
QKV_COLS = 3 * A_W + B_QW + 2 * B_KVW + 2 * C_QW + C_VW
GATE_COLS = N_BRANCH * D_MODEL
D_FF = 4 * D_MODEL

OFF_AQ, OFF_AK, OFF_AV = 0, 768, 1536
OFF_BQ, OFF_BK, OFF_BV = 2304, 3072, 3328
OFF_CQ, OFF_CK, OFF_CV = 3584, 4096, 4608

SUB = 256
_QKV_KINDS = (("aq",) * 3 + ("ak",) * 3 + ("v",) * 3 + ("bq",) * 3 + ("bk",) + ("v",)
              + ("cq",) * 2 + ("ck",) * 2 + ("v",) * 2)

VMEM_LIMIT = 56 * 1024 * 1024


def _cparams(sem):
    return pltpu.CompilerParams(dimension_semantics=sem, vmem_limit_bytes=VMEM_LIMIT)


def _rms_rows(xf, g):
    ms = jnp.mean(xf * xf, axis=-1, keepdims=True)
    return xf * lax.rsqrt(ms + EPS) * g


def _rope_half64(x, cos, sin_signed):
    return x * cos + pltpu.roll(x, 64, axis=1) * sin_signed


def _rope_half32(x, cos, sin_lo, sin_hi):
    return x * cos + pltpu.roll(x, 96, axis=1) * sin_lo + pltpu.roll(x, 32, axis=1) * sin_hi


def _norm_kernel(x_ref, g_ref, h_ref):
    h_ref[...] = _rms_rows(x_ref[...], g_ref[...]).astype(h_ref.dtype)


def _prenorm(x, g, tm=1024):
    t = x.shape[0]
    return pl.pallas_call(
        _norm_kernel,
        grid=(t // tm,),
        in_specs=[pl.BlockSpec((tm, D_MODEL), lambda i: (i, 0)),
                  pl.BlockSpec((1, D_MODEL), lambda i: (0, 0))],
        out_specs=pl.BlockSpec((tm, D_MODEL), lambda i: (i, 0)),
        out_shape=jax.ShapeDtypeStruct((t, D_MODEL), BF16),
        compiler_params=_cparams(("parallel",)),
        name="prenorm",
    )(x, g.reshape(1, D_MODEL))


_STREAM_TILES = {1: (1, 4, 7), 2: (2, 5, 8)}


ROW_CHUNK = 256


def _row_chunks(n):
    return [slice(r, r + ROW_CHUNK) for r in range(0, n, ROW_CHUNK)]


def _qkv_kernel(h_ref, w_ref, tab_ref, gq_ref, gk_ref, o_ref, s1_ref, s2_ref, epi_scr, *, tm):
    def head_epilogue(x, kind, rows):
        if kind in ("aq", "ak"):
            y = _rope_half64(x, tab_ref[0, rows, :], tab_ref[1, rows, :])
            return y * (HEAD_DIM ** -0.5 * LOG2E) if kind == "aq" else y
        if kind in ("bq", "bk"):
            g = gq_ref[...] * (HEAD_DIM ** -0.5 * LOG2E) if kind == "bq" else gk_ref[...]
            return _rope_half32(_rms_rows(x, g), tab_ref[2, rows, :], tab_ref[3, rows, :], tab_ref[4, rows, :])
        if kind in ("cq", "ck"):
            y = _rope_half32(x, tab_ref[5, rows, :], tab_ref[6, rows, :], tab_ref[7, rows, :])
            return y * (C_QK_DIM ** -0.5 * LOG2E) if kind == "cq" else y
        return x

    n_heads = SUB // 128
    for jj, kind in enumerate(_QKV_KINDS):
        g = next((g for g in (1, 2) if jj in _STREAM_TILES[g]), None)
        for c, rows in enumerate(_row_chunks(tm)):
            acc = jnp.dot(h_ref[rows, :], w_ref[:, jj * SUB:(jj + 1) * SUB], preferred_element_type=F32)
            for hh in range(n_heads):
                sl = slice(hh * 128, (hh + 1) * 128)
                y = head_epilogue(acc[:, sl], kind, rows)
                o_ref[rows, jj * SUB + hh * 128:jj * SUB + (hh + 1) * 128] = y.astype(o_ref.dtype)
                if g is not None:
                    ref = (s1_ref, s2_ref)[g - 1]
                    dil = DIL_GROUPS[g][1]
                    nu = ROW_CHUNK // dil
                    slot = (jj % 2) * n_heads + hh
                    col0 = _STREAM_TILES[g].index(jj) * SUB + hh * 128
                    epi_scr[slot, rows, :] = y
                    for r in range(dil):
                        rows_r = epi_scr[slot, pl.ds(rows.start + r, nu, stride=dil), :]
                        ref[r, c * nu:(c + 1) * nu, col0:col0 + 128] = rows_r.astype(ref.dtype)


def _qkv_proj(h, w_in, li, tabs, gq, gk, batch, seq, tm=512):
    t = h.shape[0]
    nseq = seq // tm
    d1, d2 = DIL_GROUPS[1][1], DIL_GROUPS[2][1]

    def stream_spec(dil):
        return pl.BlockSpec((None, dil, tm // dil, 3 * SUB), lambda i: (i // nseq, 0, i % nseq, 0))

    return pl.pallas_call(
        functools.partial(_qkv_kernel, tm=tm),
        grid=(t // tm,),
        in_specs=[pl.BlockSpec((tm, D_MODEL), lambda i: (i, 0)),
                  pl.BlockSpec((None, D_MODEL, QKV_COLS), lambda i: (li, 0, 0),
                               pipeline_mode=pl.Buffered(1)),
                  pl.BlockSpec((8, tm, 128), lambda i: (0, i % nseq, 0)),
                  pl.BlockSpec((1, 128), lambda i: (0, 0)),
                  pl.BlockSpec((1, 128), lambda i: (0, 0))],
        out_specs=[pl.BlockSpec((tm, QKV_COLS), lambda i: (i, 0)), stream_spec(d1), stream_spec(d2)],
        out_shape=[jax.ShapeDtypeStruct((t, QKV_COLS), BF16),
                   jax.ShapeDtypeStruct((batch, d1, seq // d1, 3 * SUB), BF16),
                   jax.ShapeDtypeStruct((batch, d2, seq // d2, 3 * SUB), BF16)],
        scratch_shapes=[pltpu.VMEM((2 * (SUB // 128), tm, 128), F32)],
        compiler_params=_cparams(("parallel",)),
        name="qkv_proj",
    )(h, w_in, tabs, gq.reshape(1, 128), gk.reshape(1, 128))


A_TQ = 128
A_TK = A_TQ + 2 * A_HALF


def _attn_a_kernel(q_ref, k_ref, v_ref, o_ref, lse_ref, *, length):
    nt = length // A_TQ
    rows = lax.broadcasted_iota(jnp.int32, (A_TQ, A_TK), 0)
    cols = lax.broadcasted_iota(jnp.int32, (A_TQ, A_TK), 1)

    def body(i, carry):
        q0 = pl.multiple_of(i * A_TQ, A_TQ)
        k0 = pl.multiple_of(jnp.clip(q0 - A_HALF, 0, length - A_TK), 16)
        valid = jnp.abs(rows - cols + (q0 - k0)) <= A_HALF
        for hh in range(2):
            sl = slice(hh * 128, (hh + 1) * 128)
            q = q_ref[pl.ds(q0, A_TQ), sl]
            k = k_ref[pl.ds(k0, A_TK), sl]
            v = v_ref[pl.ds(k0, A_TK), sl]
            s = lax.dot_general(q, k, (((1,), (1,)), ((), ())), preferred_element_type=F32)
            s = jnp.where(valid, s, NEG_INF)
            m = jnp.max(s, axis=1, keepdims=True)
            e = jnp.exp2(s - m)
            den = jnp.sum(e, axis=1, keepdims=True)
            o = jnp.dot(e.astype(BF16), v, preferred_element_type=F32) / den
            o_ref[pl.ds(q0, A_TQ), sl] = o.astype(o_ref.dtype)
            lse_ref[pl.ds(q0, A_TQ), sl] = jnp.broadcast_to(m + jnp.log2(den), (A_TQ, 128))
        return carry

    lax.fori_loop(0, nt, body, 0, unroll=min(nt, 4))


def _attn_a_group(src, gi, batch, seq):
    dil = DIL_GROUPS[gi][1]
    length = seq // dil
    kern = functools.partial(_attn_a_kernel, length=length)
    if dil == 1:
        spec = lambda c: pl.BlockSpec((seq, SUB), lambda b: (b, c))
        out_spec = pl.BlockSpec((seq, SUB), lambda b: (b, 0))
        return pl.pallas_call(
            kern, grid=(batch,),
            in_specs=[spec(OFF_AQ // SUB + gi), spec(OFF_AK // SUB + gi), spec(OFF_AV // SUB + gi)],
            out_specs=[out_spec, out_spec],
            out_shape=[jax.ShapeDtypeStruct((batch * seq, SUB), BF16),
                       jax.ShapeDtypeStruct((batch * seq, SUB), F32)],
            compiler_params=_cparams(("parallel",)),
            name=f"attn_a{gi}",
        )(src, src, src)
    spec = lambda c: pl.BlockSpec((None, None, length, SUB), lambda b, r: (b, r, 0, c))
    out_spec = pl.BlockSpec((None, None, length, SUB), lambda b, r: (b, r, 0, 0))
    return pl.pallas_call(
        kern, grid=(batch, dil),
        in_specs=[spec(0), spec(1), spec(2)],
        out_specs=[out_spec, out_spec],
        out_shape=[jax.ShapeDtypeStruct((batch, dil, length, SUB), BF16),
                   jax.ShapeDtypeStruct((batch, dil, length, SUB), F32)],
        compiler_params=_cparams(("parallel", "parallel")),
        name=f"attn_a{gi}",
    )(src, src, src)


FLASH_BLOCK = 256


def _transpose_values(v_ref, vt_scr, tk):
    for t in range(vt_scr.shape[0]):
        vt_scr[t] = v_ref[t * tk:(t + 1) * tk, :].astype(F32).T.astype(vt_scr.dtype)


N_SCORE_BUFS = 2


def _flash_scratch(seq, tk, rows):
    return ([pltpu.VMEM((seq // tk, HEAD_DIM, tk), BF16),
             pltpu.VMEM((N_SCORE_BUFS, tk, rows), F32),
             pltpu.VMEM((tk, rows), BF16),
             pltpu.VMEM((HEAD_DIM, rows), F32)])


def _flash_keymajor(q, k_ref, vt_scr, s_scr, p_scr, acc_scr, tk):
    rows = q.shape[0]
    nk = vt_scr.shape[0]
    q_blocks = [slice(c, c + FLASH_BLOCK) for c in range(0, rows, FLASH_BLOCK)]
    k_chunks = [slice(c, c + FLASH_BLOCK) for c in range(0, tk, FLASH_BLOCK)]

    def scores(t, qb, dst):
        k = k_ref[pl.ds(pl.multiple_of(t * tk, tk), tk), :]
        dst[:, qb] = lax.dot_general(k, q[qb, :], (((1,), (1,)), ((), ())), preferred_element_type=F32)

    def update(m, l, src, t, qb):
        m_new = m
        for kc in k_chunks:
            m_new = jnp.maximum(m_new, jnp.max(src[kc, qb], axis=0, keepdims=True))
        alpha = jnp.exp2(m - m_new)
        l_tile = None
        for kc in k_chunks:
            p = jnp.exp2(src[kc, qb] - m_new)
            p_sum = jnp.sum(p, axis=0, keepdims=True)
            l_tile = p_sum if l_tile is None else l_tile + p_sum
            p_scr[kc, qb] = p.astype(BF16)
        pv = jnp.dot(vt_scr[t], p_scr[:, qb], preferred_element_type=F32)
        acc_scr[:, qb] = alpha * acc_scr[:, qb] + pv
        return m_new, alpha * l + l_tile

    def tile(t, src, dst, m, l):
        m, l = list(m), list(l)
        for bi, qb in enumerate(q_blocks):
            if dst is not None:
                scores(t + 1, qb, dst)
            m[bi], l[bi] = update(m[bi], l[bi], src, t, qb)
        return tuple(m), tuple(l)

    m = tuple(jnp.full((1, FLASH_BLOCK), NEG_INF, F32) for _ in q_blocks)
    l = tuple(jnp.zeros((1, FLASH_BLOCK), F32) for _ in q_blocks)
    acc_scr[...] = jnp.zeros_like(acc_scr)
    for qb in q_blocks:
        scores(0, qb, s_scr.at[0])

    def trip(t, carry):
        return lax.cond(t % 2 == 0,
                        lambda ml: tile(t, s_scr.at[0], s_scr.at[1], *ml),
                        lambda ml: tile(t, s_scr.at[1], s_scr.at[0], *ml), carry)

    m, l = lax.fori_loop(0, nk - 1, trip, (m, l))
    _, l = tile(nk - 1, s_scr.at[(nk - 1) % 2], None, m, l)
    return acc_scr[...], jnp.concatenate(l, axis=1)


def _attn_b_kernel(q_ref, k_ref, v_ref, o_ref, vt_scr, s_scr, p_scr, acc_scr, *, tq, tk):
    @pl.when(pl.program_id(2) == 0)
    def _():
        _transpose_values(v_ref, vt_scr, tk)

    q = jnp.concatenate([q_ref[:, h * 128:(h + 1) * 128] for h in range(B_GROUP)], axis=0)
    acc, l = _flash_keymajor(q, k_ref, vt_scr, s_scr, p_scr, acc_scr, tk)
    o = acc * (1.0 / l)
    for h in range(B_GROUP):
        o_ref[:, h * 128:(h + 1) * 128] = o[:, h * tq:(h + 1) * tq].T.astype(o_ref.dtype)


def _attn_b(qkv, batch, seq, tq=512, tk=1024):
    nq = seq // tq
    gw = B_GROUP * 128
    rows = B_GROUP * tq
    return pl.pallas_call(
        functools.partial(_attn_b_kernel, tq=tq, tk=tk),
        grid=(batch, B_KV_HEADS, nq),
        in_specs=[pl.BlockSpec((tq, gw), lambda b, kh, i: (b * nq + i, OFF_BQ // gw + kh)),
                  pl.BlockSpec((seq, 128), lambda b, kh, i: (b, OFF_BK // 128 + kh)),
                  pl.BlockSpec((seq, 128), lambda b, kh, i: (b, OFF_BV // 128 + kh))],
        out_specs=pl.BlockSpec((tq, gw), lambda b, kh, i: (b * nq + i, kh)),
        out_shape=jax.ShapeDtypeStruct((batch * seq, B_QW), BF16),
        scratch_shapes=_flash_scratch(seq, tk, rows),
        compiler_params=_cparams(("parallel", "parallel", "arbitrary")),
        name="attn_b",
    )(qkv, qkv, qkv)


def _attn_c_kernel(lam_ref, g_ref, q_ref, k_ref, v_ref, o_ref, vt_scr, s_scr, p_scr, acc_scr,
                   *, tq, tk, lam_init):
    @pl.when(pl.program_id(2) == 0)
    def _():
        _transpose_values(v_ref, vt_scr, tk)

    lamv = lam_ref[...]
    lam = (jnp.exp(jnp.sum(lamv[0:1] * lamv[1:2], axis=1, keepdims=True))
           - jnp.exp(jnp.sum(lamv[2:3] * lamv[3:4], axis=1, keepdims=True)) + lam_init)
    q = q_ref[...]
    lane = lax.broadcasted_iota(jnp.int32, q.shape, 1)
    zero = jnp.zeros_like(q)
    q2 = jnp.concatenate([jnp.where(lane < C_QK_DIM, q, zero), jnp.where(lane >= C_QK_DIM, q, zero)], axis=0)
    acc, l = _flash_keymajor(q2, k_ref, vt_scr, s_scr, p_scr, acc_scr, tk)
    o = acc * (1.0 / l)
    oc = (o[:, :tq] - lam * o[:, tq:]).T
    o_ref[...] = (_rms_rows(oc, g_ref[...]) * (1.0 - lam_init)).astype(o_ref.dtype)


def _attn_c(qkv, c_lambda, c_head_norm, lam_init, batch, seq, tq=512, tk=1024):
    nq = seq // tq
    rows = 2 * tq
    return pl.pallas_call(
        functools.partial(_attn_c_kernel, tq=tq, tk=tk, lam_init=lam_init),
        grid=(batch, C_HEADS, nq),
        in_specs=[pl.BlockSpec((4, C_QK_DIM), lambda b, h, i: (0, 0)),
                  pl.BlockSpec((1, 128), lambda b, h, i: (0, 0)),
                  pl.BlockSpec((tq, 128), lambda b, h, i: (b * nq + i, OFF_CQ // 128 + h)),
                  pl.BlockSpec((seq, 128), lambda b, h, i: (b, OFF_CK // 128 + h)),
                  pl.BlockSpec((seq, 128), lambda b, h, i: (b, OFF_CV // 128 + h))],
        out_specs=pl.BlockSpec((tq, 128), lambda b, h, i: (b * nq + i, h)),
        out_shape=jax.ShapeDtypeStruct((batch * seq, C_VW), BF16),
        scratch_shapes=_flash_scratch(seq, tk, rows),
        compiler_params=_cparams(("parallel", "parallel", "arbitrary")),
        name="attn_c",
    )(c_lambda, c_head_norm.reshape(1, 128), qkv, qkv, qkv)


def _merge_kernel(h_ref, wg0, wg1, wg2, oa0, oa1, oa2, ls0, ls1, ls2, ob, oc, wa, wb, wc, out_ref,
                  oa_scr, o1_scr, o2_scr, l1_scr, l2_scr, *, tm):
    @pl.when(pl.program_id(1) == 0)
    def _():
        for g, o_src, l_src, o_dst, l_dst in ((1, oa1, ls1, o1_scr, l1_scr), (2, oa2, ls2, o2_scr, l2_scr)):
            dil = DIL_GROUPS[g][1]
            for r in range(dil):
                for hh in range(SUB // 128):
                    sl = slice(hh * 128, (hh + 1) * 128)
                    o_dst[hh, pl.ds(r, tm // dil, stride=dil), :] = o_src[r, :, sl].astype(F32)
                    l_dst[hh, pl.ds(r, tm // dil, stride=dil), :] = l_src[r, :, sl]
        for hh in range(SUB // 128):
            sl = slice(hh * 128, (hh + 1) * 128)
            l0, l1, l2 = ls0[:, sl], l1_scr[hh], l2_scr[hh]
            m = jnp.maximum(jnp.maximum(l0, l1), l2)
            e0, e1, e2 = jnp.exp2(l0 - m), jnp.exp2(l1 - m), jnp.exp2(l2 - m)
            num = e0 * oa0[:, sl].astype(F32) + e1 * o1_scr[hh] + e2 * o2_scr[hh]
            oa_scr[:, sl] = (num / (e0 + e1 + e2)).astype(oa_scr.dtype)

    for rows in _row_chunks(tm):
        hr = h_ref[rows, :]
        merged = None
        for wg, o_br, w_br in ((wg0, oa_scr, wa), (wg1, ob, wb), (wg2, oc, wc)):
            gate = jax.nn.sigmoid(jnp.dot(hr, wg[...], preferred_element_type=F32))
            term = gate * jnp.dot(o_br[rows, :], w_br[...], preferred_element_type=F32)
            merged = term if merged is None else merged + term
        out_ref[rows, :] = merged.astype(out_ref.dtype)


def _merge(h, w_in, li, oa, lse, ob, oc, wa, wb, wc, seq, tm=1024, tn=512):
    t = ob.shape[0]
    nj = D_MODEL // tn
    nseq = seq // tm

    def row(w):
        return pl.BlockSpec((tm, w), lambda i, j: (i, 0))

    def stream(g):
        dil = DIL_GROUPS[g][1]
        return pl.BlockSpec((None, dil, tm // dil, SUB), lambda i, j: (i // nseq, 0, i % nseq, 0))

    def gate_w(br):
        return pl.BlockSpec((None, D_MODEL, tn),
                            lambda i, j: (li, 0, (QKV_COLS + br * D_MODEL) // tn + j))

    def wspec(k):
        return pl.BlockSpec((None, k, tn), lambda i, j: (li, 0, j))

    return pl.pallas_call(
        functools.partial(_merge_kernel, tm=tm),
        grid=(t // tm, nj),
        in_specs=[row(D_MODEL), gate_w(0), gate_w(1), gate_w(2),
                  row(SUB), stream(1), stream(2), row(SUB), stream(1), stream(2),
                  row(B_QW), row(C_VW), wspec(SUB), wspec(B_QW), wspec(C_VW)],
        out_specs=pl.BlockSpec((tm, tn), lambda i, j: (i, j)),
        out_shape=jax.ShapeDtypeStruct((t, D_MODEL), BF16),
        scratch_shapes=[pltpu.VMEM((tm, SUB), BF16)] + [pltpu.VMEM((SUB // 128, tm, 128), F32)] * 4,
        compiler_params=_cparams(("parallel", "arbitrary")),
        name="merge",
    )(h, w_in, w_in, w_in, *oa, *lse, ob, oc, wa, wb, wc)


def _out_kernel(m_ref, w_ref, x_ref, gpost_ref, gnext_ref, x1_ref, h_ref):
    for rows in _row_chunks(m_ref.shape[0]):
        y = jnp.dot(m_ref[rows, :], w_ref[...], preferred_element_type=F32)
        x1 = x_ref[rows, :] + _rms_rows(y, gpost_ref[...])
        x1_ref[rows, :] = x1
        h_ref[rows, :] = _rms_rows(x1, gnext_ref[...]).astype(h_ref.dtype)


def _out_proj(merged, w_out, li, x, g_post, g_next, tm=512):
    t = x.shape[0]
    row = lambda i: (i, 0)
    const = lambda i: (0, 0)
    return pl.pallas_call(
        _out_kernel,
        grid=(t // tm,),
        in_specs=[pl.BlockSpec((tm, D_MODEL), row),
                  pl.BlockSpec((None, D_MODEL, D_MODEL), lambda i: (li, 0, 0)),
                  pl.BlockSpec((tm, D_MODEL), row),
                  pl.BlockSpec((1, D_MODEL), const),
                  pl.BlockSpec((1, D_MODEL), const)],
        out_specs=[pl.BlockSpec((tm, D_MODEL), row), pl.BlockSpec((tm, D_MODEL), row)],
        out_shape=[jax.ShapeDtypeStruct((t, D_MODEL), F32), jax.ShapeDtypeStruct((t, D_MODEL), BF16)],
        compiler_params=_cparams(("parallel",)),
        name="out_proj",
    )(merged, w_out, x, g_post.reshape(1, D_MODEL), g_next.reshape(1, D_MODEL))


def _mlp_kernel(h_ref, w1_ref, w2_ref, x_ref, gpost_ref, *rest, with_next):
    if with_next:
        gnext_ref, x2_ref, hn_ref, acc_ref = rest
    else:
        x2_ref, acc_ref = rest
    f = pl.program_id(1)

    @pl.when(f == 0)
    def _():
        acc_ref[...] = jnp.zeros_like(acc_ref)

    def partial_out(rows):
        u = jnp.dot(h_ref[rows, :], w1_ref[...], preferred_element_type=F32)
        u = jnp.square(jnp.maximum(u, 0.0)).astype(BF16)
        return acc_ref[rows, :] + jnp.dot(u, w2_ref[...], preferred_element_type=F32)

    last = pl.num_programs(1) - 1

    @pl.when(f < last)
    def _():
        acc_ref[...] = partial_out(slice(None))

    @pl.when(f == last)
    def _():
        for rows in _row_chunks(h_ref.shape[0]):
            x2 = x_ref[rows, :] + _rms_rows(partial_out(rows), gpost_ref[...])
            x2_ref[rows, :] = x2
            if with_next:
                hn_ref[rows, :] = _rms_rows(x2, gnext_ref[...]).astype(hn_ref.dtype)


def _mlp(h, w1, w2, li, x, g_post, g_next, tm=512, tf=1024):
    t = x.shape[0]
    with_next = g_next is not None
    row = lambda i, f: (i, 0)
    const = lambda i, f: (0, 0)
    in_specs = [pl.BlockSpec((tm, D_MODEL), row),
                pl.BlockSpec((None, D_MODEL, tf), lambda i, f: (li, 0, f)),
                pl.BlockSpec((None, tf, D_MODEL), lambda i, f: (li, f, 0)),
                pl.BlockSpec((tm, D_MODEL), row),
                pl.BlockSpec((1, D_MODEL), const)]
    args = [h, w1, w2, x, g_post.reshape(1, D_MODEL)]
    out_specs = [pl.BlockSpec((tm, D_MODEL), row)]
    out_shape = [jax.ShapeDtypeStruct((t, D_MODEL), F32)]
    if with_next:
        in_specs.append(pl.BlockSpec((1, D_MODEL), const))
        args.append(g_next.reshape(1, D_MODEL))
        out_specs.append(pl.BlockSpec((tm, D_MODEL), row))
        out_shape.append(jax.ShapeDtypeStruct((t, D_MODEL), BF16))
    res = pl.pallas_call(
        functools.partial(_mlp_kernel, with_next=with_next),
        grid=(t // tm, D_FF // tf),
        in_specs=in_specs,
        out_specs=out_specs,
        out_shape=out_shape,
        scratch_shapes=[pltpu.VMEM((tm, D_MODEL), F32)],
        compiler_params=_cparams(("parallel", "arbitrary")),
        name="mlp",
    )(*args)
    return (res[0], res[1]) if with_next else (res[0], None)


def _rope_tables(pos, dim):
    inv = 1.0 / (ROPE_THETA ** (jnp.arange(0, dim, 2, dtype=F32) / dim))
    ang = pos.astype(F32)[:, None] * inv[None, :]
    return jnp.cos(ang), jnp.sin(ang)


def _tables(seq):
    pos = jnp.arange(seq)
    ca, sa = _rope_tables(pos, HEAD_DIM)
    cos_a = jnp.concatenate([ca, ca], axis=1)
    sin_a = jnp.concatenate([-sa, sa], axis=1)
    z32 = jnp.zeros((seq, 32), F32)

    def half32(c, s):
        return jnp.concatenate([c, c], 1), jnp.concatenate([-s, z32], 1), jnp.concatenate([z32, s], 1)

    cr, sr = _rope_tables(pos // GRID_W, AXIAL_DIM)
    cc, sc = _rope_tables(pos % GRID_W, AXIAL_DIM)
    r_c, r_lo, r_hi = half32(cr, sr)
    c_c, c_lo, c_hi = half32(cc, sc)
    cos_b = jnp.concatenate([r_c, c_c], 1)
    sin_b_lo = jnp.concatenate([r_lo, c_lo], 1)
    sin_b_hi = jnp.concatenate([r_hi, c_hi], 1)
    c3, s3 = _rope_tables(pos, C_QK_DIM)
    m_c, m_lo, m_hi = half32(c3, s3)
    cos_c = jnp.concatenate([m_c, m_c], 1)
    sin_c_lo = jnp.concatenate([m_lo, m_lo], 1)
    sin_c_hi = jnp.concatenate([m_hi, m_hi], 1)
    return jnp.stack([cos_a, sin_a, cos_b, sin_b_lo, sin_b_hi, cos_c, sin_c_lo, sin_c_hi], axis=0)


def _trunk(x3, tabs, p):
    batch, seq, _ = x3.shape
    x = x3.reshape(batch * seq, D_MODEL)
    h = _prenorm(x, p["g_mix_pre"][0])
    for li in range(DEPTH):
        lam_init = 0.8 - 0.6 * math.exp(-0.3 * li)
        qkv, st1, st2 = _qkv_proj(h, p["w_in"], li, tabs, p["b_q_norm"][li], p["b_k_norm"][li],
                                  batch, seq)
        oa, lse = zip(*[_attn_a_group(src, gi, batch, seq) for gi, src in enumerate((qkv, st1, st2))])
        ob = _attn_b(qkv, batch, seq)
        oc = _attn_c(qkv, p["c_lambda"][li], p["c_head_norm"][li], lam_init, batch, seq)
        merged = _merge(h, p["w_in"], li, oa, lse, ob, oc,
                        p["w_branch_a"], p["w_branch_b"], p["w_branch_c"], seq)
        x, hm = _out_proj(merged, p["w_out"], li, x, p["g_mix_post"][li], p["g_mlp_pre"][li])
        g_next = p["g_mix_pre"][li + 1] if li + 1 < DEPTH else None
        x, h = _mlp(hm, p["w_mlp_in"], p["w_mlp_out"], li, x, p["g_mlp_post"][li], g_next)
    return x.reshape(batch, seq, D_MODEL)


def kernel(x_prompt, x_sample, g_mix_pre, w_in, b_q_norm, b_k_norm, c_lambda, c_head_norm,
           w_branch_a, w_branch_b, w_branch_c, w_out, g_mix_post,
           g_mlp_pre, w_mlp_in, w_mlp_out, g_mlp_post):
    p = dict(g_mix_pre=g_mix_pre, w_in=w_in.astype(BF16), b_q_norm=b_q_norm, b_k_norm=b_k_norm,
             c_lambda=c_lambda, c_head_norm=c_head_norm,
             w_branch_a=w_branch_a.astype(BF16), w_branch_b=w_branch_b.astype(BF16),
             w_branch_c=w_branch_c.astype(BF16), w_out=w_out.astype(BF16), g_mix_post=g_mix_post,
             g_mlp_pre=g_mlp_pre, w_mlp_in=w_mlp_in.astype(BF16), w_mlp_out=w_mlp_out.astype(BF16),
             g_mlp_post=g_mlp_post)
    tabs = _tables(max(x_prompt.shape[1], x_sample.shape[1]))
    return (_trunk(x_prompt, tabs, p), _trunk(x_sample, tabs, p))
```

```python
import functools
import math

import jax
import jax.numpy as jnp
from jax import lax
from jax.experimental import pallas as pl
from jax.experimental.pallas import tpu as pltpu

F32 = jnp.float32
BF16 = jnp.bfloat16

D_MODEL = 2048
DEPTH = 2
HEAD_DIM = 128
GRID_W = 64
ROPE_THETA = 10000.0
EPS = 1e-6
NEG_INF = -1e30
LOG2E = math.log2(math.e)

DIL_GROUPS = ((128, 1), (512, 4), (2048, 16))
A_HALF = 64
A_W = 768
B_Q_HEADS = 6
B_KV_HEADS = 2
B_GROUP = B_Q_HEADS // B_KV_HEADS
B_QW = 768
B_KVW = 256
AXIAL_DIM = 64
C_HEADS = 4
C_QK_DIM = 64
C_QW = 512
C_VW = 512
N_BRANCH = 3
QKV_COLS = 3 * A_W + B_QW + 2 * B_KVW + 2 * C_QW + C_VW
GATE_COLS = N_BRANCH * D_MODEL
D_FF = 4 * D_MODEL

OFF_AQ, OFF_AK, OFF_AV = 0, 768, 1536
OFF_BQ, OFF_BK, OFF_BV = 2304, 3072, 3328
OFF_CQ, OFF_CK, OFF_CV = 3584, 4096, 4608

SUB = 256
_QKV_KINDS = (("aq",) * 3 + ("ak",) * 3 + ("v",) * 3 + ("bq",) * 3 + ("bk",) + ("v",)
              + ("cq",) * 2 + ("ck",) * 2 + ("v",) * 2)

VMEM_LIMIT = 56 * 1024 * 1024


def _cparams(sem):
    return pltpu.CompilerParams(dimension_semantics=sem, vmem_limit_bytes=VMEM_LIMIT)


def _rms_rows(xf, g):
    ms = jnp.mean(xf * xf, axis=-1, keepdims=True)
    return xf * lax.rsqrt(ms + EPS) * g


def _rope_half64(x, cos, sin_signed):
    return x * cos + pltpu.roll(x, 64, axis=1) * sin_signed


def _rope_half32(x, cos, sin_lo, sin_hi):
    return x * cos + pltpu.roll(x, 96, axis=1) * sin_lo + pltpu.roll(x, 32, axis=1) * sin_hi


def _norm_kernel(x_ref, g_ref, h_ref):
    h_ref[...] = _rms_rows(x_ref[...], g_ref[...]).astype(h_ref.dtype)


def _prenorm(x, g, tm=1024):
    t = x.shape[0]
    return pl.pallas_call(
        _norm_kernel,
        grid=(t // tm,),
        in_specs=[pl.BlockSpec((tm, D_MODEL), lambda i: (i, 0)),
                  pl.BlockSpec((1, D_MODEL), lambda i: (0, 0))],
        out_specs=pl.BlockSpec((tm, D_MODEL), lambda i: (i, 0)),
        out_shape=jax.ShapeDtypeStruct((t, D_MODEL), BF16),
        compiler_params=_cparams(("parallel",)),
        name="prenorm",
    )(x, g.reshape(1, D_MODEL))


_STREAM_TILES = {1: (1, 4, 7), 2: (2, 5, 8)}


ROW_CHUNK = 256


def _row_chunks(n):
    return [slice(r, r + ROW_CHUNK) for r in range(0, n, ROW_CHUNK)]


def _qkv_kernel(h_ref, w_ref, tab_ref, gq_ref, gk_ref, o_ref, s1_ref, s2_ref, epi_scr, *, tm):
    def head_epilogue(x, kind, rows):
        if kind in ("aq", "ak"):
            y = _rope_half64(x, tab_ref[0, rows, :], tab_ref[1, rows, :])
            return y * (HEAD_DIM ** -0.5 * LOG2E) if kind == "aq" else y
        if kind in ("bq", "bk"):
            g = gq_ref[...] * (HEAD_DIM ** -0.5 * LOG2E) if kind == "bq" else gk_ref[...]
            return _rope_half32(_rms_rows(x, g), tab_ref[2, rows, :], tab_ref[3, rows, :], tab_ref[4, rows, :])
        if kind in ("cq", "ck"):
            y = _rope_half32(x, tab_ref[5, rows, :], tab_ref[6, rows, :], tab_ref[7, rows, :])
            return y * (C_QK_DIM ** -0.5 * LOG2E) if kind == "cq" else y
        return x

    n_heads = SUB // 128
    for jj, kind in enumerate(_QKV_KINDS):
        g = next((g for g in (1, 2) if jj in _STREAM_TILES[g]), None)
        for c, rows in enumerate(_row_chunks(tm)):
            acc = jnp.dot(h_ref[rows, :], w_ref[:, jj * SUB:(jj + 1) * SUB], preferred_element_type=F32)
            for hh in range(n_heads):
                sl = slice(hh * 128, (hh + 1) * 128)
                y = head_epilogue(acc[:, sl], kind, rows)
                o_ref[rows, jj * SUB + hh * 128:jj * SUB + (hh + 1) * 128] = y.astype(o_ref.dtype)
                if g is not None:
                    ref = (s1_ref, s2_ref)[g - 1]
                    dil = DIL_GROUPS[g][1]
                    nu = ROW_CHUNK // dil
                    slot = (jj % 2) * n_heads + hh
                    col0 = _STREAM_TILES[g].index(jj) * SUB + hh * 128
                    epi_scr[slot, rows, :] = y
                    for r in range(dil):
                        rows_r = epi_scr[slot, pl.ds(rows.start + r, nu, stride=dil), :]
                        ref[r, c * nu:(c + 1) * nu, col0:col0 + 128] = rows_r.astype(ref.dtype)


def _qkv_proj(h, w_in, li, tabs, gq, gk, batch, seq, tm=512):
    t = h.shape[0]
    nseq = seq // tm
    d1, d2 = DIL_GROUPS[1][1], DIL_GROUPS[2][1]

    def stream_spec(dil):
        return pl.BlockSpec((None, dil, tm // dil, 3 * SUB), lambda i: (i // nseq, 0, i % nseq, 0))

    return pl.pallas_call(
        functools.partial(_qkv_kernel, tm=tm),
        grid=(t // tm,),
        in_specs=[pl.BlockSpec((tm, D_MODEL), lambda i: (i, 0)),
                  pl.BlockSpec((None, D_MODEL, QKV_COLS), lambda i: (li, 0, 0),
                               pipeline_mode=pl.Buffered(1)),
                  pl.BlockSpec((8, tm, 128), lambda i: (0, i % nseq, 0)),
                  pl.BlockSpec((1, 128), lambda i: (0, 0)),
                  pl.BlockSpec((1, 128), lambda i: (0, 0))],
        out_specs=[pl.BlockSpec((tm, QKV_COLS), lambda i: (i, 0)), stream_spec(d1), stream_spec(d2)],
        out_shape=[jax.ShapeDtypeStruct((t, QKV_COLS), BF16),
                   jax.ShapeDtypeStruct((batch, d1, seq // d1, 3 * SUB), BF16),
                   jax.ShapeDtypeStruct((batch, d2, seq // d2, 3 * SUB), BF16)],
        scratch_shapes=[pltpu.VMEM((2 * (SUB // 128), tm, 128), F32)],
        compiler_params=_cparams(("parallel",)),
        name="qkv_proj",
    )(h, w_in, tabs, gq.reshape(1, 128), gk.reshape(1, 128))


A_TQ = 128
A_TK = A_TQ + 2 * A_HALF


def _attn_a_kernel(q_ref, k_ref, v_ref, o_ref, lse_ref, *, length):
    nt = length // A_TQ
    rows = lax.broadcasted_iota(jnp.int32, (A_TQ, A_TK), 0)
    cols = lax.broadcasted_iota(jnp.int32, (A_TQ, A_TK), 1)

    def body(i, carry):
        q0 = pl.multiple_of(i * A_TQ, A_TQ)
        k0 = pl.multiple_of(jnp.clip(q0 - A_HALF, 0, length - A_TK), 16)
        valid = jnp.abs(rows - cols + (q0 - k0)) <= A_HALF
        for hh in range(2):
            sl = slice(hh * 128, (hh + 1) * 128)
            q = q_ref[pl.ds(q0, A_TQ), sl]
            k = k_ref[pl.ds(k0, A_TK), sl]
            v = v_ref[pl.ds(k0, A_TK), sl]
            s = lax.dot_general(q, k, (((1,), (1,)), ((), ())), preferred_element_type=F32)
            s = jnp.where(valid, s, NEG_INF)
            m = jnp.max(s, axis=1, keepdims=True)
            e = jnp.exp2(s - m)
            den = jnp.sum(e, axis=1, keepdims=True)
            o = jnp.dot(e.astype(BF16), v, preferred_element_type=F32) / den
            o_ref[pl.ds(q0, A_TQ), sl] = o.astype(o_ref.dtype)
            lse_ref[pl.ds(q0, A_TQ), sl] = jnp.broadcast_to(m + jnp.log2(den), (A_TQ, 128))
        return carry

    lax.fori_loop(0, nt, body, 0, unroll=min(nt, 4))


def _attn_a_group(src, gi, batch, seq):
    dil = DIL_GROUPS[gi][1]
    length = seq // dil
    kern = functools.partial(_attn_a_kernel, length=length)
    if dil == 1:
        spec = lambda c: pl.BlockSpec((seq, SUB), lambda b: (b, c))
        out_spec = pl.BlockSpec((seq, SUB), lambda b: (b, 0))
        return pl.pallas_call(
            kern, grid=(batch,),
            in_specs=[spec(OFF_AQ // SUB + gi), spec(OFF_AK // SUB + gi), spec(OFF_AV // SUB + gi)],
            out_specs=[out_spec, out_spec],
            out_shape=[jax.ShapeDtypeStruct((batch * seq, SUB), BF16),
                       jax.ShapeDtypeStruct((batch * seq, SUB), F32)],
            compiler_params=_cparams(("parallel",)),
            name=f"attn_a{gi}",
        )(src, src, src)
    spec = lambda c: pl.BlockSpec((None, None, length, SUB), lambda b, r: (b, r, 0, c))
    out_spec = pl.BlockSpec((None, None, length, SUB), lambda b, r: (b, r, 0, 0))
    return pl.pallas_call(
        kern, grid=(batch, dil),
        in_specs=[spec(0), spec(1), spec(2)],
        out_specs=[out_spec, out_spec],
        out_shape=[jax.ShapeDtypeStruct((batch, dil, length, SUB), BF16),
                   jax.ShapeDtypeStruct((batch, dil, length, SUB), F32)],
        compiler_params=_cparams(("parallel", "parallel")),
        name=f"attn_a{gi}",
    )(src, src, src)


FLASH_BLOCK = 256


def _transpose_values(v_ref, vt_scr, tk):
    for t in range(vt_scr.shape[0]):
        vt_scr[t] = v_ref[t * tk:(t + 1) * tk, :].astype(F32).T.astype(vt_scr.dtype)


def _flash_scratch(seq, tk, rows):
    return ([pltpu.VMEM((seq // tk, HEAD_DIM, tk), BF16),
             pltpu.VMEM((HEAD_DIM, rows), F32)])


def _flash_keymajor(q, k_ref, vt_scr, acc_scr, tk):
    rows = q.shape[0]
    nk = vt_scr.shape[0]
    q_blocks = [slice(c, c + FLASH_BLOCK) for c in range(0, rows, FLASH_BLOCK)]
    k_chunks = [slice(c, c + FLASH_BLOCK) for c in range(0, tk, FLASH_BLOCK)]

    def scores(t, qb):
        k = k_ref[t * tk:(t + 1) * tk, :]
        return lax.dot_general(k, q[qb, :], (((1,), (1,)), ((), ())), preferred_element_type=F32)

    def update(m, l, s, t, qb):
        m_new = m
        for kc in k_chunks:
            m_new = jnp.maximum(m_new, jnp.max(s[kc, :], axis=0, keepdims=True))
        alpha = jnp.exp2(m - m_new)
        l_tile = pv = None
        for kc in k_chunks:
            p = jnp.exp2(s[kc, :] - m_new)
            p_sum = jnp.sum(p, axis=0, keepdims=True)
            d = jnp.dot(vt_scr[t, :, kc], p.astype(BF16), preferred_element_type=F32)
            l_tile = p_sum if l_tile is None else l_tile + p_sum
            pv = d if pv is None else pv + d
        acc_scr[:, qb] = pv if t == 0 else alpha * acc_scr[:, qb] + pv
        return m_new, alpha * l + l_tile

    m = [jnp.full((1, FLASH_BLOCK), NEG_INF, F32) for _ in q_blocks]
    l = [jnp.zeros((1, FLASH_BLOCK), F32) for _ in q_blocks]
    s_cur = [scores(0, qb) for qb in q_blocks]
    for t in range(nk):
        s_next = []
        for bi, qb in enumerate(q_blocks):
            if t + 1 < nk:
                s_next.append(scores(t + 1, qb))
            m[bi], l[bi] = update(m[bi], l[bi], s_cur[bi], t, qb)
        s_cur = s_next
    return acc_scr[...], jnp.concatenate(l, axis=1)


def _attn_b_kernel(q_ref, k_ref, v_ref, o_ref, vt_scr, acc_scr, *, tq, tk):
    @pl.when(pl.program_id(2) == 0)
    def _():
        _transpose_values(v_ref, vt_scr, tk)

    q = jnp.concatenate([q_ref[:, h * 128:(h + 1) * 128] for h in range(B_GROUP)], axis=0)
    acc, l = _flash_keymajor(q, k_ref, vt_scr, acc_scr, tk)
    o = acc * (1.0 / l)
    for h in range(B_GROUP):
        o_ref[:, h * 128:(h + 1) * 128] = o[:, h * tq:(h + 1) * tq].T.astype(o_ref.dtype)


def _attn_b(qkv, batch, seq, tq=512, tk=1024):
    nq = seq // tq
    gw = B_GROUP * 128
    rows = B_GROUP * tq
    return pl.pallas_call(
        functools.partial(_attn_b_kernel, tq=tq, tk=tk),
        grid=(batch, B_KV_HEADS, nq),
        in_specs=[pl.BlockSpec((tq, gw), lambda b, kh, i: (b * nq + i, OFF_BQ // gw + kh)),
                  pl.BlockSpec((seq, 128), lambda b, kh, i: (b, OFF_BK // 128 + kh)),
                  pl.BlockSpec((seq, 128), lambda b, kh, i: (b, OFF_BV // 128 + kh))],
        out_specs=pl.BlockSpec((tq, gw), lambda b, kh, i: (b * nq + i, kh)),
        out_shape=jax.ShapeDtypeStruct((batch * seq, B_QW), BF16),
        scratch_shapes=_flash_scratch(seq, tk, rows),
        compiler_params=_cparams(("parallel", "parallel", "arbitrary")),
        name="attn_b",
    )(qkv, qkv, qkv)


def _attn_c_kernel(lam_ref, g_ref, q_ref, k_ref, v_ref, o_ref, vt_scr, acc_scr, *, tq, tk, lam_init):
    @pl.when(pl.program_id(2) == 0)
    def _():
        _transpose_values(v_ref, vt_scr, tk)

    lamv = lam_ref[...]
    lam = (jnp.exp(jnp.sum(lamv[0:1] * lamv[1:2], axis=1, keepdims=True))
           - jnp.exp(jnp.sum(lamv[2:3] * lamv[3:4], axis=1, keepdims=True)) + lam_init)
    q = q_ref[...]
    lane = lax.broadcasted_iota(jnp.int32, q.shape, 1)
    zero = jnp.zeros_like(q)
    q2 = jnp.concatenate([jnp.where(lane < C_QK_DIM, q, zero), jnp.where(lane >= C_QK_DIM, q, zero)], axis=0)
    acc, l = _flash_keymajor(q2, k_ref, vt_scr, acc_scr, tk)
    o = acc * (1.0 / l)
    oc = (o[:, :tq] - lam * o[:, tq:]).T
    o_ref[...] = (_rms_rows(oc, g_ref[...]) * (1.0 - lam_init)).astype(o_ref.dtype)


def _attn_c(qkv, c_lambda, c_head_norm, lam_init, batch, seq, tq=512, tk=1024):
    nq = seq // tq
    rows = 2 * tq
    return pl.pallas_call(
        functools.partial(_attn_c_kernel, tq=tq, tk=tk, lam_init=lam_init),
        grid=(batch, C_HEADS, nq),
        in_specs=[pl.BlockSpec((4, C_QK_DIM), lambda b, h, i: (0, 0)),
                  pl.BlockSpec((1, 128), lambda b, h, i: (0, 0)),
                  pl.BlockSpec((tq, 128), lambda b, h, i: (b * nq + i, OFF_CQ // 128 + h)),
                  pl.BlockSpec((seq, 128), lambda b, h, i: (b, OFF_CK // 128 + h)),
                  pl.BlockSpec((seq, 128), lambda b, h, i: (b, OFF_CV // 128 + h))],
        out_specs=pl.BlockSpec((tq, 128), lambda b, h, i: (b * nq + i, h)),
        out_shape=jax.ShapeDtypeStruct((batch * seq, C_VW), BF16),
        scratch_shapes=_flash_scratch(seq, tk, rows),
        compiler_params=_cparams(("parallel", "parallel", "arbitrary")),
        name="attn_c",
    )(c_lambda, c_head_norm.reshape(1, 128), qkv, qkv, qkv)


def _merge_kernel(h_ref, wg0, wg1, wg2, oa0, oa1, oa2, ls0, ls1, ls2, ob, oc, wa, wb, wc, out_ref,
                  oa_scr, o1_scr, o2_scr, l1_scr, l2_scr, *, tm):
    @pl.when(pl.program_id(1) == 0)
    def _():
        for g, o_src, l_src, o_dst, l_dst in ((1, oa1, ls1, o1_scr, l1_scr), (2, oa2, ls2, o2_scr, l2_scr)):
            dil = DIL_GROUPS[g][1]
            for r in range(dil):
                for hh in range(SUB // 128):
                    sl = slice(hh * 128, (hh + 1) * 128)
                    o_dst[hh, pl.ds(r, tm // dil, stride=dil), :] = o_src[r, :, sl].astype(F32)
                    l_dst[hh, pl.ds(r, tm // dil, stride=dil), :] = l_src[r, :, sl]
        for hh in range(SUB // 128):
            sl = slice(hh * 128, (hh + 1) * 128)
            l0, l1, l2 = ls0[:, sl], l1_scr[hh], l2_scr[hh]
            m = jnp.maximum(jnp.maximum(l0, l1), l2)
            e0, e1, e2 = jnp.exp2(l0 - m), jnp.exp2(l1 - m), jnp.exp2(l2 - m)
            num = e0 * oa0[:, sl].astype(F32) + e1 * o1_scr[hh] + e2 * o2_scr[hh]
            oa_scr[:, sl] = (num / (e0 + e1 + e2)).astype(oa_scr.dtype)

    for rows in _row_chunks(tm):
        hr = h_ref[rows, :]
        merged = None
        for wg, o_br, w_br in ((wg0, oa_scr, wa), (wg1, ob, wb), (wg2, oc, wc)):
            gate = jax.nn.sigmoid(jnp.dot(hr, wg[...], preferred_element_type=F32))
            term = gate * jnp.dot(o_br[rows, :], w_br[...], preferred_element_type=F32)
            merged = term if merged is None else merged + term
        out_ref[rows, :] = merged.astype(out_ref.dtype)


def _merge(h, w_in, li, oa, lse, ob, oc, wa, wb, wc, seq, tm=1024, tn=512):
    t = ob.shape[0]
    nj = D_MODEL // tn
    nseq = seq // tm

    def row(w):
        return pl.BlockSpec((tm, w), lambda i, j: (i, 0))

    def stream(g):
        dil = DIL_GROUPS[g][1]
        return pl.BlockSpec((None, dil, tm // dil, SUB), lambda i, j: (i // nseq, 0, i % nseq, 0))

    def gate_w(br):
        return pl.BlockSpec((None, D_MODEL, tn),
                            lambda i, j: (li, 0, (QKV_COLS + br * D_MODEL) // tn + j))

    def wspec(k):
        return pl.BlockSpec((None, k, tn), lambda i, j: (li, 0, j))

    return pl.pallas_call(
        functools.partial(_merge_kernel, tm=tm),
        grid=(t // tm, nj),
        in_specs=[row(D_MODEL), gate_w(0), gate_w(1), gate_w(2),
                  row(SUB), stream(1), stream(2), row(SUB), stream(1), stream(2),
                  row(B_QW), row(C_VW), wspec(SUB), wspec(B_QW), wspec(C_VW)],
        out_specs=pl.BlockSpec((tm, tn), lambda i, j: (i, j)),
        out_shape=jax.ShapeDtypeStruct((t, D_MODEL), BF16),
        scratch_shapes=[pltpu.VMEM((tm, SUB), BF16)] + [pltpu.VMEM((SUB // 128, tm, 128), F32)] * 4,
        compiler_params=_cparams(("parallel", "arbitrary")),
        name="merge",
    )(h, w_in, w_in, w_in, *oa, *lse, ob, oc, wa, wb, wc)


def _out_kernel(m_ref, w_ref, x_ref, gpost_ref, gnext_ref, x1_ref, h_ref):
    for rows in _row_chunks(m_ref.shape[0]):
        y = jnp.dot(m_ref[rows, :], w_ref[...], preferred_element_type=F32)
        x1 = x_ref[rows, :] + _rms_rows(y, gpost_ref[...])
        x1_ref[rows, :] = x1
        h_ref[rows, :] = _rms_rows(x1, gnext_ref[...]).astype(h_ref.dtype)


def _out_proj(merged, w_out, li, x, g_post, g_next, tm=512):
    t = x.shape[0]
    row = lambda i: (i, 0)
    const = lambda i: (0, 0)
    return pl.pallas_call(
        _out_kernel,
        grid=(t // tm,),
        in_specs=[pl.BlockSpec((tm, D_MODEL), row),
                  pl.BlockSpec((None, D_MODEL, D_MODEL), lambda i: (li, 0, 0)),
                  pl.BlockSpec((tm, D_MODEL), row),
                  pl.BlockSpec((1, D_MODEL), const),
                  pl.BlockSpec((1, D_MODEL), const)],
        out_specs=[pl.BlockSpec((tm, D_MODEL), row), pl.BlockSpec((tm, D_MODEL), row)],
        out_shape=[jax.ShapeDtypeStruct((t, D_MODEL), F32), jax.ShapeDtypeStruct((t, D_MODEL), BF16)],
        compiler_params=_cparams(("parallel",)),
        name="out_proj",
    )(merged, w_out, x, g_post.reshape(1, D_MODEL), g_next.reshape(1, D_MODEL))


def _mlp_kernel(h_ref, w1_ref, w2_ref, x_ref, gpost_ref, *rest, with_next):
    if with_next:
        gnext_ref, x2_ref, hn_ref, acc_ref = rest
    else:
        x2_ref, acc_ref = rest
    f = pl.program_id(1)

    @pl.when(f == 0)
    def _():
        acc_ref[...] = jnp.zeros_like(acc_ref)

    def partial_out(rows):
        u = jnp.dot(h_ref[rows, :], w1_ref[...], preferred_element_type=F32)
        u = jnp.square(jnp.maximum(u, 0.0)).astype(BF16)
        return acc_ref[rows, :] + jnp.dot(u, w2_ref[...], preferred_element_type=F32)

    last = pl.num_programs(1) - 1

    @pl.when(f < last)
    def _():
        acc_ref[...] = partial_out(slice(None))

    @pl.when(f == last)
    def _():
        for rows in _row_chunks(h_ref.shape[0]):
            x2 = x_ref[rows, :] + _rms_rows(partial_out(rows), gpost_ref[...])
            x2_ref[rows, :] = x2
            if with_next:
                hn_ref[rows, :] = _rms_rows(x2, gnext_ref[...]).astype(hn_ref.dtype)


def _mlp(h, w1, w2, li, x, g_post, g_next, tm=512, tf=1024):
    t = x.shape[0]
    with_next = g_next is not None
    row = lambda i, f: (i, 0)
    const = lambda i, f: (0, 0)
    in_specs = [pl.BlockSpec((tm, D_MODEL), row),
                pl.BlockSpec((None, D_MODEL, tf), lambda i, f: (li, 0, f)),
                pl.BlockSpec((None, tf, D_MODEL), lambda i, f: (li, f, 0)),
                pl.BlockSpec((tm, D_MODEL), row),
                pl.BlockSpec((1, D_MODEL), const)]
    args = [h, w1, w2, x, g_post.reshape(1, D_MODEL)]
    out_specs = [pl.BlockSpec((tm, D_MODEL), row)]
    out_shape = [jax.ShapeDtypeStruct((t, D_MODEL), F32)]
    if with_next:
        in_specs.append(pl.BlockSpec((1, D_MODEL), const))
        args.append(g_next.reshape(1, D_MODEL))
        out_specs.append(pl.BlockSpec((tm, D_MODEL), row))
        out_shape.append(jax.ShapeDtypeStruct((t, D_MODEL), BF16))
    res = pl.pallas_call(
        functools.partial(_mlp_kernel, with_next=with_next),
        grid=(t // tm, D_FF // tf),
        in_specs=in_specs,
        out_specs=out_specs,
        out_shape=out_shape,
        scratch_shapes=[pltpu.VMEM((tm, D_MODEL), F32)],
        compiler_params=_cparams(("parallel", "arbitrary")),
        name="mlp",
    )(*args)
    return (res[0], res[1]) if with_next else (res[0], None)


def _rope_tables(pos, dim):
    inv = 1.0 / (ROPE_THETA ** (jnp.arange(0, dim, 2, dtype=F32) / dim))
    ang = pos.astype(F32)[:, None] * inv[None, :]
    return jnp.cos(ang), jnp.sin(ang)


def _tables(seq):
    pos = jnp.arange(seq)
    ca, sa = _rope_tables(pos, HEAD_DIM)
    cos_a = jnp.concatenate([ca, ca], axis=1)
    sin_a = jnp.concatenate([-sa, sa], axis=1)
    z32 = jnp.zeros((seq, 32), F32)

    def half32(c, s):
        return jnp.concatenate([c, c], 1), jnp.concatenate([-s, z32], 1), jnp.concatenate([z32, s], 1)

    cr, sr = _rope_tables(pos // GRID_W, AXIAL_DIM)
    cc, sc = _rope_tables(pos % GRID_W, AXIAL_DIM)
    r_c, r_lo, r_hi = half32(cr, sr)
    c_c, c_lo, c_hi = half32(cc, sc)
    cos_b = jnp.concatenate([r_c, c_c], 1)
    sin_b_lo = jnp.concatenate([r_lo, c_lo], 1)
    sin_b_hi = jnp.concatenate([r_hi, c_hi], 1)
    c3, s3 = _rope_tables(pos, C_QK_DIM)
    m_c, m_lo, m_hi = half32(c3, s3)
    cos_c = jnp.concatenate([m_c, m_c], 1)
    sin_c_lo = jnp.concatenate([m_lo, m_lo], 1)
    sin_c_hi = jnp.concatenate([m_hi, m_hi], 1)
    return jnp.stack([cos_a, sin_a, cos_b, sin_b_lo, sin_b_hi, cos_c, sin_c_lo, sin_c_hi], axis=0)


def _trunk(x3, tabs, p):
    batch, seq, _ = x3.shape
    x = x3.reshape(batch * seq, D_MODEL)
    h = _prenorm(x, p["g_mix_pre"][0])
    for li in range(DEPTH):
        lam_init = 0.8 - 0.6 * math.exp(-0.3 * li)
        qkv, st1, st2 = _qkv_proj(h, p["w_in"], li, tabs, p["b_q_norm"][li], p["b_k_norm"][li],
                                  batch, seq)
        oa, lse = zip(*[_attn_a_group(src, gi, batch, seq) for gi, src in enumerate((qkv, st1, st2))])
        ob = _attn_b(qkv, batch, seq)
        oc = _attn_c(qkv, p["c_lambda"][li], p["c_head_norm"][li], lam_init, batch, seq)
        merged = _merge(h, p["w_in"], li, oa, lse, ob, oc,
                        p["w_branch_a"], p["w_branch_b"], p["w_branch_c"], seq)
        x, hm = _out_proj(merged, p["w_out"], li, x, p["g_mix_post"][li], p["g_mlp_pre"][li])
        g_next = p["g_mix_pre"][li + 1] if li + 1 < DEPTH else None
        x, h = _mlp(hm, p["w_mlp_in"], p["w_mlp_out"], li, x, p["g_mlp_post"][li], g_next)
    return x.reshape(batch, seq, D_MODEL)


def kernel(x_prompt, x_sample, g_mix_pre, w_in, b_q_norm, b_k_norm, c_lambda, c_head_norm,
           w_branch_a, w_branch_b, w_branch_c, w_out, g_mix_post,
           g_mlp_pre, w_mlp_in, w_mlp_out, g_mlp_post):
    p = dict(g_mix_pre=g_mix_pre, w_in=w_in.astype(BF16), b_q_norm=b_q_norm, b_k_norm=b_k_norm,
             c_lambda=c_lambda, c_head_norm=c_head_norm,
             w_branch_a=w_branch_a.astype(BF16), w_branch_b=w_branch_b.astype(BF16),
             w_branch_c=w_branch_c.astype(BF16), w_out=w_out.astype(BF16), g_mix_post=g_mix_post,
             g_mlp_pre=g_mlp_pre, w_mlp_in=w_mlp_in.astype(BF16), w_mlp_out=w_mlp_out.astype(BF16),
             g_mlp_post=g_mlp_post)
    tabs = _tables(max(x_prompt.shape[1], x_sample.shape[1]))
    return (_trunk(x_prompt, tabs, p), _trunk(x_sample, tabs, p))
```

```python
import functools
import math

import jax
import jax.numpy as jnp
from jax import lax
from jax.experimental import pallas as pl
from jax.experimental.pallas import tpu as pltpu

F32 = jnp.float32
BF16 = jnp.bfloat16

D_MODEL = 2048
DEPTH = 2
HEAD_DIM = 128
GRID_W = 64
ROPE_THETA = 10000.0
EPS = 1e-6
NEG_INF = -1e30
LOG2E = math.log2(math.e)

DIL_GROUPS = ((128, 1), (512, 4), (2048, 16))
A_HALF = 64
A_W = 768
B_Q_HEADS = 6
B_KV_HEADS = 2
B_GROUP = B_Q_HEADS // B_KV_HEADS
B_QW = 768
B_KVW = 256
AXIAL_DIM = 64
C_HEADS = 4
C_QK_DIM = 64
C_QW = 512
C_VW = 512
N_BRANCH = 3
QKV_COLS = 3 * A_W + B_QW + 2 * B_KVW + 2 * C_QW + C_VW
GATE_COLS = N_BRANCH * D_MODEL
D_FF = 4 * D_MODEL

OFF_AQ, OFF_AK, OFF_AV = 0, 768, 1536
OFF_BQ, OFF_BK, OFF_BV = 2304, 3072, 3328
OFF_CQ, OFF_CK, OFF_CV = 3584, 4096, 4608

SUB = 256
_QKV_KINDS = (("aq",) * 3 + ("ak",) * 3 + ("v",) * 3 + ("bq",) * 3 + ("bk",) + ("v",)
              + ("cq",) * 2 + ("ck",) * 2 + ("v",) * 2)

VMEM_LIMIT = 56 * 1024 * 1024


def _cparams(sem):
    return pltpu.CompilerParams(dimension_semantics=sem, vmem_limit_bytes=VMEM_LIMIT)


def _rms_rows(xf, g):
    ms = jnp.mean(xf * xf, axis=-1, keepdims=True)
    return xf * lax.rsqrt(ms + EPS) * g


def _rope_half64(x, cos, sin_signed):
    return x * cos + pltpu.roll(x, 64, axis=1) * sin_signed


def _rope_half32(x, cos, sin_lo, sin_hi):
    return x * cos + pltpu.roll(x, 96, axis=1) * sin_lo + pltpu.roll(x, 32, axis=1) * sin_hi


def _norm_kernel(x_ref, g_ref, h_ref):
    h_ref[...] = _rms_rows(x_ref[...], g_ref[...]).astype(h_ref.dtype)


def _prenorm(x, g, tm=1024):
    t = x.shape[0]
    return pl.pallas_call(
        _norm_kernel,
        grid=(t // tm,),
        in_specs=[pl.BlockSpec((tm, D_MODEL), lambda i: (i, 0)),
                  pl.BlockSpec((1, D_MODEL), lambda i: (0, 0))],
        out_specs=pl.BlockSpec((tm, D_MODEL), lambda i: (i, 0)),
        out_shape=jax.ShapeDtypeStruct((t, D_MODEL), BF16),
        compiler_params=_cparams(("parallel",)),
        name="prenorm",
    )(x, g.reshape(1, D_MODEL))


_STREAM_TILES = {1: (1, 4, 7), 2: (2, 5, 8)}


ROW_CHUNK = 256


def _row_chunks(n):
    return [slice(r, r + ROW_CHUNK) for r in range(0, n, ROW_CHUNK)]


def _qkv_kernel(h_ref, w_ref, tab_ref, gq_ref, gk_ref, o_ref, s1_ref, s2_ref, epi_scr, *, tm):
    def head_epilogue(x, kind, rows):
        if kind in ("aq", "ak"):
            y = _rope_half64(x, tab_ref[0, rows, :], tab_ref[1, rows, :])
            return y * (HEAD_DIM ** -0.5 * LOG2E) if kind == "aq" else y
        if kind in ("bq", "bk"):
            g = gq_ref[...] * (HEAD_DIM ** -0.5 * LOG2E) if kind == "bq" else gk_ref[...]
            return _rope_half32(_rms_rows(x, g), tab_ref[2, rows, :], tab_ref[3, rows, :], tab_ref[4, rows, :])
        if kind in ("cq", "ck"):
            y = _rope_half32(x, tab_ref[5, rows, :], tab_ref[6, rows, :], tab_ref[7, rows, :])
            return y * (C_QK_DIM ** -0.5 * LOG2E) if kind == "cq" else y
        return x

    n_heads = SUB // 128
    for jj, kind in enumerate(_QKV_KINDS):
        g = next((g for g in (1, 2) if jj in _STREAM_TILES[g]), None)
        for c, rows in enumerate(_row_chunks(tm)):
            acc = jnp.dot(h_ref[rows, :], w_ref[:, jj * SUB:(jj + 1) * SUB], preferred_element_type=F32)
            for hh in range(n_heads):
                sl = slice(hh * 128, (hh + 1) * 128)
                y = head_epilogue(acc[:, sl], kind, rows)
                o_ref[rows, jj * SUB + hh * 128:jj * SUB + (hh + 1) * 128] = y.astype(o_ref.dtype)
                if g is not None:
                    ref = (s1_ref, s2_ref)[g - 1]
                    dil = DIL_GROUPS[g][1]
                    nu = ROW_CHUNK // dil
                    slot = (jj % 2) * n_heads + hh
                    col0 = _STREAM_TILES[g].index(jj) * SUB + hh * 128
                    epi_scr[slot, rows, :] = y
                    for r in range(dil):
                        rows_r = epi_scr[slot, pl.ds(rows.start + r, nu, stride=dil), :]
                        ref[r, c * nu:(c + 1) * nu, col0:col0 + 128] = rows_r.astype(ref.dtype)


def _qkv_proj(h, w_in, li, tabs, gq, gk, batch, seq, tm=512):
    t = h.shape[0]
    nseq = seq // tm
    d1, d2 = DIL_GROUPS[1][1], DIL_GROUPS[2][1]

    def stream_spec(dil):
        return pl.BlockSpec((None, dil, tm // dil, 3 * SUB), lambda i: (i // nseq, 0, i % nseq, 0))

    return pl.pallas_call(
        functools.partial(_qkv_kernel, tm=tm),
        grid=(t // tm,),
        in_specs=[pl.BlockSpec((tm, D_MODEL), lambda i: (i, 0)),
                  pl.BlockSpec((None, D_MODEL, QKV_COLS), lambda i: (li, 0, 0),
                               pipeline_mode=pl.Buffered(1)),
                  pl.BlockSpec((8, tm, 128), lambda i: (0, i % nseq, 0)),
                  pl.BlockSpec((1, 128), lambda i: (0, 0)),
                  pl.BlockSpec((1, 128), lambda i: (0, 0))],
        out_specs=[pl.BlockSpec((tm, QKV_COLS), lambda i: (i, 0)), stream_spec(d1), stream_spec(d2)],
        out_shape=[jax.ShapeDtypeStruct((t, QKV_COLS), BF16),
                   jax.ShapeDtypeStruct((batch, d1, seq // d1, 3 * SUB), BF16),
                   jax.ShapeDtypeStruct((batch, d2, seq // d2, 3 * SUB), BF16)],
        scratch_shapes=[pltpu.VMEM((2 * (SUB // 128), tm, 128), F32)],
        compiler_params=_cparams(("parallel",)),
        name="qkv_proj",
    )(h, w_in, tabs, gq.reshape(1, 128), gk.reshape(1, 128))


A_TQ = 128
A_TK = A_TQ + 2 * A_HALF


def _attn_a_kernel(q_ref, k_ref, v_ref, o_ref, lse_ref, *, length):
    nt = q_ref.shape[0] // A_TQ
    tiles_per_stream = length // A_TQ
    rows = lax.broadcasted_iota(jnp.int32, (A_TQ, A_TK), 0)
    cols = lax.broadcasted_iota(jnp.int32, (A_TQ, A_TK), 1)

    def body(i, carry):
        q0 = pl.multiple_of(i * A_TQ, A_TQ)
        stream0 = (i // tiles_per_stream) * length
        k0 = pl.multiple_of(jnp.clip(q0 - A_HALF, stream0, stream0 + length - A_TK), 16)
        valid = jnp.abs(rows - cols + (q0 - k0)) <= A_HALF
        for hh in range(2):
            sl = slice(hh * 128, (hh + 1) * 128)
            q = q_ref[pl.ds(q0, A_TQ), sl]
            k = k_ref[pl.ds(k0, A_TK), sl]
            v = v_ref[pl.ds(k0, A_TK), sl]
            s = lax.dot_general(q, k, (((1,), (1,)), ((), ())), preferred_element_type=F32)
            s = jnp.where(valid, s, NEG_INF)
            m = jnp.max(s, axis=1, keepdims=True)
            e = jnp.exp2(s - m)
            den = jnp.sum(e, axis=1, keepdims=True)
            o = jnp.dot(e.astype(BF16), v, preferred_element_type=F32) / den
            o_ref[pl.ds(q0, A_TQ), sl] = o.astype(o_ref.dtype)
            lse_ref[pl.ds(q0, A_TQ), sl] = jnp.broadcast_to(m + jnp.log2(den), (A_TQ, 128))
        return carry

    lax.fori_loop(0, nt, body, 0, unroll=min(nt, 4))


def _attn_a_group(src, gi, batch, seq):
    dil = DIL_GROUPS[gi][1]
    length = seq // dil
    if dil == 1:
        cols = (OFF_AQ // SUB + gi, OFF_AK // SUB + gi, OFF_AV // SUB + gi)
    else:
        src = src.reshape(batch * seq, 3 * SUB)
        cols = (0, 1, 2)
    out_spec = pl.BlockSpec((seq, SUB), lambda b: (b, 0))
    o, lse = pl.pallas_call(
        functools.partial(_attn_a_kernel, length=length),
        grid=(batch,),
        in_specs=[pl.BlockSpec((seq, SUB), functools.partial(lambda c, b: (b, c), c)) for c in cols],
        out_specs=[out_spec, out_spec],
        out_shape=[jax.ShapeDtypeStruct((batch * seq, SUB), BF16),
                   jax.ShapeDtypeStruct((batch * seq, SUB), F32)],
        compiler_params=_cparams(("parallel",)),
        name=f"attn_a{gi}",
    )(src, src, src)
    if dil == 1:
        return o, lse
    return o.reshape(batch, dil, length, SUB), lse.reshape(batch, dil, length, SUB)


FLASH_BLOCK = 256


def _transpose_values(v_ref, vt_scr, tk):
    for t in range(vt_scr.shape[0]):
        vt_scr[t] = v_ref[t * tk:(t + 1) * tk, :].astype(F32).T.astype(vt_scr.dtype)


N_SCORE_BUFS = 3


def _flash_scratch(seq, tk, rows):
    return ([pltpu.VMEM((seq // tk, HEAD_DIM, tk), BF16),
             pltpu.VMEM((N_SCORE_BUFS, tk, rows), F32),
             pltpu.VMEM((2, tk, rows), BF16),
             pltpu.VMEM((HEAD_DIM, rows), F32)])


def _flash_keymajor(q, k_ref, vt_scr, s_scr, p_scr, acc_scr, tk):
    rows = q.shape[0]
    nk = vt_scr.shape[0]
    q_blocks = [slice(c, c + FLASH_BLOCK) for c in range(0, rows, FLASH_BLOCK)]
    k_chunks = [slice(c, c + FLASH_BLOCK) for c in range(0, tk, FLASH_BLOCK)]
    bufs = [s_scr.at[b] for b in range(N_SCORE_BUFS)]

    def scores(t, qb, dst):
        k = k_ref[t * tk:(t + 1) * tk, :]
        dst[:, qb] = lax.dot_general(k, q[qb, :], (((1,), (1,)), ((), ())), preferred_element_type=F32)

    def update(m, l, src, t, qb):
        m_new = m
        for kc in k_chunks:
            m_new = jnp.maximum(m_new, jnp.max(src[kc, qb], axis=0, keepdims=True))
        alpha = jnp.exp2(m - m_new)
        l_tile = None
        p_buf = p_scr.at[t % 2]
        for kc in k_chunks:
            p = jnp.exp2(src[kc, qb] - m_new)
            p_sum = jnp.sum(p, axis=0, keepdims=True)
            l_tile = p_sum if l_tile is None else l_tile + p_sum
            p_buf[kc, qb] = p.astype(BF16)
        pv = jnp.dot(vt_scr[t], p_buf[:, qb], preferred_element_type=F32)
        acc_scr[:, qb] = pv if t == 0 else alpha * acc_scr[:, qb] + pv
        return m_new, alpha * l + l_tile

    m = [jnp.full((1, FLASH_BLOCK), NEG_INF, F32) for _ in q_blocks]
    l = [jnp.zeros((1, FLASH_BLOCK), F32) for _ in q_blocks]
    for qb in q_blocks:
        scores(0, qb, bufs[0])
    for t in range(nk):
        for bi, qb in enumerate(q_blocks):
            if t + 1 < nk:
                scores(t + 1, qb, bufs[(t + 1) % N_SCORE_BUFS])
            m[bi], l[bi] = update(m[bi], l[bi], bufs[t % N_SCORE_BUFS], t, qb)
    return acc_scr[...], jnp.concatenate(l, axis=1)


def _attn_b_kernel(q_ref, k_ref, v_ref, o_ref, vt_scr, s_scr, p_scr, acc_scr, *, tq, tk):
    @pl.when(pl.program_id(2) == 0)
    def _():
        _transpose_values(v_ref, vt_scr, tk)

    q = jnp.concatenate([q_ref[:, h * 128:(h + 1) * 128] for h in range(B_GROUP)], axis=0)
    acc, l = _flash_keymajor(q, k_ref, vt_scr, s_scr, p_scr, acc_scr, tk)
    o = acc * (1.0 / l)
    for h in range(B_GROUP):
        o_ref[:, h * 128:(h + 1) * 128] = o[:, h * tq:(h + 1) * tq].T.astype(o_ref.dtype)


def _attn_b(qkv, batch, seq, tq=512, tk=1024):
    nq = seq // tq
    gw = B_GROUP * 128
    rows = B_GROUP * tq
    return pl.pallas_call(
        functools.partial(_attn_b_kernel, tq=tq, tk=tk),
        grid=(batch, B_KV_HEADS, nq),
        in_specs=[pl.BlockSpec((tq, gw), lambda b, kh, i: (b * nq + i, OFF_BQ // gw + kh)),
                  pl.BlockSpec((seq, 128), lambda b, kh, i: (b, OFF_BK // 128 + kh)),
                  pl.BlockSpec((seq, 128), lambda b, kh, i: (b, OFF_BV // 128 + kh))],
        out_specs=pl.BlockSpec((tq, gw), lambda b, kh, i: (b * nq + i, kh)),
        out_shape=jax.ShapeDtypeStruct((batch * seq, B_QW), BF16),
        scratch_shapes=_flash_scratch(seq, tk, rows),
        compiler_params=_cparams(("parallel", "parallel", "arbitrary")),
        name="attn_b",
    )(qkv, qkv, qkv)


def _attn_c_kernel(lam_ref, g_ref, q_ref, k_ref, v_ref, o_ref, vt_scr, s_scr, p_scr, acc_scr,
                   *, tq, tk, lam_init):
    @pl.when(pl.program_id(2) == 0)
    def _():
        _transpose_values(v_ref, vt_scr, tk)

    lamv = lam_ref[...]
    lam = (jnp.exp(jnp.sum(lamv[0:1] * lamv[1:2], axis=1, keepdims=True))
           - jnp.exp(jnp.sum(lamv[2:3] * lamv[3:4], axis=1, keepdims=True)) + lam_init)
    q = q_ref[...]
    lane = lax.broadcasted_iota(jnp.int32, q.shape, 1)
    zero = jnp.zeros_like(q)
    q2 = jnp.concatenate([jnp.where(lane < C_QK_DIM, q, zero), jnp.where(lane >= C_QK_DIM, q, zero)], axis=0)
    acc, l = _flash_keymajor(q2, k_ref, vt_scr, s_scr, p_scr, acc_scr, tk)
    o = acc * (1.0 / l)
    oc = (o[:, :tq] - lam * o[:, tq:]).T
    o_ref[...] = (_rms_rows(oc, g_ref[...]) * (1.0 - lam_init)).astype(o_ref.dtype)


def _attn_c(qkv, c_lambda, c_head_norm, lam_init, batch, seq, tq=512, tk=1024):
    nq = seq // tq
    rows = 2 * tq
    return pl.pallas_call(
        functools.partial(_attn_c_kernel, tq=tq, tk=tk, lam_init=lam_init),
        grid=(batch, C_HEADS, nq),
        in_specs=[pl.BlockSpec((4, C_QK_DIM), lambda b, h, i: (0, 0)),
                  pl.BlockSpec((1, 128), lambda b, h, i: (0, 0)),
                  pl.BlockSpec((tq, 128), lambda b, h, i: (b * nq + i, OFF_CQ // 128 + h)),
                  pl.BlockSpec((seq, 128), lambda b, h, i: (b, OFF_CK // 128 + h)),
                  pl.BlockSpec((seq, 128), lambda b, h, i: (b, OFF_CV // 128 + h))],
        out_specs=pl.BlockSpec((tq, 128), lambda b, h, i: (b * nq + i, h)),
        out_shape=jax.ShapeDtypeStruct((batch * seq, C_VW), BF16),
        scratch_shapes=_flash_scratch(seq, tk, rows),
        compiler_params=_cparams(("parallel", "parallel", "arbitrary")),
        name="attn_c",
    )(c_lambda, c_head_norm.reshape(1, 128), qkv, qkv, qkv)


def _merge_kernel(h_ref, wg0, wg1, wg2, oa0, oa1, oa2, ls0, ls1, ls2, ob, oc, wa, wb, wc, out_ref,
                  oa_scr, o1_scr, o2_scr, l1_scr, l2_scr, *, tm):
    @pl.when(pl.program_id(1) == 0)
    def _():
        for g, o_src, l_src, o_dst, l_dst in ((1, oa1, ls1, o1_scr, l1_scr), (2, oa2, ls2, o2_scr, l2_scr)):
            dil = DIL_GROUPS[g][1]
            for r in range(dil):
                for hh in range(SUB // 128):
                    sl = slice(hh * 128, (hh + 1) * 128)
                    o_dst[hh, pl.ds(r, tm // dil, stride=dil), :] = o_src[r, :, sl].astype(F32)
                    l_dst[hh, pl.ds(r, tm // dil, stride=dil), :] = l_src[r, :, sl]
        for hh in range(SUB // 128):
            sl = slice(hh * 128, (hh + 1) * 128)
            l0, l1, l2 = ls0[:, sl], l1_scr[hh], l2_scr[hh]
            m = jnp.maximum(jnp.maximum(l0, l1), l2)
            e0, e1, e2 = jnp.exp2(l0 - m), jnp.exp2(l1 - m), jnp.exp2(l2 - m)
            num = e0 * oa0[:, sl].astype(F32) + e1 * o1_scr[hh] + e2 * o2_scr[hh]
            oa_scr[:, sl] = (num / (e0 + e1 + e2)).astype(oa_scr.dtype)

    for rows in _row_chunks(tm):
        hr = h_ref[rows, :]
        merged = None
        for wg, o_br, w_br in ((wg0, oa_scr, wa), (wg1, ob, wb), (wg2, oc, wc)):
            gate = jax.nn.sigmoid(jnp.dot(hr, wg[...], preferred_element_type=F32))
            term = gate * jnp.dot(o_br[rows, :], w_br[...], preferred_element_type=F32)
            merged = term if merged is None else merged + term
        out_ref[rows, :] = merged.astype(out_ref.dtype)


def _merge(h, w_in, li, oa, lse, ob, oc, wa, wb, wc, seq, tm=1024, tn=512):
    t = ob.shape[0]
    nj = D_MODEL // tn
    nseq = seq // tm

    def row(w):
        return pl.BlockSpec((tm, w), lambda i, j: (i, 0))

    def stream(g):
        dil = DIL_GROUPS[g][1]
        return pl.BlockSpec((None, dil, tm // dil, SUB), lambda i, j: (i // nseq, 0, i % nseq, 0))

    def gate_w(br):
        return pl.BlockSpec((None, D_MODEL, tn),
                            lambda i, j: (li, 0, (QKV_COLS + br * D_MODEL) // tn + j))

    def wspec(k):
        return pl.BlockSpec((None, k, tn), lambda i, j: (li, 0, j))

    return pl.pallas_call(
        functools.partial(_merge_kernel, tm=tm),
        grid=(t // tm, nj),
        in_specs=[row(D_MODEL), gate_w(0), gate_w(1), gate_w(2),
                  row(SUB), stream(1), stream(2), row(SUB), stream(1), stream(2),
                  row(B_QW), row(C_VW), wspec(SUB), wspec(B_QW), wspec(C_VW)],
        out_specs=pl.BlockSpec((tm, tn), lambda i, j: (i, j)),
        out_shape=jax.ShapeDtypeStruct((t, D_MODEL), BF16),
        scratch_shapes=[pltpu.VMEM((tm, SUB), BF16)] + [pltpu.VMEM((SUB // 128, tm, 128), F32)] * 4,
        compiler_params=_cparams(("parallel", "arbitrary")),
        name="merge",
    )(h, w_in, w_in, w_in, *oa, *lse, ob, oc, wa, wb, wc)


def _out_kernel(m_ref, w_ref, x_ref, gpost_ref, gnext_ref, x1_ref, h_ref):
    for rows in _row_chunks(m_ref.shape[0]):
        y = jnp.dot(m_ref[rows, :], w_ref[...], preferred_element_type=F32)
        x1 = x_ref[rows, :] + _rms_rows(y, gpost_ref[...])
        x1_ref[rows, :] = x1
        h_ref[rows, :] = _rms_rows(x1, gnext_ref[...]).astype(h_ref.dtype)


def _out_proj(merged, w_out, li, x, g_post, g_next, tm=512):
    t = x.shape[0]
    row = lambda i: (i, 0)
    const = lambda i: (0, 0)
    return pl.pallas_call(
        _out_kernel,
        grid=(t // tm,),
        in_specs=[pl.BlockSpec((tm, D_MODEL), row),
                  pl.BlockSpec((None, D_MODEL, D_MODEL), lambda i: (li, 0, 0)),
                  pl.BlockSpec((tm, D_MODEL), row),
                  pl.BlockSpec((1, D_MODEL), const),
                  pl.BlockSpec((1, D_MODEL), const)],
        out_specs=[pl.BlockSpec((tm, D_MODEL), row), pl.BlockSpec((tm, D_MODEL), row)],
        out_shape=[jax.ShapeDtypeStruct((t, D_MODEL), F32), jax.ShapeDtypeStruct((t, D_MODEL), BF16)],
        compiler_params=_cparams(("parallel",)),
        name="out_proj",
    )(merged, w_out, x, g_post.reshape(1, D_MODEL), g_next.reshape(1, D_MODEL))


def _mlp_kernel(h_ref, w1_ref, w2_ref, x_ref, gpost_ref, *rest, with_next):
    if with_next:
        gnext_ref, x2_ref, hn_ref, acc_ref = rest
    else:
        x2_ref, acc_ref = rest
    f = pl.program_id(1)

    @pl.when(f == 0)
    def _():
        acc_ref[...] = jnp.zeros_like(acc_ref)

    def partial_out(rows):
        u = jnp.dot(h_ref[rows, :], w1_ref[...], preferred_element_type=F32)
        u = jnp.square(jnp.maximum(u, 0.0)).astype(BF16)
        return acc_ref[rows, :] + jnp.dot(u, w2_ref[...], preferred_element_type=F32)

    last = pl.num_programs(1) - 1

    @pl.when(f < last)
    def _():
        acc_ref[...] = partial_out(slice(None))

    @pl.when(f == last)
    def _():
        for rows in _row_chunks(h_ref.shape[0]):
            x2 = x_ref[rows, :] + _rms_rows(partial_out(rows), gpost_ref[...])
            x2_ref[rows, :] = x2
            if with_next:
                hn_ref[rows, :] = _rms_rows(x2, gnext_ref[...]).astype(hn_ref.dtype)


def _mlp(h, w1, w2, li, x, g_post, g_next, tm=512, tf=1024):
    t = x.shape[0]
    with_next = g_next is not None
    row = lambda i, f: (i, 0)
    const = lambda i, f: (0, 0)
    in_specs = [pl.BlockSpec((tm, D_MODEL), row),
                pl.BlockSpec((None, D_MODEL, tf), lambda i, f: (li, 0, f)),
                pl.BlockSpec((None, tf, D_MODEL), lambda i, f: (li, f, 0)),
                pl.BlockSpec((tm, D_MODEL), row),
                pl.BlockSpec((1, D_MODEL), const)]
    args = [h, w1, w2, x, g_post.reshape(1, D_MODEL)]
    out_specs = [pl.BlockSpec((tm, D_MODEL), row)]
    out_shape = [jax.ShapeDtypeStruct((t, D_MODEL), F32)]
    if with_next:
        in_specs.append(pl.BlockSpec((1, D_MODEL), const))
        args.append(g_next.reshape(1, D_MODEL))
        out_specs.append(pl.BlockSpec((tm, D_MODEL), row))
        out_shape.append(jax.ShapeDtypeStruct((t, D_MODEL), BF16))
    res = pl.pallas_call(
        functools.partial(_mlp_kernel, with_next=with_next),
        grid=(t // tm, D_FF // tf),
        in_specs=in_specs,
        out_specs=out_specs,
        out_shape=out_shape,
        scratch_shapes=[pltpu.VMEM((tm, D_MODEL), F32)],
        compiler_params=_cparams(("parallel", "arbitrary")),
        name="mlp",
    )(*args)
    return (res[0], res[1]) if with_next else (res[0], None)


def _rope_tables(pos, dim):
    inv = 1.0 / (ROPE_THETA ** (jnp.arange(0, dim, 2, dtype=F32) / dim))
    ang = pos.astype(F32)[:, None] * inv[None, :]
    return jnp.cos(ang), jnp.sin(ang)


def _tables(seq):
    pos = jnp.arange(seq)
    ca, sa = _rope_tables(pos, HEAD_DIM)
    cos_a = jnp.concatenate([ca, ca], axis=1)
    sin_a = jnp.concatenate([-sa, sa], axis=1)
    z32 = jnp.zeros((seq, 32), F32)

    def half32(c, s):
        return jnp.concatenate([c, c], 1), jnp.concatenate([-s, z32], 1), jnp.concatenate([z32, s], 1)

    cr, sr = _rope_tables(pos // GRID_W, AXIAL_DIM)
    cc, sc = _rope_tables(pos % GRID_W, AXIAL_DIM)
    r_c, r_lo, r_hi = half32(cr, sr)
    c_c, c_lo, c_hi = half32(cc, sc)
    cos_b = jnp.concatenate([r_c, c_c], 1)
    sin_b_lo = jnp.concatenate([r_lo, c_lo], 1)
    sin_b_hi = jnp.concatenate([r_hi, c_hi], 1)
    c3, s3 = _rope_tables(pos, C_QK_DIM)
    m_c, m_lo, m_hi = half32(c3, s3)
    cos_c = jnp.concatenate([m_c, m_c], 1)
    sin_c_lo = jnp.concatenate([m_lo, m_lo], 1)
    sin_c_hi = jnp.concatenate([m_hi, m_hi], 1)
    return jnp.stack([cos_a, sin_a, cos_b, sin_b_lo, sin_b_hi, cos_c, sin_c_lo, sin_c_hi], axis=0)


def _trunk(x3, tabs, p):
    batch, seq, _ = x3.shape
    x = x3.reshape(batch * seq, D_MODEL)
    h = _prenorm(x, p["g_mix_pre"][0])
    for li in range(DEPTH):
        lam_init = 0.8 - 0.6 * math.exp(-0.3 * li)
        qkv, st1, st2 = _qkv_proj(h, p["w_in"], li, tabs, p["b_q_norm"][li], p["b_k_norm"][li],
                                  batch, seq)
        oa, lse = zip(*[_attn_a_group(src, gi, batch, seq) for gi, src in enumerate((qkv, st1, st2))])
        ob = _attn_b(qkv, batch, seq)
        oc = _attn_c(qkv, p["c_lambda"][li], p["c_head_norm"][li], lam_init, batch, seq)
        merged = _merge(h, p["w_in"], li, oa, lse, ob, oc,
                        p["w_branch_a"], p["w_branch_b"], p["w_branch_c"], seq)
        x, hm = _out_proj(merged, p["w_out"], li, x, p["g_mix_post"][li], p["g_mlp_pre"][li])
        g_next = p["g_mix_pre"][li + 1] if li + 1 < DEPTH else None
        x, h = _mlp(hm, p["w_mlp_in"], p["w_mlp_out"], li, x, p["g_mlp_post"][li], g_next)
    return x.reshape(batch, seq, D_MODEL)


def kernel(x_prompt, x_sample, g_mix_pre, w_in, b_q_norm, b_k_norm, c_lambda, c_head_norm,
           w_branch_a, w_branch_b, w_branch_c, w_out, g_mix_post,
           g_mlp_pre, w_mlp_in, w_mlp_out, g_mlp_post):
    p = dict(g_mix_pre=g_mix_pre, w_in=w_in.astype(BF16), b_q_norm=b_q_norm, b_k_norm=b_k_norm,
             c_lambda=c_lambda, c_head_norm=c_head_norm,
             w_branch_a=w_branch_a.astype(BF16), w_branch_b=w_branch_b.astype(BF16),
             w_branch_c=w_branch_c.astype(BF16), w_out=w_out.astype(BF16), g_mix_post=g_mix_post,
             g_mlp_pre=g_mlp_pre, w_mlp_in=w_mlp_in.astype(BF16), w_mlp_out=w_mlp_out.astype(BF16),
             g_mlp_post=g_mlp_post)
    tabs = _tables(max(x_prompt.shape[1], x_sample.shape[1]))
    return (_trunk(x_prompt, tabs, p), _trunk(x_sample, tabs, p))
```

```python
import functools
import math

import jax
import jax.numpy as jnp
from jax import lax
from jax.experimental import pallas as pl
from jax.experimental.pallas import tpu as pltpu

F32 = jnp.float32
BF16 = jnp.bfloat16

D_MODEL = 2048
DEPTH = 2
HEAD_DIM = 128
GRID_W = 64
ROPE_THETA = 10000.0
EPS = 1e-6
NEG_INF = -1e30
LOG2E = math.log2(math.e)

DIL_GROUPS = ((128, 1), (512, 4), (2048, 16))
A_HALF = 64
A_W = 768
B_Q_HEADS = 6
B_KV_HEADS = 2
B_GROUP = B_Q_HEADS // B_KV_HEADS
B_QW = 768
B_KVW = 256
AXIAL_DIM = 64
C_HEADS = 4
C_QK_DIM = 64
C_QW = 512
C_VW = 512
N_BRANCH = 3
QKV_COLS = 3 * A_W + B_QW + 2 * B_KVW + 2 * C_QW + C_VW
GATE_COLS = N_BRANCH * D_MODEL
D_FF = 4 * D_MODEL

OFF_AQ, OFF_AK, OFF_AV = 0, 768, 1536
OFF_BQ, OFF_BK, OFF_BV = 2304, 3072, 3328
OFF_CQ, OFF_CK, OFF_CV = 3584, 4096, 4608

SUB = 256
_QKV_KINDS = (("aq",) * 3 + ("ak",) * 3 + ("v",) * 3 + ("bq",) * 3 + ("bk",) + ("v",)
              + ("cq",) * 2 + ("ck",) * 2 + ("v",) * 2)

VMEM_LIMIT = 56 * 1024 * 1024


def _cparams(sem):
    return pltpu.CompilerParams(dimension_semantics=sem, vmem_limit_bytes=VMEM_LIMIT)


def _rms_rows(xf, g):
    ms = jnp.mean(xf * xf, axis=-1, keepdims=True)
    return xf * lax.rsqrt(ms + EPS) * g


def _rope_half64(x, cos, sin_signed):
    return x * cos + pltpu.roll(x, 64, axis=1) * sin_signed


def _rope_half32(x, cos, sin_lo, sin_hi):
    return x * cos + pltpu.roll(x, 96, axis=1) * sin_lo + pltpu.roll(x, 32, axis=1) * sin_hi


_STREAM_TILES = {1: (1, 4, 7), 2: (2, 5, 8)}


ROW_CHUNK = 256


def _row_chunks(n):
    return [slice(r, r + ROW_CHUNK) for r in range(0, n, ROW_CHUNK)]


def _qkv_kernel(*refs, tm, norm_input):
    if norm_input:
        x_ref, gpre_ref, w_ref, tab_ref, gq_ref, gk_ref, o_ref, s1_ref, s2_ref, h_ref, epi_scr = refs
        for rows in _row_chunks(tm):
            h_ref[rows, :] = _rms_rows(x_ref[rows, :], gpre_ref[...]).astype(h_ref.dtype)
    else:
        h_ref, w_ref, tab_ref, gq_ref, gk_ref, o_ref, s1_ref, s2_ref, epi_scr = refs

    def head_epilogue(x, kind, rows):
        if kind in ("aq", "ak"):
            y = _rope_half64(x, tab_ref[0, rows, :], tab_ref[1, rows, :])
            return y * (HEAD_DIM ** -0.5 * LOG2E) if kind == "aq" else y
        if kind in ("bq", "bk"):
            g = gq_ref[...] * (HEAD_DIM ** -0.5 * LOG2E) if kind == "bq" else gk_ref[...]
            return _rope_half32(_rms_rows(x, g), tab_ref[2, rows, :], tab_ref[3, rows, :], tab_ref[4, rows, :])
        if kind in ("cq", "ck"):
            y = _rope_half32(x, tab_ref[5, rows, :], tab_ref[6, rows, :], tab_ref[7, rows, :])
            return y * (C_QK_DIM ** -0.5 * LOG2E) if kind == "cq" else y
        return x

    n_heads = SUB // 128
    for jj, kind in enumerate(_QKV_KINDS):
        g = next((g for g in (1, 2) if jj in _STREAM_TILES[g]), None)
        for c, rows in enumerate(_row_chunks(tm)):
            acc = jnp.dot(h_ref[rows, :], w_ref[:, jj * SUB:(jj + 1) * SUB], preferred_element_type=F32)
            for hh in range(n_heads):
                sl = slice(hh * 128, (hh + 1) * 128)
                y = head_epilogue(acc[:, sl], kind, rows)
                o_ref[rows, jj * SUB + hh * 128:jj * SUB + (hh + 1) * 128] = y.astype(o_ref.dtype)
                if g is not None:
                    ref = (s1_ref, s2_ref)[g - 1]
                    dil = DIL_GROUPS[g][1]
                    nu = ROW_CHUNK // dil
                    slot = (jj % 2) * n_heads + hh
                    col0 = _STREAM_TILES[g].index(jj) * SUB + hh * 128
                    epi_scr[slot, rows, :] = y
                    for r in range(dil):
                        rows_r = epi_scr[slot, pl.ds(rows.start + r, nu, stride=dil), :]
                        ref[r, c * nu:(c + 1) * nu, col0:col0 + 128] = rows_r.astype(ref.dtype)


def _qkv_proj(h, w_in, li, tabs, gq, gk, batch, seq, g_pre=None, tm=512):
    t = h.shape[0]
    nseq = seq // tm
    d1, d2 = DIL_GROUPS[1][1], DIL_GROUPS[2][1]
    norm_input = g_pre is not None
    row = pl.BlockSpec((tm, D_MODEL), lambda i: (i, 0))
    vec = lambda n: pl.BlockSpec((1, n), lambda i: (0, 0))

    def stream_spec(dil):
        return pl.BlockSpec((None, dil, tm // dil, 3 * SUB), lambda i: (i // nseq, 0, i % nseq, 0))

    in_specs = [pl.BlockSpec((None, D_MODEL, QKV_COLS), lambda i: (li, 0, 0), pipeline_mode=pl.Buffered(1)),
                pl.BlockSpec((8, tm, 128), lambda i: (0, i % nseq, 0)), vec(128), vec(128)]
    args = [w_in, tabs, gq.reshape(1, 128), gk.reshape(1, 128)]
    out_specs = [pl.BlockSpec((tm, QKV_COLS), lambda i: (i, 0)), stream_spec(d1), stream_spec(d2)]
    out_shape = [jax.ShapeDtypeStruct((t, QKV_COLS), BF16),
                 jax.ShapeDtypeStruct((batch, d1, seq // d1, 3 * SUB), BF16),
                 jax.ShapeDtypeStruct((batch, d2, seq // d2, 3 * SUB), BF16)]
    if norm_input:
        in_specs = [row, vec(D_MODEL)] + in_specs
        args = [h, g_pre.reshape(1, D_MODEL)] + args
        out_specs.append(row)
        out_shape.append(jax.ShapeDtypeStruct((t, D_MODEL), BF16))
    else:
        in_specs = [row] + in_specs
        args = [h] + args
    return pl.pallas_call(
        functools.partial(_qkv_kernel, tm=tm, norm_input=norm_input),
        grid=(t // tm,),
        in_specs=in_specs,
        out_specs=out_specs,
        out_shape=out_shape,
        scratch_shapes=[pltpu.VMEM((2 * (SUB // 128), tm, 128), F32)],
        compiler_params=_cparams(("parallel",)),
        name="qkv_proj",
    )(*args)


A_TQ = 128
A_TK = A_TQ + 2 * A_HALF


def _attn_a_kernel(q_ref, k_ref, v_ref, o_ref, lse_ref, *, length):
    nt = q_ref.shape[0] // A_TQ
    tiles_per_stream = length // A_TQ
    rows = lax.broadcasted_iota(jnp.int32, (A_TQ, A_TK), 0)
    cols = lax.broadcasted_iota(jnp.int32, (A_TQ, A_TK), 1)

    def body(i, carry):
        q0 = pl.multiple_of(i * A_TQ, A_TQ)
        stream0 = (i // tiles_per_stream) * length
        k0 = pl.multiple_of(jnp.clip(q0 - A_HALF, stream0, stream0 + length - A_TK), 16)
        valid = jnp.abs(rows - cols + (q0 - k0)) <= A_HALF
        for hh in range(2):
            sl = slice(hh * 128, (hh + 1) * 128)
            q = q_ref[pl.ds(q0, A_TQ), sl]
            k = k_ref[pl.ds(k0, A_TK), sl]
            v = v_ref[pl.ds(k0, A_TK), sl]
            s = lax.dot_general(q, k, (((1,), (1,)), ((), ())), preferred_element_type=F32)
            s = jnp.where(valid, s, NEG_INF)
            m = jnp.max(s, axis=1, keepdims=True)
            e = jnp.exp2(s - m)
            den = jnp.sum(e, axis=1, keepdims=True)
            o = jnp.dot(e.astype(BF16), v, preferred_element_type=F32) / den
            o_ref[pl.ds(q0, A_TQ), sl] = o.astype(o_ref.dtype)
            lse_ref[pl.ds(q0, A_TQ), sl] = jnp.broadcast_to(m + jnp.log2(den), (A_TQ, 128))
        return carry

    lax.fori_loop(0, nt, body, 0, unroll=min(nt, 4))


def _attn_a_group(src, gi, batch, seq):
    dil = DIL_GROUPS[gi][1]
    length = seq // dil
    if dil == 1:
        cols = (OFF_AQ // SUB + gi, OFF_AK // SUB + gi, OFF_AV // SUB + gi)
    else:
        src = src.reshape(batch * seq, 3 * SUB)
        cols = (0, 1, 2)
    out_spec = pl.BlockSpec((seq, SUB), lambda b: (b, 0))
    o, lse = pl.pallas_call(
        functools.partial(_attn_a_kernel, length=length),
        grid=(batch,),
        in_specs=[pl.BlockSpec((seq, SUB), functools.partial(lambda c, b: (b, c), c)) for c in cols],
        out_specs=[out_spec, out_spec],
        out_shape=[jax.ShapeDtypeStruct((batch * seq, SUB), BF16),
                   jax.ShapeDtypeStruct((batch * seq, SUB), F32)],
        compiler_params=_cparams(("parallel",)),
        name=f"attn_a{gi}",
    )(src, src, src)
    if dil == 1:
        return o, lse
    return o.reshape(batch, dil, length, SUB), lse.reshape(batch, dil, length, SUB)


FLASH_BLOCK = 256


def _transpose_values(v_ref, vt_scr, tk):
    for t in range(vt_scr.shape[0]):
        vt_scr[t] = v_ref[t * tk:(t + 1) * tk, :].astype(F32).T.astype(vt_scr.dtype)


N_SCORE_BUFS = 3


def _flash_scratch(seq, tk, rows):
    return ([pltpu.VMEM((seq // tk, HEAD_DIM, tk), BF16),
             pltpu.VMEM((N_SCORE_BUFS, tk, rows), F32),
             pltpu.VMEM((2, tk, rows), BF16),
             pltpu.VMEM((HEAD_DIM, rows), F32)])


def _flash_keymajor(q, k_ref, vt_scr, s_scr, p_scr, acc_scr, tk):
    rows = q.shape[0]
    nk = vt_scr.shape[0]
    q_blocks = [slice(c, c + FLASH_BLOCK) for c in range(0, rows, FLASH_BLOCK)]
    k_chunks = [slice(c, c + FLASH_BLOCK) for c in range(0, tk, FLASH_BLOCK)]
    bufs = [s_scr.at[b] for b in range(N_SCORE_BUFS)]

    def scores(t, qb, dst):
        k = k_ref[t * tk:(t + 1) * tk, :]
        dst[:, qb] = lax.dot_general(k, q[qb, :], (((1,), (1,)), ((), ())), preferred_element_type=F32)

    def update(m, l, src, t, qb):
        m_new = m
        for kc in k_chunks:
            m_new = jnp.maximum(m_new, jnp.max(src[kc, qb], axis=0, keepdims=True))
        alpha = jnp.exp2(m - m_new)
        l_tile = None
        p_buf = p_scr.at[t % 2]
        for kc in k_chunks:
            p = jnp.exp2(src[kc, qb] - m_new)
            p_sum = jnp.sum(p, axis=0, keepdims=True)
            l_tile = p_sum if l_tile is None else l_tile + p_sum
            p_buf[kc, qb] = p.astype(BF16)
        pv = jnp.dot(vt_scr[t], p_buf[:, qb], preferred_element_type=F32)
        acc_scr[:, qb] = pv if t == 0 else alpha * acc_scr[:, qb] + pv
        return m_new, alpha * l + l_tile

    m = [jnp.full((1, FLASH_BLOCK), NEG_INF, F32) for _ in q_blocks]
    l = [jnp.zeros((1, FLASH_BLOCK), F32) for _ in q_blocks]
    for qb in q_blocks:
        scores(0, qb, bufs[0])
    for t in range(nk):
        for bi, qb in enumerate(q_blocks):
            if t + 1 < nk:
                scores(t + 1, qb, bufs[(t + 1) % N_SCORE_BUFS])
            m[bi], l[bi] = update(m[bi], l[bi], bufs[t % N_SCORE_BUFS], t, qb)
    return acc_scr[...], jnp.concatenate(l, axis=1)


def _attn_b_kernel(q_ref, k_ref, v_ref, o_ref, vt_scr, s_scr, p_scr, acc_scr, *, tq, tk):
    @pl.when(pl.program_id(2) == 0)
    def _():
        _transpose_values(v_ref, vt_scr, tk)

    q = jnp.concatenate([q_ref[:, h * 128:(h + 1) * 128] for h in range(B_GROUP)], axis=0)
    acc, l = _flash_keymajor(q, k_ref, vt_scr, s_scr, p_scr, acc_scr, tk)
    o = acc * (1.0 / l)
    for h in range(B_GROUP):
        o_ref[:, h * 128:(h + 1) * 128] = o[:, h * tq:(h + 1) * tq].T.astype(o_ref.dtype)


def _attn_b(qkv, batch, seq, tq=512, tk=1024):
    nq = seq // tq
    gw = B_GROUP * 128
    rows = B_GROUP * tq
    return pl.pallas_call(
        functools.partial(_attn_b_kernel, tq=tq, tk=tk),
        grid=(batch, B_KV_HEADS, nq),
        in_specs=[pl.BlockSpec((tq, gw), lambda b, kh, i: (b * nq + i, OFF_BQ // gw + kh)),
                  pl.BlockSpec((seq, 128), lambda b, kh, i: (b, OFF_BK // 128 + kh)),
                  pl.BlockSpec((seq, 128), lambda b, kh, i: (b, OFF_BV // 128 + kh))],
        out_specs=pl.BlockSpec((tq, gw), lambda b, kh, i: (b * nq + i, kh)),
        out_shape=jax.ShapeDtypeStruct((batch * seq, B_QW), BF16),
        scratch_shapes=_flash_scratch(seq, tk, rows),
        compiler_params=_cparams(("parallel", "parallel", "arbitrary")),
        name="attn_b",
    )(qkv, qkv, qkv)


def _attn_c_kernel(lam_ref, g_ref, q_ref, k_ref, v_ref, o_ref, vt_scr, s_scr, p_scr, acc_scr,
                   *, tq, tk, lam_init):
    @pl.when(pl.program_id(2) == 0)
    def _():
        _transpose_values(v_ref, vt_scr, tk)

    lamv = lam_ref[...]
    lam = (jnp.exp(jnp.sum(lamv[0:1] * lamv[1:2], axis=1, keepdims=True))
           - jnp.exp(jnp.sum(lamv[2:3] * lamv[3:4], axis=1, keepdims=True)) + lam_init)
    q = q_ref[...]
    lane = lax.broadcasted_iota(jnp.int32, q.shape, 1)
    zero = jnp.zeros_like(q)
    q2 = jnp.concatenate([jnp.where(lane < C_QK_DIM, q, zero), jnp.where(lane >= C_QK_DIM, q, zero)], axis=0)
    acc, l = _flash_keymajor(q2, k_ref, vt_scr, s_scr, p_scr, acc_scr, tk)
    o = acc * (1.0 / l)
    oc = (o[:, :tq] - lam * o[:, tq:]).T
    o_ref[...] = (_rms_rows(oc, g_ref[...]) * (1.0 - lam_init)).astype(o_ref.dtype)


def _attn_c(qkv, c_lambda, c_head_norm, lam_init, batch, seq, tq=512, tk=1024):
    nq = seq // tq
    rows = 2 * tq
    return pl.pallas_call(
        functools.partial(_attn_c_kernel, tq=tq, tk=tk, lam_init=lam_init),
        grid=(batch, C_HEADS, nq),
        in_specs=[pl.BlockSpec((4, C_QK_DIM), lambda b, h, i: (0, 0)),
                  pl.BlockSpec((1, 128), lambda b, h, i: (0, 0)),
                  pl.BlockSpec((tq, 128), lambda b, h, i: (b * nq + i, OFF_CQ // 128 + h)),
                  pl.BlockSpec((seq, 128), lambda b, h, i: (b, OFF_CK // 128 + h)),
                  pl.BlockSpec((seq, 128), lambda b, h, i: (b, OFF_CV // 128 + h))],
        out_specs=pl.BlockSpec((tq, 128), lambda b, h, i: (b * nq + i, h)),
        out_shape=jax.ShapeDtypeStruct((batch * seq, C_VW), BF16),
        scratch_shapes=_flash_scratch(seq, tk, rows),
        compiler_params=_cparams(("parallel", "parallel", "arbitrary")),
        name="attn_c",
    )(c_lambda, c_head_norm.reshape(1, 128), qkv, qkv, qkv)


def _merge_kernel(h_ref, wg0, wg1, wg2, oa0, oa1, oa2, ls0, ls1, ls2, ob, oc, wa, wb, wc, out_ref,
                  oa_scr, o1_scr, o2_scr, l1_scr, l2_scr, *, tm):
    @pl.when(pl.program_id(1) == 0)
    def _():
        for g, o_src, l_src, o_dst, l_dst in ((1, oa1, ls1, o1_scr, l1_scr), (2, oa2, ls2, o2_scr, l2_scr)):
            dil = DIL_GROUPS[g][1]
            for r in range(dil):
                for hh in range(SUB // 128):
                    sl = slice(hh * 128, (hh + 1) * 128)
                    o_dst[hh, pl.ds(r, tm // dil, stride=dil), :] = o_src[r, :, sl].astype(F32)
                    l_dst[hh, pl.ds(r, tm // dil, stride=dil), :] = l_src[r, :, sl]
        for hh in range(SUB // 128):
            sl = slice(hh * 128, (hh + 1) * 128)
            l0, l1, l2 = ls0[:, sl], l1_scr[hh], l2_scr[hh]
            m = jnp.maximum(jnp.maximum(l0, l1), l2)
            e0, e1, e2 = jnp.exp2(l0 - m), jnp.exp2(l1 - m), jnp.exp2(l2 - m)
            num = e0 * oa0[:, sl].astype(F32) + e1 * o1_scr[hh] + e2 * o2_scr[hh]
            oa_scr[:, sl] = (num / (e0 + e1 + e2)).astype(oa_scr.dtype)

    for rows in _row_chunks(tm):
        hr = h_ref[rows, :]
        merged = None
        for wg, o_br, w_br in ((wg0, oa_scr, wa), (wg1, ob, wb), (wg2, oc, wc)):
            gate = jax.nn.sigmoid(jnp.dot(hr, wg[...], preferred_element_type=F32))
            term = gate * jnp.dot(o_br[rows, :], w_br[...], preferred_element_type=F32)
            merged = term if merged is None else merged + term
        out_ref[rows, :] = merged.astype(out_ref.dtype)


def _merge(h, w_in, li, oa, lse, ob, oc, wa, wb, wc, seq, tm=1024, tn=512):
    t = ob.shape[0]
    nj = D_MODEL // tn
    nseq = seq // tm

    def row(w):
        return pl.BlockSpec((tm, w), lambda i, j: (i, 0))

    def stream(g):
        dil = DIL_GROUPS[g][1]
        return pl.BlockSpec((None, dil, tm // dil, SUB), lambda i, j: (i // nseq, 0, i % nseq, 0))

    def gate_w(br):
        return pl.BlockSpec((None, D_MODEL, tn),
                            lambda i, j: (li, 0, (QKV_COLS + br * D_MODEL) // tn + j))

    def wspec(k):
        return pl.BlockSpec((None, k, tn), lambda i, j: (li, 0, j))

    return pl.pallas_call(
        functools.partial(_merge_kernel, tm=tm),
        grid=(t // tm, nj),
        in_specs=[row(D_MODEL), gate_w(0), gate_w(1), gate_w(2),
                  row(SUB), stream(1), stream(2), row(SUB), stream(1), stream(2),
                  row(B_QW), row(C_VW), wspec(SUB), wspec(B_QW), wspec(C_VW)],
        out_specs=pl.BlockSpec((tm, tn), lambda i, j: (i, j)),
        out_shape=jax.ShapeDtypeStruct((t, D_MODEL), BF16),
        scratch_shapes=[pltpu.VMEM((tm, SUB), BF16)] + [pltpu.VMEM((SUB // 128, tm, 128), F32)] * 4,
        compiler_params=_cparams(("parallel", "arbitrary")),
        name="merge",
    )(h, w_in, w_in, w_in, *oa, *lse, ob, oc, wa, wb, wc)


def _out_kernel(m_ref, w_ref, x_ref, gpost_ref, gnext_ref, x1_ref, h_ref):
    for rows in _row_chunks(m_ref.shape[0]):
        y = jnp.dot(m_ref[rows, :], w_ref[...], preferred_element_type=F32)
        x1 = x_ref[rows, :] + _rms_rows(y, gpost_ref[...])
        x1_ref[rows, :] = x1
        h_ref[rows, :] = _rms_rows(x1, gnext_ref[...]).astype(h_ref.dtype)


def _out_proj(merged, w_out, li, x, g_post, g_next, tm=512):
    t = x.shape[0]
    row = lambda i: (i, 0)
    const = lambda i: (0, 0)
    return pl.pallas_call(
        _out_kernel,
        grid=(t // tm,),
        in_specs=[pl.BlockSpec((tm, D_MODEL), row),
                  pl.BlockSpec((None, D_MODEL, D_MODEL), lambda i: (li, 0, 0)),
                  pl.BlockSpec((tm, D_MODEL), row),
                  pl.BlockSpec((1, D_MODEL), const),
                  pl.BlockSpec((1, D_MODEL), const)],
        out_specs=[pl.BlockSpec((tm, D_MODEL), row), pl.BlockSpec((tm, D_MODEL), row)],
        out_shape=[jax.ShapeDtypeStruct((t, D_MODEL), F32), jax.ShapeDtypeStruct((t, D_MODEL), BF16)],
        compiler_params=_cparams(("parallel",)),
        name="out_proj",
    )(merged, w_out, x, g_post.reshape(1, D_MODEL), g_next.reshape(1, D_MODEL))


def _mlp_kernel(h_ref, w1_ref, w2_ref, x_ref, gpost_ref, *rest, with_next):
    if with_next:
        gnext_ref, x2_ref, hn_ref, acc_ref = rest
    else:
        x2_ref, acc_ref = rest
    f = pl.program_id(1)

    @pl.when(f == 0)
    def _():
        acc_ref[...] = jnp.zeros_like(acc_ref)

    def partial_out(rows):
        u = jnp.dot(h_ref[rows, :], w1_ref[...], preferred_element_type=F32)
        u = jnp.square(jnp.maximum(u, 0.0)).astype(BF16)
        return acc_ref[rows, :] + jnp.dot(u, w2_ref[...], preferred_element_type=F32)

    last = pl.num_programs(1) - 1

    @pl.when(f < last)
    def _():
        acc_ref[...] = partial_out(slice(None))

    @pl.when(f == last)
    def _():
        for rows in _row_chunks(h_ref.shape[0]):
            x2 = x_ref[rows, :] + _rms_rows(partial_out(rows), gpost_ref[...])
            x2_ref[rows, :] = x2
            if with_next:
                hn_ref[rows, :] = _rms_rows(x2, gnext_ref[...]).astype(hn_ref.dtype)


def _mlp(h, w1, w2, li, x, g_post, g_next, tm=512, tf=1024):
    t = x.shape[0]
    with_next = g_next is not None
    row = lambda i, f: (i, 0)
    const = lambda i, f: (0, 0)
    in_specs = [pl.BlockSpec((tm, D_MODEL), row),
                pl.BlockSpec((None, D_MODEL, tf), lambda i, f: (li, 0, f)),
                pl.BlockSpec((None, tf, D_MODEL), lambda i, f: (li, f, 0)),
                pl.BlockSpec((tm, D_MODEL), row),
                pl.BlockSpec((1, D_MODEL), const)]
    args = [h, w1, w2, x, g_post.reshape(1, D_MODEL)]
    out_specs = [pl.BlockSpec((tm, D_MODEL), row)]
    out_shape = [jax.ShapeDtypeStruct((t, D_MODEL), F32)]
    if with_next:
        in_specs.append(pl.BlockSpec((1, D_MODEL), const))
        args.append(g_next.reshape(1, D_MODEL))
        out_specs.append(pl.BlockSpec((tm, D_MODEL), row))
        out_shape.append(jax.ShapeDtypeStruct((t, D_MODEL), BF16))
    res = pl.pallas_call(
        functools.partial(_mlp_kernel, with_next=with_next),
        grid=(t // tm, D_FF // tf),
        in_specs=in_specs,
        out_specs=out_specs,
        out_shape=out_shape,
        scratch_shapes=[pltpu.VMEM((tm, D_MODEL), F32)],
        compiler_params=_cparams(("parallel", "arbitrary")),
        name="mlp",
    )(*args)
    return (res[0], res[1]) if with_next else (res[0], None)


def _rope_tables(pos, dim):
    inv = 1.0 / (ROPE_THETA ** (jnp.arange(0, dim, 2, dtype=F32) / dim))
    ang = pos.astype(F32)[:, None] * inv[None, :]
    return jnp.cos(ang), jnp.sin(ang)


def _tables(seq):
    pos = jnp.arange(seq)
    ca, sa = _rope_tables(pos, HEAD_DIM)
    cos_a = jnp.concatenate([ca, ca], axis=1)
    sin_a = jnp.concatenate([-sa, sa], axis=1)
    z32 = jnp.zeros((seq, 32), F32)

    def half32(c, s):
        return jnp.concatenate([c, c], 1), jnp.concatenate([-s, z32], 1), jnp.concatenate([z32, s], 1)

    cr, sr = _rope_tables(pos // GRID_W, AXIAL_DIM)
    cc, sc = _rope_tables(pos % GRID_W, AXIAL_DIM)
    r_c, r_lo, r_hi = half32(cr, sr)
    c_c, c_lo, c_hi = half32(cc, sc)
    cos_b = jnp.concatenate([r_c, c_c], 1)
    sin_b_lo = jnp.concatenate([r_lo, c_lo], 1)
    sin_b_hi = jnp.concatenate([r_hi, c_hi], 1)
    c3, s3 = _rope_tables(pos, C_QK_DIM)
    m_c, m_lo, m_hi = half32(c3, s3)
    cos_c = jnp.concatenate([m_c, m_c], 1)
    sin_c_lo = jnp.concatenate([m_lo, m_lo], 1)
    sin_c_hi = jnp.concatenate([m_hi, m_hi], 1)
    return jnp.stack([cos_a, sin_a, cos_b, sin_b_lo, sin_b_hi, cos_c, sin_c_lo, sin_c_hi], axis=0)


def _trunk(x3, tabs, p):
    batch, seq, _ = x3.shape
    x = x3.reshape(batch * seq, D_MODEL)
    h = None
    for li in range(DEPTH):
        lam_init = 0.8 - 0.6 * math.exp(-0.3 * li)
        if li == 0:
            qkv, st1, st2, h = _qkv_proj(x, p["w_in"], li, tabs, p["b_q_norm"][li], p["b_k_norm"][li],
                                         batch, seq, g_pre=p["g_mix_pre"][li])
        else:
            qkv, st1, st2 = _qkv_proj(h, p["w_in"], li, tabs, p["b_q_norm"][li], p["b_k_norm"][li],
                                      batch, seq)
        oa, lse = zip(*[_attn_a_group(src, gi, batch, seq) for gi, src in enumerate((qkv, st1, st2))])
        ob = _attn_b(qkv, batch, seq)
        oc = _attn_c(qkv, p["c_lambda"][li], p["c_head_norm"][li], lam_init, batch, seq)
        merged = _merge(h, p["w_in"], li, oa, lse, ob, oc,
                        p["w_branch_a"], p["w_branch_b"], p["w_branch_c"], seq)
        x, hm = _out_proj(merged, p["w_out"], li, x, p["g_mix_post"][li], p["g_mlp_pre"][li])
        g_next = p["g_mix_pre"][li + 1] if li + 1 < DEPTH else None
        x, h = _mlp(hm, p["w_mlp_in"], p["w_mlp_out"], li, x, p["g_mlp_post"][li], g_next)
    return x.reshape(batch, seq, D_MODEL)


def kernel(x_prompt, x_sample, g_mix_pre, w_in, b_q_norm, b_k_norm, c_lambda, c_head_norm,
           w_branch_a, w_branch_b, w_branch_c, w_out, g_mix_post,
           g_mlp_pre, w_mlp_in, w_mlp_out, g_mlp_post):
    p = dict(g_mix_pre=g_mix_pre, w_in=w_in.astype(BF16), b_q_norm=b_q_norm, b_k_norm=b_k_norm,
             c_lambda=c_lambda, c_head_norm=c_head_norm,
             w_branch_a=w_branch_a.astype(BF16), w_branch_b=w_branch_b.astype(BF16),
             w_branch_c=w_branch_c.astype(BF16), w_out=w_out.astype(BF16), g_mix_post=g_mix_post,
             g_mlp_pre=g_mlp_pre, w_mlp_in=w_mlp_in.astype(BF16), w_mlp_out=w_mlp_out.astype(BF16),
             g_mlp_post=g_mlp_post)
    tabs = _tables(max(x_prompt.shape[1], x_sample.shape[1]))
    return (_trunk(x_prompt, tabs, p), _trunk(x_sample, tabs, p))
```

```python
import functools
import math

import jax
import jax.numpy as jnp
from jax import lax
from jax.experimental import pallas as pl
from jax.experimental.pallas import tpu as pltpu

F32 = jnp.float32
BF16 = jnp.bfloat16

D_MODEL = 2048
DEPTH = 2
HEAD_DIM = 128
GRID_W = 64
ROPE_THETA = 10000.0
EPS = 1e-6
NEG_INF = -1e30
LOG2E = math.log2(math.e)

DIL_GROUPS = ((128, 1), (512, 4), (2048, 16))
A_HALF = 64
A_W = 768
B_Q_HEADS = 6
B_KV_HEADS = 2
B_GROUP = B_Q_HEADS // B_KV_HEADS
B_QW = 768
B_KVW = 256
AXIAL_DIM = 64
C_HEADS = 4
C_QK_DIM = 64
C_QW = 512
C_VW = 512
N_BRANCH = 3
QKV_COLS = 3 * A_W + B_QW + 2 * B_KVW + 2 * C_QW + C_VW
GATE_COLS = N_BRANCH * D_MODEL
D_FF = 4 * D_MODEL

OFF_AQ, OFF_AK, OFF_AV = 0, A_W, 2 * A_W
OFF_BQ = 3 * A_W
OFF_BK, OFF_BV = OFF_BQ + B_QW, OFF_BQ + B_QW + B_KVW
OFF_CQ = OFF_BV + B_KVW
OFF_CK, OFF_CV = OFF_CQ + C_QW, OFF_CQ + 2 * C_QW

SUB = 2 * HEAD_DIM
_QKV_KINDS = (("aq",) * 3 + ("ak",) * 3 + ("v",) * 3 + ("bq",) * 3 + ("bk",) + ("v",)
              + ("cq",) * 2 + ("ck",) * 2 + ("v",) * 2)
N_TABLES = 8

V7X_VMEM_BYTES = 64 * 1024 * 1024
VMEM_LIMIT = V7X_VMEM_BYTES - 8 * 1024 * 1024


def _cparams(sem):
    return pltpu.CompilerParams(dimension_semantics=sem, vmem_limit_bytes=VMEM_LIMIT)


def _rms_rows(xf, g):
    ms = jnp.mean(xf * xf, axis=-1, keepdims=True)
    return xf * lax.rsqrt(ms + EPS) * g


def _rope_half64(x, cos, sin_signed):
    return x * cos + pltpu.roll(x, 64, axis=1) * sin_signed


def _rope_half32(x, cos, sin_lo, sin_hi):
    return x * cos + pltpu.roll(x, 96, axis=1) * sin_lo + pltpu.roll(x, 32, axis=1) * sin_hi


_STREAM_TILES = {1: (1, 4, 7), 2: (2, 5, 8)}


ROW_CHUNK = 256


def _row_chunks(n):
    return [slice(r, r + ROW_CHUNK) for r in range(0, n, ROW_CHUNK)]


def _qkv_kernel(*refs, tm, norm_input):
    if norm_input:
        x_ref, gpre_ref, w_ref, tab_ref, gq_ref, gk_ref, o_ref, s1_ref, s2_ref, h_ref, epi_scr = refs
        for rows in _row_chunks(tm):
            h_ref[rows, :] = _rms_rows(x_ref[rows, :], gpre_ref[...]).astype(h_ref.dtype)
    else:
        h_ref, w_ref, tab_ref, gq_ref, gk_ref, o_ref, s1_ref, s2_ref, epi_scr = refs

    def head_epilogue(x, kind, rows):
        if kind in ("aq", "ak"):
            y = _rope_half64(x, tab_ref[0, rows, :], tab_ref[1, rows, :])
            return y * (HEAD_DIM ** -0.5 * LOG2E) if kind == "aq" else y
        if kind in ("bq", "bk"):
            g = gq_ref[...] * (HEAD_DIM ** -0.5 * LOG2E) if kind == "bq" else gk_ref[...]
            return _rope_half32(_rms_rows(x, g), tab_ref[2, rows, :], tab_ref[3, rows, :], tab_ref[4, rows, :])
        if kind in ("cq", "ck"):
            y = _rope_half32(x, tab_ref[5, rows, :], tab_ref[6, rows, :], tab_ref[7, rows, :])
            return y * (C_QK_DIM ** -0.5 * LOG2E) if kind == "cq" else y
        return x

    n_heads = SUB // HEAD_DIM
    for jj, kind in enumerate(_QKV_KINDS):
        g = next((g for g in (1, 2) if jj in _STREAM_TILES[g]), None)
        for c, rows in enumerate(_row_chunks(tm)):
            acc = jnp.dot(h_ref[rows, :], w_ref[:, jj * SUB:(jj + 1) * SUB], preferred_element_type=F32)
            for hh in range(n_heads):
                sl = slice(hh * HEAD_DIM, (hh + 1) * HEAD_DIM)
                y = head_epilogue(acc[:, sl], kind, rows)
                col = jj * SUB + hh * HEAD_DIM
                o_ref[rows, col:col + HEAD_DIM] = y.astype(o_ref.dtype)
                if g is not None:
                    ref = (s1_ref, s2_ref)[g - 1]
                    dil = DIL_GROUPS[g][1]
                    nu = ROW_CHUNK // dil
                    slot = (jj % 2) * n_heads + hh
                    col0 = _STREAM_TILES[g].index(jj) * SUB + hh * HEAD_DIM
                    epi_scr[slot, rows, :] = y
                    for r in range(dil):
                        rows_r = epi_scr[slot, pl.ds(rows.start + r, nu, stride=dil), :]
                        ref[r, c * nu:(c + 1) * nu, col0:col0 + HEAD_DIM] = rows_r.astype(ref.dtype)


def _qkv_proj(h, w_in, li, tabs, gq, gk, batch, seq, g_pre=None, tm=512):
    t = h.shape[0]
    nseq = seq // tm
    d1, d2 = DIL_GROUPS[1][1], DIL_GROUPS[2][1]
    norm_input = g_pre is not None
    row = pl.BlockSpec((tm, D_MODEL), lambda i: (i, 0))
    vec = lambda n: pl.BlockSpec((1, n), lambda i: (0, 0))

    def stream_spec(dil):
        return pl.BlockSpec((None, dil, tm // dil, 3 * SUB), lambda i: (i // nseq, 0, i % nseq, 0))

    in_specs = [pl.BlockSpec((None, D_MODEL, QKV_COLS), lambda i: (li, 0, 0), pipeline_mode=pl.Buffered(1)),
                pl.BlockSpec((N_TABLES, tm, HEAD_DIM), lambda i: (0, i % nseq, 0)),
                vec(HEAD_DIM), vec(HEAD_DIM)]
    args = [w_in, tabs, gq.reshape(1, HEAD_DIM), gk.reshape(1, HEAD_DIM)]
    out_specs = [pl.BlockSpec((tm, QKV_COLS), lambda i: (i, 0)), stream_spec(d1), stream_spec(d2)]
    out_shape = [jax.ShapeDtypeStruct((t, QKV_COLS), BF16),
                 jax.ShapeDtypeStruct((batch, d1, seq // d1, 3 * SUB), BF16),
                 jax.ShapeDtypeStruct((batch, d2, seq // d2, 3 * SUB), BF16)]
    if norm_input:
        in_specs = [row, vec(D_MODEL)] + in_specs
        args = [h, g_pre.reshape(1, D_MODEL)] + args
        out_specs.append(row)
        out_shape.append(jax.ShapeDtypeStruct((t, D_MODEL), BF16))
    else:
        in_specs = [row] + in_specs
        args = [h] + args
    return pl.pallas_call(
        functools.partial(_qkv_kernel, tm=tm, norm_input=norm_input),
        grid=(t // tm,),
        in_specs=in_specs,
        out_specs=out_specs,
        out_shape=out_shape,
        scratch_shapes=[pltpu.VMEM((2 * (SUB // HEAD_DIM), tm, HEAD_DIM), F32)],
        compiler_params=_cparams(("parallel",)),
        name="qkv_proj",
    )(*args)


A_TQ = 128
A_TK = A_TQ + 2 * A_HALF


def _attn_a_kernel(q_ref, k_ref, v_ref, o_ref, lse_ref, *, length):
    nt = q_ref.shape[0] // A_TQ
    tiles_per_stream = length // A_TQ
    rows = lax.broadcasted_iota(jnp.int32, (A_TQ, A_TK), 0)
    cols = lax.broadcasted_iota(jnp.int32, (A_TQ, A_TK), 1)

    def body(i, carry):
        q0 = pl.multiple_of(i * A_TQ, A_TQ)
        stream0 = (i // tiles_per_stream) * length
        k0 = pl.multiple_of(jnp.clip(q0 - A_HALF, stream0, stream0 + length - A_TK), 16)
        valid = jnp.abs(rows - cols + (q0 - k0)) <= A_HALF
        for hh in range(2):
            sl = slice(hh * HEAD_DIM, (hh + 1) * HEAD_DIM)
            q = q_ref[pl.ds(q0, A_TQ), sl]
            k = k_ref[pl.ds(k0, A_TK), sl]
            v = v_ref[pl.ds(k0, A_TK), sl]
            s = lax.dot_general(q, k, (((1,), (1,)), ((), ())), preferred_element_type=F32)
            s = jnp.where(valid, s, NEG_INF)
            m = jnp.max(s, axis=1, keepdims=True)
            e = jnp.exp2(s - m)
            den = jnp.sum(e, axis=1, keepdims=True)
            o = jnp.dot(e.astype(BF16), v, preferred_element_type=F32) / den
            o_ref[pl.ds(q0, A_TQ), sl] = o.astype(o_ref.dtype)
            lse_ref[pl.ds(q0, A_TQ), sl] = jnp.broadcast_to(m + jnp.log2(den), (A_TQ, HEAD_DIM))
        return carry

    lax.fori_loop(0, nt, body, 0, unroll=min(nt, 4))


def _attn_a_group(src, gi, batch, seq):
    dil = DIL_GROUPS[gi][1]
    length = seq // dil
    if dil == 1:
        cols = (OFF_AQ // SUB + gi, OFF_AK // SUB + gi, OFF_AV // SUB + gi)
    else:
        src = src.reshape(batch * seq, 3 * SUB)
        cols = (0, 1, 2)
    out_spec = pl.BlockSpec((seq, SUB), lambda b: (b, 0))
    o, lse = pl.pallas_call(
        functools.partial(_attn_a_kernel, length=length),
        grid=(batch,),
        in_specs=[pl.BlockSpec((seq, SUB), functools.partial(lambda c, b: (b, c), c)) for c in cols],
        out_specs=[out_spec, out_spec],
        out_shape=[jax.ShapeDtypeStruct((batch * seq, SUB), BF16),
                   jax.ShapeDtypeStruct((batch * seq, SUB), F32)],
        compiler_params=_cparams(("parallel",)),
        name=f"attn_a{gi}",
    )(src, src, src)
    if dil == 1:
        return o, lse
    return o.reshape(batch, dil, length, SUB), lse.reshape(batch, dil, length, SUB)


FLASH_BLOCK = 256


def _transpose_values(v_ref, vt_scr, tk):
    for t in range(vt_scr.shape[0]):
        vt_scr[t] = v_ref[t * tk:(t + 1) * tk, :].astype(F32).T.astype(vt_scr.dtype)


N_SCORE_BUFS = 3


def _flash_scratch(seq, tk, rows):
    return ([pltpu.VMEM((seq // tk, HEAD_DIM, tk), BF16),
             pltpu.VMEM((N_SCORE_BUFS, tk, rows), F32),
             pltpu.VMEM((2, tk, rows), BF16),
             pltpu.VMEM((HEAD_DIM, rows), F32)])


def _flash_keymajor(q, k_ref, vt_scr, s_scr, p_scr, acc_scr, tk):
    rows = q.shape[0]
    nk = vt_scr.shape[0]
    q_blocks = [slice(c, c + FLASH_BLOCK) for c in range(0, rows, FLASH_BLOCK)]
    k_chunks = [slice(c, c + FLASH_BLOCK) for c in range(0, tk, FLASH_BLOCK)]
    bufs = [s_scr.at[b] for b in range(N_SCORE_BUFS)]

    def scores(t, qb, dst):
        k = k_ref[t * tk:(t + 1) * tk, :]
        dst[:, qb] = lax.dot_general(k, q[qb, :], (((1,), (1,)), ((), ())), preferred_element_type=F32)

    def update(m, l, src, t, qb):
        m_new = m
        for kc in k_chunks:
            m_new = jnp.maximum(m_new, jnp.max(src[kc, qb], axis=0, keepdims=True))
        alpha = jnp.exp2(m - m_new)
        l_tile = None
        p_buf = p_scr.at[t % 2]
        for kc in k_chunks:
            p = jnp.exp2(src[kc, qb] - m_new)
            p_sum = jnp.sum(p, axis=0, keepdims=True)
            l_tile = p_sum if l_tile is None else l_tile + p_sum
            p_buf[kc, qb] = p.astype(BF16)
        pv = jnp.dot(vt_scr[t], p_buf[:, qb], preferred_element_type=F32)
        acc_scr[:, qb] = pv if t == 0 else alpha * acc_scr[:, qb] + pv
        return m_new, alpha * l + l_tile

    m = [jnp.full((1, FLASH_BLOCK), NEG_INF, F32) for _ in q_blocks]
    l = [jnp.zeros((1, FLASH_BLOCK), F32) for _ in q_blocks]
    for qb in q_blocks:
        scores(0, qb, bufs[0])
    for t in range(nk):
        for bi, qb in enumerate(q_blocks):
            if t + 1 < nk:
                scores(t + 1, qb, bufs[(t + 1) % N_SCORE_BUFS])
            m[bi], l[bi] = update(m[bi], l[bi], bufs[t % N_SCORE_BUFS], t, qb)
    return acc_scr[...], jnp.concatenate(l, axis=1)


def _attn_b_kernel(q_ref, k_ref, v_ref, o_ref, vt_scr, s_scr, p_scr, acc_scr, *, tq, tk):
    @pl.when(pl.program_id(2) == 0)
    def _():
        _transpose_values(v_ref, vt_scr, tk)

    q = jnp.concatenate([q_ref[:, h * HEAD_DIM:(h + 1) * HEAD_DIM] for h in range(B_GROUP)], axis=0)
    acc, l = _flash_keymajor(q, k_ref, vt_scr, s_scr, p_scr, acc_scr, tk)
    o = acc * (1.0 / l)
    for h in range(B_GROUP):
        o_ref[:, h * HEAD_DIM:(h + 1) * HEAD_DIM] = o[:, h * tq:(h + 1) * tq].T.astype(o_ref.dtype)


def _attn_b(qkv, batch, seq, tq=512, tk=1024):
    nq = seq // tq
    gw = B_GROUP * HEAD_DIM
    rows = B_GROUP * tq
    return pl.pallas_call(
        functools.partial(_attn_b_kernel, tq=tq, tk=tk),
        grid=(batch, B_KV_HEADS, nq),
        in_specs=[pl.BlockSpec((tq, gw), lambda b, kh, i: (b * nq + i, OFF_BQ // gw + kh)),
                  pl.BlockSpec((seq, HEAD_DIM), lambda b, kh, i: (b, OFF_BK // HEAD_DIM + kh)),
                  pl.BlockSpec((seq, HEAD_DIM), lambda b, kh, i: (b, OFF_BV // HEAD_DIM + kh))],
        out_specs=pl.BlockSpec((tq, gw), lambda b, kh, i: (b * nq + i, kh)),
        out_shape=jax.ShapeDtypeStruct((batch * seq, B_QW), BF16),
        scratch_shapes=_flash_scratch(seq, tk, rows),
        compiler_params=_cparams(("parallel", "parallel", "arbitrary")),
        name="attn_b",
    )(qkv, qkv, qkv)


def _attn_c_kernel(lam_ref, g_ref, q_ref, k_ref, v_ref, o_ref, vt_scr, s_scr, p_scr, acc_scr,
                   *, tq, tk, lam_init):
    @pl.when(pl.program_id(2) == 0)
    def _():
        _transpose_values(v_ref, vt_scr, tk)

    lamv = lam_ref[...]
    lam = (jnp.exp(jnp.sum(lamv[0:1] * lamv[1:2], axis=1, keepdims=True))
           - jnp.exp(jnp.sum(lamv[2:3] * lamv[3:4], axis=1, keepdims=True)) + lam_init)
    q = q_ref[...]
    lane = lax.broadcasted_iota(jnp.int32, q.shape, 1)
    zero = jnp.zeros_like(q)
    q2 = jnp.concatenate([jnp.where(lane < C_QK_DIM, q, zero), jnp.where(lane >= C_QK_DIM, q, zero)], axis=0)
    acc, l = _flash_keymajor(q2, k_ref, vt_scr, s_scr, p_scr, acc_scr, tk)
    o = acc * (1.0 / l)
    oc = (o[:, :tq] - lam * o[:, tq:]).T
    o_ref[...] = (_rms_rows(oc, g_ref[...]) * (1.0 - lam_init)).astype(o_ref.dtype)


def _attn_c(qkv, c_lambda, c_head_norm, lam_init, batch, seq, tq=512, tk=1024):
    nq = seq // tq
    rows = 2 * tq
    return pl.pallas_call(
        functools.partial(_attn_c_kernel, tq=tq, tk=tk, lam_init=lam_init),
        grid=(batch, C_HEADS, nq),
        in_specs=[pl.BlockSpec((4, C_QK_DIM), lambda b, h, i: (0, 0)),
                  pl.BlockSpec((1, HEAD_DIM), lambda b, h, i: (0, 0)),
                  pl.BlockSpec((tq, HEAD_DIM), lambda b, h, i: (b * nq + i, OFF_CQ // HEAD_DIM + h)),
                  pl.BlockSpec((seq, HEAD_DIM), lambda b, h, i: (b, OFF_CK // HEAD_DIM + h)),
                  pl.BlockSpec((seq, HEAD_DIM), lambda b, h, i: (b, OFF_CV // HEAD_DIM + h))],
        out_specs=pl.BlockSpec((tq, HEAD_DIM), lambda b, h, i: (b * nq + i, h)),
        out_shape=jax.ShapeDtypeStruct((batch * seq, C_VW), BF16),
        scratch_shapes=_flash_scratch(seq, tk, rows),
        compiler_params=_cparams(("parallel", "parallel", "arbitrary")),
        name="attn_c",
    )(c_lambda, c_head_norm.reshape(1, HEAD_DIM), qkv, qkv, qkv)


def _merge_kernel(h_ref, wg0, wg1, wg2, oa0, oa1, oa2, ls0, ls1, ls2, ob, oc, wa, wb, wc, out_ref,
                  oa_scr, o1_scr, o2_scr, l1_scr, l2_scr, *, tm):
    @pl.when(pl.program_id(1) == 0)
    def _():
        for g, o_src, l_src, o_dst, l_dst in ((1, oa1, ls1, o1_scr, l1_scr), (2, oa2, ls2, o2_scr, l2_scr)):
            dil = DIL_GROUPS[g][1]
            for r in range(dil):
                for hh in range(SUB // HEAD_DIM):
                    sl = slice(hh * HEAD_DIM, (hh + 1) * HEAD_DIM)
                    o_dst[hh, pl.ds(r, tm // dil, stride=dil), :] = o_src[r, :, sl].astype(F32)
                    l_dst[hh, pl.ds(r, tm // dil, stride=dil), :] = l_src[r, :, sl]
        for hh in range(SUB // HEAD_DIM):
            sl = slice(hh * HEAD_DIM, (hh + 1) * HEAD_DIM)
            l0, l1, l2 = ls0[:, sl], l1_scr[hh], l2_scr[hh]
            m = jnp.maximum(jnp.maximum(l0, l1), l2)
            e0, e1, e2 = jnp.exp2(l0 - m), jnp.exp2(l1 - m), jnp.exp2(l2 - m)
            num = e0 * oa0[:, sl].astype(F32) + e1 * o1_scr[hh] + e2 * o2_scr[hh]
            oa_scr[:, sl] = (num / (e0 + e1 + e2)).astype(oa_scr.dtype)

    for rows in _row_chunks(tm):
        hr = h_ref[rows, :]
        merged = None
        for wg, o_br, w_br in ((wg0, oa_scr, wa), (wg1, ob, wb), (wg2, oc, wc)):
            gate = jax.nn.sigmoid(jnp.dot(hr, wg[...], preferred_element_type=F32))
            term = gate * jnp.dot(o_br[rows, :], w_br[...], preferred_element_type=F32)
            merged = term if merged is None else merged + term
        out_ref[rows, :] = merged.astype(out_ref.dtype)


def _merge(h, w_in, li, oa, lse, ob, oc, wa, wb, wc, seq, tm=1024, tn=512):
    t = ob.shape[0]
    nj = D_MODEL // tn
    nseq = seq // tm

    def row(w):
        return pl.BlockSpec((tm, w), lambda i, j: (i, 0))

    def stream(g):
        dil = DIL_GROUPS[g][1]
        return pl.BlockSpec((None, dil, tm // dil, SUB), lambda i, j: (i // nseq, 0, i % nseq, 0))

    def gate_w(br):
        return pl.BlockSpec((None, D_MODEL, tn),
                            lambda i, j: (li, 0, (QKV_COLS + br * D_MODEL) // tn + j))

    def wspec(k):
        return pl.BlockSpec((None, k, tn), lambda i, j: (li, 0, j))

    return pl.pallas_call(
        functools.partial(_merge_kernel, tm=tm),
        grid=(t // tm, nj),
        in_specs=[row(D_MODEL), gate_w(0), gate_w(1), gate_w(2),
                  row(SUB), stream(1), stream(2), row(SUB), stream(1), stream(2),
                  row(B_QW), row(C_VW), wspec(SUB), wspec(B_QW), wspec(C_VW)],
        out_specs=pl.BlockSpec((tm, tn), lambda i, j: (i, j)),
        out_shape=jax.ShapeDtypeStruct((t, D_MODEL), BF16),
        scratch_shapes=[pltpu.VMEM((tm, SUB), BF16)] + [pltpu.VMEM((SUB // HEAD_DIM, tm, HEAD_DIM), F32)] * 4,
        compiler_params=_cparams(("parallel", "arbitrary")),
        name="merge",
    )(h, w_in, w_in, w_in, *oa, *lse, ob, oc, wa, wb, wc)


def _out_kernel(m_ref, w_ref, x_ref, gpost_ref, gnext_ref, x1_ref, h_ref):
    for rows in _row_chunks(m_ref.shape[0]):
        y = jnp.dot(m_ref[rows, :], w_ref[...], preferred_element_type=F32)
        x1 = x_ref[rows, :] + _rms_rows(y, gpost_ref[...])
        x1_ref[rows, :] = x1
        h_ref[rows, :] = _rms_rows(x1, gnext_ref[...]).astype(h_ref.dtype)


def _out_proj(merged, w_out, li, x, g_post, g_next, tm=512):
    t = x.shape[0]
    row = lambda i: (i, 0)
    const = lambda i: (0, 0)
    return pl.pallas_call(
        _out_kernel,
        grid=(t // tm,),
        in_specs=[pl.BlockSpec((tm, D_MODEL), row),
                  pl.BlockSpec((None, D_MODEL, D_MODEL), lambda i: (li, 0, 0)),
                  pl.BlockSpec((tm, D_MODEL), row),
                  pl.BlockSpec((1, D_MODEL), const),
                  pl.BlockSpec((1, D_MODEL), const)],
        out_specs=[pl.BlockSpec((tm, D_MODEL), row), pl.BlockSpec((tm, D_MODEL), row)],
        out_shape=[jax.ShapeDtypeStruct((t, D_MODEL), F32), jax.ShapeDtypeStruct((t, D_MODEL), BF16)],
        compiler_params=_cparams(("parallel",)),
        name="out_proj",
    )(merged, w_out, x, g_post.reshape(1, D_MODEL), g_next.reshape(1, D_MODEL))


def _mlp_kernel(h_ref, w1_ref, w2_ref, x_ref, gpost_ref, *rest, with_next):
    if with_next:
        gnext_ref, x2_ref, hn_ref, acc_ref = rest
    else:
        x2_ref, acc_ref = rest
    f = pl.program_id(1)

    @pl.when(f == 0)
    def _():
        acc_ref[...] = jnp.zeros_like(acc_ref)

    def partial_out(rows):
        u = jnp.dot(h_ref[rows, :], w1_ref[...], preferred_element_type=F32)
        u = jnp.square(jnp.maximum(u, 0.0)).astype(BF16)
        return acc_ref[rows, :] + jnp.dot(u, w2_ref[...], preferred_element_type=F32)

    last = pl.num_programs(1) - 1

    @pl.when(f < last)
    def _():
        acc_ref[...] = partial_out(slice(None))

    @pl.when(f == last)
    def _():
        for rows in _row_chunks(h_ref.shape[0]):
            x2 = x_ref[rows, :] + _rms_rows(partial_out(rows), gpost_ref[...])
            x2_ref[rows, :] = x2
            if with_next:
                hn_ref[rows, :] = _rms_rows(x2, gnext_ref[...]).astype(hn_ref.dtype)


def _mlp(h, w1, w2, li, x, g_post, g_next, tm=512, tf=1024):
    t = x.shape[0]
    with_next = g_next is not None
    row = lambda i, f: (i, 0)
    const = lambda i, f: (0, 0)
    in_specs = [pl.BlockSpec((tm, D_MODEL), row),
                pl.BlockSpec((None, D_MODEL, tf), lambda i, f: (li, 0, f)),
                pl.BlockSpec((None, tf, D_MODEL), lambda i, f: (li, f, 0)),
                pl.BlockSpec((tm, D_MODEL), row),
                pl.BlockSpec((1, D_MODEL), const)]
    args = [h, w1, w2, x, g_post.reshape(1, D_MODEL)]
    out_specs = [pl.BlockSpec((tm, D_MODEL), row)]
    out_shape = [jax.ShapeDtypeStruct((t, D_MODEL), F32)]
    if with_next:
        in_specs.append(pl.BlockSpec((1, D_MODEL), const))
        args.append(g_next.reshape(1, D_MODEL))
        out_specs.append(pl.BlockSpec((tm, D_MODEL), row))
        out_shape.append(jax.ShapeDtypeStruct((t, D_MODEL), BF16))
    res = pl.pallas_call(
        functools.partial(_mlp_kernel, with_next=with_next),
        grid=(t // tm, D_FF // tf),
        in_specs=in_specs,
        out_specs=out_specs,
        out_shape=out_shape,
        scratch_shapes=[pltpu.VMEM((tm, D_MODEL), F32)],
        compiler_params=_cparams(("parallel", "arbitrary")),
        name="mlp",
    )(*args)
    return (res[0], res[1]) if with_next else (res[0], None)


def _rope_tables(pos, dim):
    inv = 1.0 / (ROPE_THETA ** (jnp.arange(0, dim, 2, dtype=F32) / dim))
    ang = pos.astype(F32)[:, None] * inv[None, :]
    return jnp.cos(ang), jnp.sin(ang)


def _tables(seq):
    pos = jnp.arange(seq)
    ca, sa = _rope_tables(pos, HEAD_DIM)
    cos_a = jnp.concatenate([ca, ca], axis=1)
    sin_a = jnp.concatenate([-sa, sa], axis=1)
    z32 = jnp.zeros((seq, 32), F32)

    def half32(c, s):
        return jnp.concatenate([c, c], 1), jnp.concatenate([-s, z32], 1), jnp.concatenate([z32, s], 1)

    cr, sr = _rope_tables(pos // GRID_W, AXIAL_DIM)
    cc, sc = _rope_tables(pos % GRID_W, AXIAL_DIM)
    r_c, r_lo, r_hi = half32(cr, sr)
    c_c, c_lo, c_hi = half32(cc, sc)
    cos_b = jnp.concatenate([r_c, c_c], 1)
    sin_b_lo = jnp.concatenate([r_lo, c_lo], 1)
    sin_b_hi = jnp.concatenate([r_hi, c_hi], 1)
    c3, s3 = _rope_tables(pos, C_QK_DIM)
    m_c, m_lo, m_hi = half32(c3, s3)
    cos_c = jnp.concatenate([m_c, m_c], 1)
    sin_c_lo = jnp.concatenate([m_lo, m_lo], 1)
    sin_c_hi = jnp.concatenate([m_hi, m_hi], 1)
    return jnp.stack([cos_a, sin_a, cos_b, sin_b_lo, sin_b_hi, cos_c, sin_c_lo, sin_c_hi], axis=0)


def _trunk(x3, tabs, p):
    batch, seq, _ = x3.shape
    x = x3.reshape(batch * seq, D_MODEL)
    h = None
    for li in range(DEPTH):
        lam_init = 0.8 - 0.6 * math.exp(-0.3 * li)
        if li == 0:
            qkv, st1, st2, h = _qkv_proj(x, p["w_in"], li, tabs, p["b_q_norm"][li], p["b_k_norm"][li],
                                         batch, seq, g_pre=p["g_mix_pre"][li])
        else:
            qkv, st1, st2 = _qkv_proj(h, p["w_in"], li, tabs, p["b_q_norm"][li], p["b_k_norm"][li],
                                      batch, seq)
        oa, lse = zip(*[_attn_a_group(src, gi, batch, seq) for gi, src in enumerate((qkv, st1, st2))])
        ob = _attn_b(qkv, batch, seq)
        oc = _attn_c(qkv, p["c_lambda"][li], p["c_head_norm"][li], lam_init, batch, seq)
        merged = _merge(h, p["w_in"], li, oa, lse, ob, oc,
                        p["w_branch_a"], p["w_branch_b"], p["w_branch_c"], seq)
        x, hm = _out_proj(merged, p["w_out"], li, x, p["g_mix_post"][li], p["g_mlp_pre"][li])
        g_next = p["g_mix_pre"][li + 1] if li + 1 < DEPTH else None
        x, h = _mlp(hm, p["w_mlp_in"], p["w_mlp_out"], li, x, p["g_mlp_post"][li], g_next)
    return x.reshape(batch, seq, D_MODEL)


def kernel(x_prompt, x_sample, g_mix_pre, w_in, b_q_norm, b_k_norm, c_lambda, c_head_norm,
           w_branch_a, w_branch_b, w_branch_c, w_out, g_mix_post,
           g_mlp_pre, w_mlp_in, w_mlp_out, g_mlp_post):
    p = dict(g_mix_pre=g_mix_pre, w_in=w_in.astype(BF16), b_q_norm=b_q_norm, b_k_norm=b_k_norm,
             c_lambda=c_lambda, c_head_norm=c_head_norm,
             w_branch_a=w_branch_a.astype(BF16), w_branch_b=w_branch_b.astype(BF16),
             w_branch_c=w_branch_c.astype(BF16), w_out=w_out.astype(BF16), g_mix_post=g_mix_post,
             g_mlp_pre=g_mlp_pre, w_mlp_in=w_mlp_in.astype(BF16), w_mlp_out=w_mlp_out.astype(BF16),
             g_mlp_post=g_mlp_post)
    tabs = _tables(max(x_prompt.shape[1], x_sample.shape[1]))
    return (_trunk(x_prompt, tabs, p), _trunk(x_sample, tabs, p))
```

```python
import functools
import math

import jax
import jax.numpy as jnp
from jax import lax
from jax.experimental import pallas as pl
from jax.experimental.pallas import tpu as pltpu

F32 = jnp.float32
BF16 = jnp.bfloat16

D_MODEL = 2048
DEPTH = 2
HEAD_DIM = 128
GRID_W = 64
ROPE_THETA = 10000.0
EPS = 1e-6
NEG_INF = -1e30
LOG2E = math.log2(math.e)

DIL_GROUPS = ((128, 1), (512, 4), (2048, 16))
A_HALF = 64
A_W = 768
B_Q_HEADS = 6
B_KV_HEADS = 2
B_GROUP = B_Q_HEADS // B_KV_HEADS
B_QW = 768
B_KVW = 256
AXIAL_DIM = 64
C_HEADS = 4
C_QK_DIM = 64
C_QW = 512
C_VW = 512
N_BRANCH = 3
QKV_COLS = 3 * A_W + B_QW + 2 * B_KVW + 2 * C_QW + C_VW
GATE_COLS = N_BRANCH * D_MODEL
D_FF = 4 * D_MODEL

OFF_AQ, OFF_AK, OFF_AV = 0, A_W, 2 * A_W
OFF_BQ = 3 * A_W
OFF_BK, OFF_BV = OFF_BQ + B_QW, OFF_BQ + B_QW + B_KVW
OFF_CQ = OFF_BV + B_KVW
OFF_CK, OFF_CV = OFF_CQ + C_QW, OFF_CQ + 2 * C_QW

SUB = 2 * HEAD_DIM
_QKV_KINDS = (("aq",) * 3 + ("ak",) * 3 + ("v",) * 3 + ("bq",) * 3 + ("bk",) + ("v",)
              + ("cq",) * 2 + ("ck",) * 2 + ("v",) * 2)
N_TABLES = 8

V7X_VMEM_BYTES = 64 * 1024 * 1024
VMEM_LIMIT = V7X_VMEM_BYTES - 8 * 1024 * 1024


def _cparams(sem):
    return pltpu.CompilerParams(dimension_semantics=sem, vmem_limit_bytes=VMEM_LIMIT)


def _rms_rows(xf, g):
    ms = jnp.mean(xf * xf, axis=-1, keepdims=True)
    return xf * lax.rsqrt(ms + EPS) * g


def _rope_half64(x, cos, sin_signed):
    return x * cos + pltpu.roll(x, 64, axis=1) * sin_signed


def _rope_half32(x, cos, sin_lo, sin_hi):
    return x * cos + pltpu.roll(x, 96, axis=1) * sin_lo + pltpu.roll(x, 32, axis=1) * sin_hi


_STREAM_TILES = {1: (1, 4, 7), 2: (2, 5, 8)}


ROW_CHUNK = 256


def _row_chunks(n):
    return [slice(r, r + ROW_CHUNK) for r in range(0, n, ROW_CHUNK)]


def _qkv_kernel(*refs, tm, norm_input):
    if norm_input:
        x_ref, gpre_ref, w_ref, tab_ref, gq_ref, gk_ref, o_ref, s1_ref, s2_ref, h_ref, epi_scr = refs
        for rows in _row_chunks(tm):
            h_ref[rows, :] = _rms_rows(x_ref[rows, :], gpre_ref[...]).astype(h_ref.dtype)
    else:
        h_ref, w_ref, tab_ref, gq_ref, gk_ref, o_ref, s1_ref, s2_ref, epi_scr = refs

    def head_epilogue(x, kind, rows):
        if kind in ("aq", "ak"):
            y = _rope_half64(x, tab_ref[0, rows, :], tab_ref[1, rows, :])
            return y * (HEAD_DIM ** -0.5 * LOG2E) if kind == "aq" else y
        if kind in ("bq", "bk"):
            g = gq_ref[...] * (HEAD_DIM ** -0.5 * LOG2E) if kind == "bq" else gk_ref[...]
            return _rope_half32(_rms_rows(x, g), tab_ref[2, rows, :], tab_ref[3, rows, :], tab_ref[4, rows, :])
        if kind in ("cq", "ck"):
            y = _rope_half32(x, tab_ref[5, rows, :], tab_ref[6, rows, :], tab_ref[7, rows, :])
            return y * (C_QK_DIM ** -0.5 * LOG2E) if kind == "cq" else y
        return x

    n_heads = SUB // HEAD_DIM
    for jj, kind in enumerate(_QKV_KINDS):
        g = next((g for g in (1, 2) if jj in _STREAM_TILES[g]), None)
        for c, rows in enumerate(_row_chunks(tm)):
            acc = jnp.dot(h_ref[rows, :], w_ref[:, jj * SUB:(jj + 1) * SUB], preferred_element_type=F32)
            for hh in range(n_heads):
                sl = slice(hh * HEAD_DIM, (hh + 1) * HEAD_DIM)
                y = head_epilogue(acc[:, sl], kind, rows)
                col = jj * SUB + hh * HEAD_DIM
                o_ref[rows, col:col + HEAD_DIM] = y.astype(o_ref.dtype)
                if g is not None:
                    ref = (s1_ref, s2_ref)[g - 1]
                    dil = DIL_GROUPS[g][1]
                    nu = ROW_CHUNK // dil
                    slot = (jj % 2) * n_heads + hh
                    col0 = _STREAM_TILES[g].index(jj) * SUB + hh * HEAD_DIM
                    epi_scr[slot, rows, :] = y
                    for r in range(dil):
                        rows_r = epi_scr[slot, pl.ds(rows.start + r, nu, stride=dil), :]
                        ref[r, c * nu:(c + 1) * nu, col0:col0 + HEAD_DIM] = rows_r.astype(ref.dtype)


def _qkv_proj(h, w_in, li, tabs, gq, gk, batch, seq, g_pre=None, tm=512):
    t = h.shape[0]
    nseq = seq // tm
    d1, d2 = DIL_GROUPS[1][1], DIL_GROUPS[2][1]
    norm_input = g_pre is not None
    row = pl.BlockSpec((tm, D_MODEL), lambda i: (i, 0))
    vec = lambda n: pl.BlockSpec((1, n), lambda i: (0, 0))

    def stream_spec(dil):
        return pl.BlockSpec((None, dil, tm // dil, 3 * SUB), lambda i: (i // nseq, 0, i % nseq, 0))

    in_specs = [pl.BlockSpec((None, D_MODEL, QKV_COLS), lambda i: (li, 0, 0), pipeline_mode=pl.Buffered(1)),
                pl.BlockSpec((N_TABLES, tm, HEAD_DIM), lambda i: (0, i % nseq, 0)),
                vec(HEAD_DIM), vec(HEAD_DIM)]
    args = [w_in, tabs, gq.reshape(1, HEAD_DIM), gk.reshape(1, HEAD_DIM)]
    out_specs = [pl.BlockSpec((tm, QKV_COLS), lambda i: (i, 0)), stream_spec(d1), stream_spec(d2)]
    out_shape = [jax.ShapeDtypeStruct((t, QKV_COLS), BF16),
                 jax.ShapeDtypeStruct((batch, d1, seq // d1, 3 * SUB), BF16),
                 jax.ShapeDtypeStruct((batch, d2, seq // d2, 3 * SUB), BF16)]
    if norm_input:
        in_specs = [row, vec(D_MODEL)] + in_specs
        args = [h, g_pre.reshape(1, D_MODEL)] + args
        out_specs.append(row)
        out_shape.append(jax.ShapeDtypeStruct((t, D_MODEL), BF16))
    else:
        in_specs = [row] + in_specs
        args = [h] + args
    return pl.pallas_call(
        functools.partial(_qkv_kernel, tm=tm, norm_input=norm_input),
        grid=(t // tm,),
        in_specs=in_specs,
        out_specs=out_specs,
        out_shape=out_shape,
        scratch_shapes=[pltpu.VMEM((2 * (SUB // HEAD_DIM), tm, HEAD_DIM), F32)],
        compiler_params=_cparams(("parallel",)),
        name="qkv_proj",
    )(*args)


A_TQ = 128
A_TK = A_TQ + 2 * A_HALF


def _attn_a_kernel(q_ref, k_ref, v_ref, o_ref, lse_ref, *, length):
    nt = q_ref.shape[0] // A_TQ
    tiles_per_stream = length // A_TQ
    rows = lax.broadcasted_iota(jnp.int32, (A_TQ, A_TK), 0)
    cols = lax.broadcasted_iota(jnp.int32, (A_TQ, A_TK), 1)

    def body(i, carry):
        q0 = pl.multiple_of(i * A_TQ, A_TQ)
        stream0 = (i // tiles_per_stream) * length
        k0 = pl.multiple_of(jnp.clip(q0 - A_HALF, stream0, stream0 + length - A_TK), 16)
        valid = jnp.abs(rows - cols + (q0 - k0)) <= A_HALF
        for hh in range(2):
            sl = slice(hh * HEAD_DIM, (hh + 1) * HEAD_DIM)
            q = q_ref[pl.ds(q0, A_TQ), sl]
            k = k_ref[pl.ds(k0, A_TK), sl]
            v = v_ref[pl.ds(k0, A_TK), sl]
            s = lax.dot_general(q, k, (((1,), (1,)), ((), ())), preferred_element_type=F32)
            s = jnp.where(valid, s, NEG_INF)
            m = jnp.max(s, axis=1, keepdims=True)
            e = jnp.exp2(s - m)
            den = jnp.sum(e, axis=1, keepdims=True)
            o = jnp.dot(e.astype(BF16), v, preferred_element_type=F32) / den
            o_ref[pl.ds(q0, A_TQ), sl] = o.astype(o_ref.dtype)
            lse_ref[pl.ds(q0, A_TQ), sl] = jnp.broadcast_to(m + jnp.log2(den), (A_TQ, HEAD_DIM))
        return carry

    lax.fori_loop(0, nt, body, 0, unroll=min(nt, 4))


def _attn_a_group(src, gi, batch, seq):
    dil = DIL_GROUPS[gi][1]
    length = seq // dil
    if dil == 1:
        cols = (OFF_AQ // SUB + gi, OFF_AK // SUB + gi, OFF_AV // SUB + gi)
    else:
        src = src.reshape(batch * seq, 3 * SUB)
        cols = (0, 1, 2)
    out_spec = pl.BlockSpec((seq, SUB), lambda b: (b, 0))
    o, lse = pl.pallas_call(
        functools.partial(_attn_a_kernel, length=length),
        grid=(batch,),
        in_specs=[pl.BlockSpec((seq, SUB), functools.partial(lambda c, b: (b, c), c)) for c in cols],
        out_specs=[out_spec, out_spec],
        out_shape=[jax.ShapeDtypeStruct((batch * seq, SUB), BF16),
                   jax.ShapeDtypeStruct((batch * seq, SUB), F32)],
        compiler_params=_cparams(("parallel",)),
        name=f"attn_a{gi}",
    )(src, src, src)
    if dil == 1:
        return o, lse
    return o.reshape(batch, dil, length, SUB), lse.reshape(batch, dil, length, SUB)


FLASH_BLOCK = 256


def _transpose_values(v_ref, vt_scr, tk):
    for t in range(vt_scr.shape[0]):
        vt_scr[t] = v_ref[t * tk:(t + 1) * tk, :].astype(F32).T.astype(vt_scr.dtype)


N_SCORE_BUFS = 3


def _flash_scratch(seq, tk, rows):
    return ([pltpu.VMEM((seq // tk, HEAD_DIM, tk), BF16),
             pltpu.VMEM((N_SCORE_BUFS, tk, rows), F32),
             pltpu.VMEM((2, tk, rows), BF16),
             pltpu.VMEM((HEAD_DIM, rows), F32)])


def _flash_keymajor(q, k_ref, vt_scr, s_scr, p_scr, acc_scr, tk):
    rows = q.shape[0]
    nk = vt_scr.shape[0]
    q_blocks = [slice(c, c + FLASH_BLOCK) for c in range(0, rows, FLASH_BLOCK)]
    k_chunks = [slice(c, c + FLASH_BLOCK) for c in range(0, tk, FLASH_BLOCK)]
    bufs = [s_scr.at[b] for b in range(N_SCORE_BUFS)]

    def scores(t, qb, dst):
        k = k_ref[t * tk:(t + 1) * tk, :]
        dst[:, qb] = lax.dot_general(k, q[qb, :], (((1,), (1,)), ((), ())), preferred_element_type=F32)

    def update(m, l, src, t, qb):
        m_new = m
        for kc in k_chunks:
            m_new = jnp.maximum(m_new, jnp.max(src[kc, qb], axis=0, keepdims=True))
        alpha = jnp.exp2(m - m_new)
        l_tile = None
        p_buf = p_scr.at[t % 2]
        for kc in k_chunks:
            p = jnp.exp2(src[kc, qb] - m_new)
            p_sum = jnp.sum(p, axis=0, keepdims=True)
            l_tile = p_sum if l_tile is None else l_tile + p_sum
            p_buf[kc, qb] = p.astype(BF16)
        pv = jnp.dot(vt_scr[t], p_buf[:, qb], preferred_element_type=F32)
        acc_scr[:, qb] = pv if t == 0 else alpha * acc_scr[:, qb] + pv
        return m_new, alpha * l + l_tile

    m = [jnp.full((1, FLASH_BLOCK), NEG_INF, F32) for _ in q_blocks]
    l = [jnp.zeros((1, FLASH_BLOCK), F32) for _ in q_blocks]
    for qb in q_blocks:
        scores(0, qb, bufs[0])
    for t in range(nk):
        for bi, qb in enumerate(q_blocks):
            if t + 1 < nk:
                scores(t + 1, qb, bufs[(t + 1) % N_SCORE_BUFS])
            m[bi], l[bi] = update(m[bi], l[bi], bufs[t % N_SCORE_BUFS], t, qb)
    return acc_scr[...], jnp.concatenate(l, axis=1)


def _attn_b_kernel(q_ref, k_ref, v_ref, o_ref, vt_scr, s_scr, p_scr, acc_scr, *, tq, tk):
    @pl.when(pl.program_id(2) == 0)
    def _():
        _transpose_values(v_ref, vt_scr, tk)

    q = jnp.concatenate([q_ref[:, h * HEAD_DIM:(h + 1) * HEAD_DIM] for h in range(B_GROUP)], axis=0)
    acc, l = _flash_keymajor(q, k_ref, vt_scr, s_scr, p_scr, acc_scr, tk)
    o = acc * (1.0 / l)
    for h in range(B_GROUP):
        o_ref[:, h * HEAD_DIM:(h + 1) * HEAD_DIM] = o[:, h * tq:(h + 1) * tq].T.astype(o_ref.dtype)


def _attn_b(qkv, batch, seq, tq=512, tk=1024):
    nq = seq // tq
    gw = B_GROUP * HEAD_DIM
    rows = B_GROUP * tq
    return pl.pallas_call(
        functools.partial(_attn_b_kernel, tq=tq, tk=tk),
        grid=(batch, B_KV_HEADS, nq),
        in_specs=[pl.BlockSpec((tq, gw), lambda b, kh, i: (b * nq + i, OFF_BQ // gw + kh)),
                  pl.BlockSpec((seq, HEAD_DIM), lambda b, kh, i: (b, OFF_BK // HEAD_DIM + kh)),
                  pl.BlockSpec((seq, HEAD_DIM), lambda b, kh, i: (b, OFF_BV // HEAD_DIM + kh))],
        out_specs=pl.BlockSpec((tq, gw), lambda b, kh, i: (b * nq + i, kh)),
        out_shape=jax.ShapeDtypeStruct((batch * seq, B_QW), BF16),
        scratch_shapes=_flash_scratch(seq, tk, rows),
        compiler_params=_cparams(("parallel", "parallel", "arbitrary")),
        name="attn_b",
    )(qkv, qkv, qkv)


def _attn_c_kernel(lam_ref, g_ref, q_ref, k_ref, v_ref, o_ref, vt_scr, s_scr, p_scr, acc_scr,
                   *, tq, tk, lam_init):
    @pl.when(pl.program_id(2) == 0)
    def _():
        _transpose_values(v_ref, vt_scr, tk)

    lamv = lam_ref[...]
    lam = (jnp.exp(jnp.sum(lamv[0:1] * lamv[1:2], axis=1, keepdims=True))
           - jnp.exp(jnp.sum(lamv[2:3] * lamv[3:4], axis=1, keepdims=True)) + lam_init)
    q = q_ref[...]
    lane = lax.broadcasted_iota(jnp.int32, q.shape, 1)
    zero = jnp.zeros_like(q)
    q2 = jnp.concatenate([jnp.where(lane < C_QK_DIM, q, zero), jnp.where(lane >= C_QK_DIM, q, zero)], axis=0)
    acc, l = _flash_keymajor(q2, k_ref, vt_scr, s_scr, p_scr, acc_scr, tk)
    o = acc * (1.0 / l)
    oc = (o[:, :tq] - lam * o[:, tq:]).T
    o_ref[...] = (_rms_rows(oc, g_ref[...]) * (1.0 - lam_init)).astype(o_ref.dtype)


def _attn_c(qkv, c_lambda, c_head_norm, lam_init, batch, seq, tq=1024, tk=1024):
    nq = seq // tq
    rows = 2 * tq
    return pl.pallas_call(
        functools.partial(_attn_c_kernel, tq=tq, tk=tk, lam_init=lam_init),
        grid=(batch, C_HEADS, nq),
        in_specs=[pl.BlockSpec((4, C_QK_DIM), lambda b, h, i: (0, 0)),
                  pl.BlockSpec((1, HEAD_DIM), lambda b, h, i: (0, 0)),
                  pl.BlockSpec((tq, HEAD_DIM), lambda b, h, i: (b * nq + i, OFF_CQ // HEAD_DIM + h)),
                  pl.BlockSpec((seq, HEAD_DIM), lambda b, h, i: (b, OFF_CK // HEAD_DIM + h)),
                  pl.BlockSpec((seq, HEAD_DIM), lambda b, h, i: (b, OFF_CV // HEAD_DIM + h))],
        out_specs=pl.BlockSpec((tq, HEAD_DIM), lambda b, h, i: (b * nq + i, h)),
        out_shape=jax.ShapeDtypeStruct((batch * seq, C_VW), BF16),
        scratch_shapes=_flash_scratch(seq, tk, rows),
        compiler_params=_cparams(("parallel", "parallel", "arbitrary")),
        name="attn_c",
    )(c_lambda, c_head_norm.reshape(1, HEAD_DIM), qkv, qkv, qkv)


def _merge_kernel(h_ref, wg0, wg1, wg2, oa0, oa1, oa2, ls0, ls1, ls2, ob, oc, wa, wb, wc, out_ref,
                  oa_scr, o1_scr, o2_scr, l1_scr, l2_scr, *, tm):
    def mix_groups(c, rows):
        for g, o_src, l_src, o_dst, l_dst in ((1, oa1, ls1, o1_scr, l1_scr), (2, oa2, ls2, o2_scr, l2_scr)):
            dil = DIL_GROUPS[g][1]
            nu = ROW_CHUNK // dil
            for r in range(dil):
                dst_rows = pl.ds(rows.start + r, nu, stride=dil)
                for hh in range(SUB // HEAD_DIM):
                    sl = slice(hh * HEAD_DIM, (hh + 1) * HEAD_DIM)
                    o_dst[hh, dst_rows, :] = o_src[r, c * nu:(c + 1) * nu, sl].astype(F32)
                    l_dst[hh, dst_rows, :] = l_src[r, c * nu:(c + 1) * nu, sl]
        for hh in range(SUB // HEAD_DIM):
            sl = slice(hh * HEAD_DIM, (hh + 1) * HEAD_DIM)
            l0, l1, l2 = ls0[rows, sl], l1_scr[hh, rows, :], l2_scr[hh, rows, :]
            m = jnp.maximum(jnp.maximum(l0, l1), l2)
            e0, e1, e2 = jnp.exp2(l0 - m), jnp.exp2(l1 - m), jnp.exp2(l2 - m)
            num = e0 * oa0[rows, sl].astype(F32) + e1 * o1_scr[hh, rows, :] + e2 * o2_scr[hh, rows, :]
            oa_scr[rows, sl] = (num / (e0 + e1 + e2)).astype(oa_scr.dtype)

    def project(rows):
        hr = h_ref[rows, :]
        merged = None
        for wg, o_br, w_br in ((wg0, oa_scr, wa), (wg1, ob, wb), (wg2, oc, wc)):
            gate = jax.nn.sigmoid(jnp.dot(hr, wg[...], preferred_element_type=F32))
            term = gate * jnp.dot(o_br[rows, :], w_br[...], preferred_element_type=F32)
            merged = term if merged is None else merged + term
        out_ref[rows, :] = merged.astype(out_ref.dtype)

    j = pl.program_id(1)

    @pl.when(j == 0)
    def _():
        for c, rows in enumerate(_row_chunks(tm)):
            mix_groups(c, rows)
            project(rows)

    @pl.when(j > 0)
    def _():
        for rows in _row_chunks(tm):
            project(rows)


def _merge(h, w_in, li, oa, lse, ob, oc, wa, wb, wc, seq, tm=1024, tn=512):
    t = ob.shape[0]
    nj = D_MODEL // tn
    nseq = seq // tm

    def row(w):
        return pl.BlockSpec((tm, w), lambda i, j: (i, 0))

    def stream(g):
        dil = DIL_GROUPS[g][1]
        return pl.BlockSpec((None, dil, tm // dil, SUB), lambda i, j: (i // nseq, 0, i % nseq, 0))

    def gate_w(br):
        return pl.BlockSpec((None, D_MODEL, tn),
                            lambda i, j: (li, 0, (QKV_COLS + br * D_MODEL) // tn + j))

    def wspec(k):
        return pl.BlockSpec((None, k, tn), lambda i, j: (li, 0, j))

    return pl.pallas_call(
        functools.partial(_merge_kernel, tm=tm),
        grid=(t // tm, nj),
        in_specs=[row(D_MODEL), gate_w(0), gate_w(1), gate_w(2),
                  row(SUB), stream(1), stream(2), row(SUB), stream(1), stream(2),
                  row(B_QW), row(C_VW), wspec(SUB), wspec(B_QW), wspec(C_VW)],
        out_specs=pl.BlockSpec((tm, tn), lambda i, j: (i, j)),
        out_shape=jax.ShapeDtypeStruct((t, D_MODEL), BF16),
        scratch_shapes=[pltpu.VMEM((tm, SUB), BF16)] + [pltpu.VMEM((SUB // HEAD_DIM, tm, HEAD_DIM), F32)] * 4,
        compiler_params=_cparams(("parallel", "arbitrary")),
        name="merge",
    )(h, w_in, w_in, w_in, *oa, *lse, ob, oc, wa, wb, wc)


def _out_kernel(m_ref, w_ref, x_ref, gpost_ref, gnext_ref, x1_ref, h_ref):
    for rows in _row_chunks(m_ref.shape[0]):
        y = jnp.dot(m_ref[rows, :], w_ref[...], preferred_element_type=F32)
        x1 = x_ref[rows, :] + _rms_rows(y, gpost_ref[...])
        x1_ref[rows, :] = x1
        h_ref[rows, :] = _rms_rows(x1, gnext_ref[...]).astype(h_ref.dtype)


def _out_proj(merged, w_out, li, x, g_post, g_next, tm=512):
    t = x.shape[0]
    row = lambda i: (i, 0)
    const = lambda i: (0, 0)
    return pl.pallas_call(
        _out_kernel,
        grid=(t // tm,),
        in_specs=[pl.BlockSpec((tm, D_MODEL), row),
                  pl.BlockSpec((None, D_MODEL, D_MODEL), lambda i: (li, 0, 0)),
                  pl.BlockSpec((tm, D_MODEL), row),
                  pl.BlockSpec((1, D_MODEL), const),
                  pl.BlockSpec((1, D_MODEL), const)],
        out_specs=[pl.BlockSpec((tm, D_MODEL), row), pl.BlockSpec((tm, D_MODEL), row)],
        out_shape=[jax.ShapeDtypeStruct((t, D_MODEL), F32), jax.ShapeDtypeStruct((t, D_MODEL), BF16)],
        compiler_params=_cparams(("parallel",)),
        name="out_proj",
    )(merged, w_out, x, g_post.reshape(1, D_MODEL), g_next.reshape(1, D_MODEL))


def _mlp_kernel(h_ref, w1_ref, w2_ref, x_ref, gpost_ref, *rest, with_next):
    if with_next:
        gnext_ref, x2_ref, hn_ref, acc_ref = rest
    else:
        x2_ref, acc_ref = rest
    f = pl.program_id(1)

    @pl.when(f == 0)
    def _():
        acc_ref[...] = jnp.zeros_like(acc_ref)

    def partial_out(rows):
        u = jnp.dot(h_ref[rows, :], w1_ref[...], preferred_element_type=F32)
        u = jnp.square(jnp.maximum(u, 0.0)).astype(BF16)
        return acc_ref[rows, :] + jnp.dot(u, w2_ref[...], preferred_element_type=F32)

    last = pl.num_programs(1) - 1

    @pl.when(f < last)
    def _():
        acc_ref[...] = partial_out(slice(None))

    @pl.when(f == last)
    def _():
        for rows in _row_chunks(h_ref.shape[0]):
            x2 = x_ref[rows, :] + _rms_rows(partial_out(rows), gpost_ref[...])
            x2_ref[rows, :] = x2
            if with_next:
                hn_ref[rows, :] = _rms_rows(x2, gnext_ref[...]).astype(hn_ref.dtype)


def _mlp(h, w1, w2, li, x, g_post, g_next, tm=512, tf=1024):
    t = x.shape[0]
    with_next = g_next is not None
    row = lambda i, f: (i, 0)
    const = lambda i, f: (0, 0)
    in_specs = [pl.BlockSpec((tm, D_MODEL), row),
                pl.BlockSpec((None, D_MODEL, tf), lambda i, f: (li, 0, f)),
                pl.BlockSpec((None, tf, D_MODEL), lambda i, f: (li, f, 0)),
                pl.BlockSpec((tm, D_MODEL), row),
                pl.BlockSpec((1, D_MODEL), const)]
    args = [h, w1, w2, x, g_post.reshape(1, D_MODEL)]
    out_specs = [pl.BlockSpec((tm, D_MODEL), row)]
    out_shape = [jax.ShapeDtypeStruct((t, D_MODEL), F32)]
    if with_next:
        in_specs.append(pl.BlockSpec((1, D_MODEL), const))
        args.append(g_next.reshape(1, D_MODEL))
        out_specs.append(pl.BlockSpec((tm, D_MODEL), row))
        out_shape.append(jax.ShapeDtypeStruct((t, D_MODEL), BF16))
    res = pl.pallas_call(
        functools.partial(_mlp_kernel, with_next=with_next),
        grid=(t // tm, D_FF // tf),
        in_specs=in_specs,
        out_specs=out_specs,
        out_shape=out_shape,
        scratch_shapes=[pltpu.VMEM((tm, D_MODEL), F32)],
        compiler_params=_cparams(("parallel", "arbitrary")),
        name="mlp",
    )(*args)
    return (res[0], res[1]) if with_next else (res[0], None)


def _rope_tables(pos, dim):
    inv = 1.0 / (ROPE_THETA ** (jnp.arange(0, dim, 2, dtype=F32) / dim))
    ang = pos.astype(F32)[:, None] * inv[None, :]
    return jnp.cos(ang), jnp.sin(ang)


def _tables(seq):
    pos = jnp.arange(seq)
    ca, sa = _rope_tables(pos, HEAD_DIM)
    cos_a = jnp.concatenate([ca, ca], axis=1)
    sin_a = jnp.concatenate([-sa, sa], axis=1)
    z32 = jnp.zeros((seq, 32), F32)

    def half32(c, s):
        return jnp.concatenate([c, c], 1), jnp.concatenate([-s, z32], 1), jnp.concatenate([z32, s], 1)

    cr, sr = _rope_tables(pos // GRID_W, AXIAL_DIM)
    cc, sc = _rope_tables(pos % GRID_W, AXIAL_DIM)
    r_c, r_lo, r_hi = half32(cr, sr)
    c_c, c_lo, c_hi = half32(cc, sc)
    cos_b = jnp.concatenate([r_c, c_c], 1)
    sin_b_lo = jnp.concatenate([r_lo, c_lo], 1)
    sin_b_hi = jnp.concatenate([r_hi, c_hi], 1)
    c3, s3 = _rope_tables(pos, C_QK_DIM)
    m_c, m_lo, m_hi = half32(c3, s3)
    cos_c = jnp.concatenate([m_c, m_c], 1)
    sin_c_lo = jnp.concatenate([m_lo, m_lo], 1)
    sin_c_hi = jnp.concatenate([m_hi, m_hi], 1)
    return jnp.stack([cos_a, sin_a, cos_b, sin_b_lo, sin_b_hi, cos_c, sin_c_lo, sin_c_hi], axis=0)


def _trunk(x3, tabs, p):
    batch, seq, _ = x3.shape
    x = x3.reshape(batch * seq, D_MODEL)
    h = None
    for li in range(DEPTH):
        lam_init = 0.8 - 0.6 * math.exp(-0.3 * li)
        if li == 0:
            qkv, st1, st2, h = _qkv_proj(x, p["w_in"], li, tabs, p["b_q_norm"][li], p["b_k_norm"][li],
                                         batch, seq, g_pre=p["g_mix_pre"][li])
        else:
            qkv, st1, st2 = _qkv_proj(h, p["w_in"], li, tabs, p["b_q_norm"][li], p["b_k_norm"][li],
                                      batch, seq)
        oa, lse = zip(*[_attn_a_group(src, gi, batch, seq) for gi, src in enumerate((qkv, st1, st2))])
        ob = _attn_b(qkv, batch, seq)
        oc = _attn_c(qkv, p["c_lambda"][li], p["c_head_norm"][li], lam_init, batch, seq)
        merged = _merge(h, p["w_in"], li, oa, lse, ob, oc,
                        p["w_branch_a"], p["w_branch_b"], p["w_branch_c"], seq)
        x, hm = _out_proj(merged, p["w_out"], li, x, p["g_mix_post"][li], p["g_mlp_pre"][li])
        g_next = p["g_mix_pre"][li + 1] if li + 1 < DEPTH else None
        x, h = _mlp(hm, p["w_mlp_in"], p["w_mlp_out"], li, x, p["g_mlp_post"][li], g_next)
    return x.reshape(batch, seq, D_MODEL)


def kernel(x_prompt, x_sample, g_mix_pre, w_in, b_q_norm, b_k_norm, c_lambda, c_head_norm,
           w_branch_a, w_branch_b, w_branch_c, w_out, g_mix_post,
           g_mlp_pre, w_mlp_in, w_mlp_out, g_mlp_post):
    p = dict(g_mix_pre=g_mix_pre, w_in=w_in.astype(BF16), b_q_norm=b_q_norm, b_k_norm=b_k_norm,
             c_lambda=c_lambda, c_head_norm=c_head_norm,
             w_branch_a=w_branch_a.astype(BF16), w_branch_b=w_branch_b.astype(BF16),
             w_branch_c=w_branch_c.astype(BF16), w_out=w_out.astype(BF16), g_mix_post=g_mix_post,
             g_mlp_pre=g_mlp_pre, w_mlp_in=w_mlp_in.astype(BF16), w_mlp_out=w_mlp_out.astype(BF16),
             g_mlp_post=g_mlp_post)
    tabs = _tables(max(x_prompt.shape[1], x_sample.shape[1]))
    return (_trunk(x_prompt, tabs, p), _trunk(x_sample, tabs, p))
```

```python
import functools
import math

import jax
import jax.numpy as jnp
from jax import lax
from jax.experimental import pallas as pl
from jax.experimental.pallas import tpu as pltpu

F32 = jnp.float32
BF16 = jnp.bfloat16

D_MODEL = 2048
DEPTH = 2
HEAD_DIM = 128
GRID_W = 64
ROPE_THETA = 10000.0
EPS = 1e-6
NEG_INF = -1e30
LOG2E = math.log2(math.e)

DIL_GROUPS = ((128, 1), (512, 4), (2048, 16))
A_HALF = 64
A_W = 768
B_Q_HEADS = 6
B_KV_HEADS = 2
B_GROUP = B_Q_HEADS // B_KV_HEADS
B_QW = 768
B_KVW = 256
AXIAL_DIM = 64
C_HEADS = 4
C_QK_DIM = 64
C_QW = 512
C_VW = 512
N_BRANCH = 3
QKV_COLS = 3 * A_W + B_QW + 2 * B_KVW + 2 * C_QW + C_VW
GATE_COLS = N_BRANCH * D_MODEL
D_FF = 4 * D_MODEL

OFF_AQ, OFF_AK, OFF_AV = 0, A_W, 2 * A_W
OFF_BQ = 3 * A_W
OFF_BK, OFF_BV = OFF_BQ + B_QW, OFF_BQ + B_QW + B_KVW
OFF_CQ = OFF_BV + B_KVW
OFF_CK, OFF_CV = OFF_CQ + C_QW, OFF_CQ + 2 * C_QW

SUB = 2 * HEAD_DIM
_QKV_KINDS = (("aq",) * 3 + ("ak",) * 3 + ("v",) * 3 + ("bq",) * 3 + ("bk",) + ("v",)
              + ("cq",) * 2 + ("ck",) * 2 + ("v",) * 2)
N_TABLES = 8

V7X_VMEM_BYTES = 64 * 1024 * 1024
VMEM_LIMIT = V7X_VMEM_BYTES - 8 * 1024 * 1024


def _cparams(sem):
    return pltpu.CompilerParams(dimension_semantics=sem, vmem_limit_bytes=VMEM_LIMIT)


def _rms_rows(xf, g):
    ms = jnp.mean(xf * xf, axis=-1, keepdims=True)
    return xf * lax.rsqrt(ms + EPS) * g


def _rope_half64(x, cos, sin_signed):
    return x * cos + pltpu.roll(x, 64, axis=1) * sin_signed


def _rope_half32(x, cos, sin_lo, sin_hi):
    return x * cos + pltpu.roll(x, 96, axis=1) * sin_lo + pltpu.roll(x, 32, axis=1) * sin_hi


_STREAM_TILES = {1: (1, 4, 7), 2: (2, 5, 8)}


ROW_CHUNK = 256


def _row_chunks(n):
    return [slice(r, r + ROW_CHUNK) for r in range(0, n, ROW_CHUNK)]


def _qkv_kernel(*refs, tm, norm_input):
    if norm_input:
        x_ref, gpre_ref, w_ref, tab_ref, gq_ref, gk_ref, o_ref, s1_ref, s2_ref, h_ref, epi_scr = refs
        for rows in _row_chunks(tm):
            h_ref[rows, :] = _rms_rows(x_ref[rows, :], gpre_ref[...]).astype(h_ref.dtype)
    else:
        h_ref, w_ref, tab_ref, gq_ref, gk_ref, o_ref, s1_ref, s2_ref, epi_scr = refs

    def head_epilogue(x, kind, rows):
        if kind in ("aq", "ak"):
            y = _rope_half64(x, tab_ref[0, rows, :], tab_ref[1, rows, :])
            return y * (HEAD_DIM ** -0.5 * LOG2E) if kind == "aq" else y
        if kind in ("bq", "bk"):
            g = gq_ref[...] * (HEAD_DIM ** -0.5 * LOG2E) if kind == "bq" else gk_ref[...]
            return _rope_half32(_rms_rows(x, g), tab_ref[2, rows, :], tab_ref[3, rows, :], tab_ref[4, rows, :])
        if kind in ("cq", "ck"):
            y = _rope_half32(x, tab_ref[5, rows, :], tab_ref[6, rows, :], tab_ref[7, rows, :])
            return y * (C_QK_DIM ** -0.5 * LOG2E) if kind == "cq" else y
        return x

    n_heads = SUB // HEAD_DIM
    for jj, kind in enumerate(_QKV_KINDS):
        g = next((g for g in (1, 2) if jj in _STREAM_TILES[g]), None)
        for c, rows in enumerate(_row_chunks(tm)):
            acc = jnp.dot(h_ref[rows, :], w_ref[:, jj * SUB:(jj + 1) * SUB], preferred_element_type=F32)
            for hh in range(n_heads):
                sl = slice(hh * HEAD_DIM, (hh + 1) * HEAD_DIM)
                y = head_epilogue(acc[:, sl], kind, rows)
                col = jj * SUB + hh * HEAD_DIM
                o_ref[rows, col:col + HEAD_DIM] = y.astype(o_ref.dtype)
                if g is not None:
                    ref = (s1_ref, s2_ref)[g - 1]
                    dil = DIL_GROUPS[g][1]
                    nu = ROW_CHUNK // dil
                    slot = (jj % 2) * n_heads + hh
                    col0 = _STREAM_TILES[g].index(jj) * SUB + hh * HEAD_DIM
                    epi_scr[slot, rows, :] = y
                    for r in range(dil):
                        rows_r = epi_scr[slot, pl.ds(rows.start + r, nu, stride=dil), :]
                        ref[r, c * nu:(c + 1) * nu, col0:col0 + HEAD_DIM] = rows_r.astype(ref.dtype)


def _qkv_proj(h, w_in, li, tabs, gq, gk, batch, seq, g_pre=None, tm=512):
    t = h.shape[0]
    nseq = seq // tm
    d1, d2 = DIL_GROUPS[1][1], DIL_GROUPS[2][1]
    norm_input = g_pre is not None
    row = pl.BlockSpec((tm, D_MODEL), lambda i: (i, 0))
    vec = lambda n: pl.BlockSpec((1, n), lambda i: (0, 0))

    def stream_spec(dil):
        return pl.BlockSpec((None, dil, tm // dil, 3 * SUB), lambda i: (i // nseq, 0, i % nseq, 0))

    in_specs = [pl.BlockSpec((None, D_MODEL, QKV_COLS), lambda i: (li, 0, 0), pipeline_mode=pl.Buffered(1)),
                pl.BlockSpec((N_TABLES, tm, HEAD_DIM), lambda i: (0, i % nseq, 0)),
                vec(HEAD_DIM), vec(HEAD_DIM)]
    args = [w_in, tabs, gq.reshape(1, HEAD_DIM), gk.reshape(1, HEAD_DIM)]
    out_specs = [pl.BlockSpec((tm, QKV_COLS), lambda i: (i, 0)), stream_spec(d1), stream_spec(d2)]
    out_shape = [jax.ShapeDtypeStruct((t, QKV_COLS), BF16),
                 jax.ShapeDtypeStruct((batch, d1, seq // d1, 3 * SUB), BF16),
                 jax.ShapeDtypeStruct((batch, d2, seq // d2, 3 * SUB), BF16)]
    if norm_input:
        in_specs = [row, vec(D_MODEL)] + in_specs
        args = [h, g_pre.reshape(1, D_MODEL)] + args
        out_specs.append(row)
        out_shape.append(jax.ShapeDtypeStruct((t, D_MODEL), BF16))
    else:
        in_specs = [row] + in_specs
        args = [h] + args
    return pl.pallas_call(
        functools.partial(_qkv_kernel, tm=tm, norm_input=norm_input),
        grid=(t // tm,),
        in_specs=in_specs,
        out_specs=out_specs,
        out_shape=out_shape,
        scratch_shapes=[pltpu.VMEM((2 * (SUB // HEAD_DIM), tm, HEAD_DIM), F32)],
        compiler_params=_cparams(("parallel",)),
        name="qkv_proj",
    )(*args)


A_TQ = 128
A_TK = A_TQ + 2 * A_HALF


def _attn_a_kernel(q_ref, k_ref, v_ref, o_ref, lse_ref, *, length):
    nt = q_ref.shape[0] // A_TQ
    tiles_per_stream = length // A_TQ
    rows = lax.broadcasted_iota(jnp.int32, (A_TQ, A_TK), 0)
    cols = lax.broadcasted_iota(jnp.int32, (A_TQ, A_TK), 1)

    def body(i, carry):
        q0 = pl.multiple_of(i * A_TQ, A_TQ)
        stream0 = (i // tiles_per_stream) * length
        k0 = pl.multiple_of(jnp.clip(q0 - A_HALF, stream0, stream0 + length - A_TK), 16)
        valid = jnp.abs(rows - cols + (q0 - k0)) <= A_HALF
        for hh in range(2):
            sl = slice(hh * HEAD_DIM, (hh + 1) * HEAD_DIM)
            q = q_ref[pl.ds(q0, A_TQ), sl]
            k = k_ref[pl.ds(k0, A_TK), sl]
            v = v_ref[pl.ds(k0, A_TK), sl]
            s = lax.dot_general(q, k, (((1,), (1,)), ((), ())), preferred_element_type=F32)
            s = jnp.where(valid, s, NEG_INF)
            m = jnp.max(s, axis=1, keepdims=True)
            e = jnp.exp2(s - m)
            den = jnp.sum(e, axis=1, keepdims=True)
            o = jnp.dot(e.astype(BF16), v, preferred_element_type=F32) / den
            o_ref[pl.ds(q0, A_TQ), sl] = o.astype(o_ref.dtype)
            lse_ref[pl.ds(q0, A_TQ), sl] = jnp.broadcast_to(m + jnp.log2(den), (A_TQ, HEAD_DIM))
        return carry

    lax.fori_loop(0, nt, body, 0, unroll=min(nt, 4))


def _attn_a_group(src, gi, batch, seq):
    dil = DIL_GROUPS[gi][1]
    length = seq // dil
    if dil == 1:
        cols = (OFF_AQ // SUB + gi, OFF_AK // SUB + gi, OFF_AV // SUB + gi)
    else:
        src = src.reshape(batch * seq, 3 * SUB)
        cols = (0, 1, 2)
    out_spec = pl.BlockSpec((seq, SUB), lambda b: (b, 0))
    o, lse = pl.pallas_call(
        functools.partial(_attn_a_kernel, length=length),
        grid=(batch,),
        in_specs=[pl.BlockSpec((seq, SUB), functools.partial(lambda c, b: (b, c), c)) for c in cols],
        out_specs=[out_spec, out_spec],
        out_shape=[jax.ShapeDtypeStruct((batch * seq, SUB), BF16),
                   jax.ShapeDtypeStruct((batch * seq, SUB), F32)],
        compiler_params=_cparams(("parallel",)),
        name=f"attn_a{gi}",
    )(src, src, src)
    if dil == 1:
        return o, lse
    return o.reshape(batch, dil, length, SUB), lse.reshape(batch, dil, length, SUB)


FLASH_BLOCK = 256


def _transpose_values(v_ref, vt_scr, tk):
    for t in range(vt_scr.shape[0]):
        vt_scr[t] = v_ref[t * tk:(t + 1) * tk, :].astype(F32).T.astype(vt_scr.dtype)


N_SCORE_BUFS = 3


def _flash_scratch(seq, tk, rows):
    return ([pltpu.VMEM((seq // tk, HEAD_DIM, tk), BF16),
             pltpu.VMEM((N_SCORE_BUFS, tk, rows), F32),
             pltpu.VMEM((2, tk, rows), BF16),
             pltpu.VMEM((HEAD_DIM, rows), F32)])


def _flash_keymajor(q, k_ref, vt_scr, s_scr, p_scr, acc_scr, tk):
    rows = q.shape[0]
    nk = vt_scr.shape[0]
    q_blocks = [slice(c, c + FLASH_BLOCK) for c in range(0, rows, FLASH_BLOCK)]
    k_chunks = [slice(c, c + FLASH_BLOCK) for c in range(0, tk, FLASH_BLOCK)]
    bufs = [s_scr.at[b] for b in range(N_SCORE_BUFS)]

    def scores(t, qb, dst):
        k = k_ref[t * tk:(t + 1) * tk, :]
        dst[:, qb] = lax.dot_general(k, q[qb, :], (((1,), (1,)), ((), ())), preferred_element_type=F32)

    def update(m, l, src, t, qb):
        m_new = m
        for kc in k_chunks:
            m_new = jnp.maximum(m_new, jnp.max(src[kc, qb], axis=0, keepdims=True))
        alpha = jnp.exp2(m - m_new)
        l_tile = None
        p_buf = p_scr.at[t % 2]
        for kc in k_chunks:
            p = jnp.exp2(src[kc, qb] - m_new)
            p_sum = jnp.sum(p, axis=0, keepdims=True)
            l_tile = p_sum if l_tile is None else l_tile + p_sum
            p_buf[kc, qb] = p.astype(BF16)
        pv = jnp.dot(vt_scr[t], p_buf[:, qb], preferred_element_type=F32)
        acc_scr[:, qb] = pv if t == 0 else alpha * acc_scr[:, qb] + pv
        return m_new, alpha * l + l_tile

    m = [jnp.full((1, FLASH_BLOCK), NEG_INF, F32) for _ in q_blocks]
    l = [jnp.zeros((1, FLASH_BLOCK), F32) for _ in q_blocks]
    for qb in q_blocks:
        scores(0, qb, bufs[0])
    for t in range(nk):
        for bi, qb in enumerate(q_blocks):
            if t + 1 < nk:
                scores(t + 1, qb, bufs[(t + 1) % N_SCORE_BUFS])
            m[bi], l[bi] = update(m[bi], l[bi], bufs[t % N_SCORE_BUFS], t, qb)
    return acc_scr[...], jnp.concatenate(l, axis=1)


def _attn_b_kernel(q_ref, k_ref, v_ref, o_ref, vt_scr, s_scr, p_scr, acc_scr, *, tq, tk):
    @pl.when(pl.program_id(2) == 0)
    def _():
        _transpose_values(v_ref, vt_scr, tk)

    q = jnp.concatenate([q_ref[:, h * HEAD_DIM:(h + 1) * HEAD_DIM] for h in range(B_GROUP)], axis=0)
    acc, l = _flash_keymajor(q, k_ref, vt_scr, s_scr, p_scr, acc_scr, tk)
    o = acc * (1.0 / l)
    for h in range(B_GROUP):
        o_ref[:, h * HEAD_DIM:(h + 1) * HEAD_DIM] = o[:, h * tq:(h + 1) * tq].T.astype(o_ref.dtype)


def _attn_b(qkv, batch, seq, tq=512, tk=1024):
    nq = seq // tq
    gw = B_GROUP * HEAD_DIM
    rows = B_GROUP * tq
    return pl.pallas_call(
        functools.partial(_attn_b_kernel, tq=tq, tk=tk),
        grid=(batch, B_KV_HEADS, nq),
        in_specs=[pl.BlockSpec((tq, gw), lambda b, kh, i: (b * nq + i, OFF_BQ // gw + kh)),
                  pl.BlockSpec((seq, HEAD_DIM), lambda b, kh, i: (b, OFF_BK // HEAD_DIM + kh)),
                  pl.BlockSpec((seq, HEAD_DIM), lambda b, kh, i: (b, OFF_BV // HEAD_DIM + kh))],
        out_specs=pl.BlockSpec((tq, gw), lambda b, kh, i: (b * nq + i, kh)),
        out_shape=jax.ShapeDtypeStruct((batch * seq, B_QW), BF16),
        scratch_shapes=_flash_scratch(seq, tk, rows),
        compiler_params=_cparams(("parallel", "parallel", "arbitrary")),
        name="attn_b",
    )(qkv, qkv, qkv)


def _attn_c_kernel(lam_ref, g_ref, q_ref, k_ref, v_ref, o_ref, vt_scr, s_scr, p_scr, acc_scr,
                   *, tq, tk, lam_init):
    @pl.when(pl.program_id(2) == 0)
    def _():
        _transpose_values(v_ref, vt_scr, tk)

    lamv = lam_ref[...]
    lam = (jnp.exp(jnp.sum(lamv[0:1] * lamv[1:2], axis=1, keepdims=True))
           - jnp.exp(jnp.sum(lamv[2:3] * lamv[3:4], axis=1, keepdims=True)) + lam_init)
    q = q_ref[...]
    lane = lax.broadcasted_iota(jnp.int32, q.shape, 1)
    zero = jnp.zeros_like(q)
    q2 = jnp.concatenate([jnp.where(lane < C_QK_DIM, q, zero), jnp.where(lane >= C_QK_DIM, q, zero)], axis=0)
    acc, l = _flash_keymajor(q2, k_ref, vt_scr, s_scr, p_scr, acc_scr, tk)
    o = acc * (1.0 / l)
    oc = (o[:, :tq] - lam * o[:, tq:]).T
    o_ref[...] = (_rms_rows(oc, g_ref[...]) * (1.0 - lam_init)).astype(o_ref.dtype)


def _attn_c(qkv, c_lambda, c_head_norm, lam_init, batch, seq, tq=512, tk=1024):
    nq = seq // tq
    rows = 2 * tq
    return pl.pallas_call(
        functools.partial(_attn_c_kernel, tq=tq, tk=tk, lam_init=lam_init),
        grid=(batch, C_HEADS, nq),
        in_specs=[pl.BlockSpec((4, C_QK_DIM), lambda b, h, i: (0, 0)),
                  pl.BlockSpec((1, HEAD_DIM), lambda b, h, i: (0, 0)),
                  pl.BlockSpec((tq, HEAD_DIM), lambda b, h, i: (b * nq + i, OFF_CQ // HEAD_DIM + h)),
                  pl.BlockSpec((seq, HEAD_DIM), lambda b, h, i: (b, OFF_CK // HEAD_DIM + h)),
                  pl.BlockSpec((seq, HEAD_DIM), lambda b, h, i: (b, OFF_CV // HEAD_DIM + h))],
        out_specs=pl.BlockSpec((tq, HEAD_DIM), lambda b, h, i: (b * nq + i, h)),
        out_shape=jax.ShapeDtypeStruct((batch * seq, C_VW), BF16),
        scratch_shapes=_flash_scratch(seq, tk, rows),
        compiler_params=_cparams(("parallel", "parallel", "arbitrary")),
        name="attn_c",
    )(c_lambda, c_head_norm.reshape(1, HEAD_DIM), qkv, qkv, qkv)


def _merge_kernel(h_ref, wg0, wg1, wg2, oa0, oa1, oa2, ls0, ls1, ls2, ob, oc, wa, wb, wc, out_ref,
                  oa_scr, o1_scr, o2_scr, l1_scr, l2_scr, *, tm):
    def mix_groups(c, rows):
        for g, o_src, l_src, o_dst, l_dst in ((1, oa1, ls1, o1_scr, l1_scr), (2, oa2, ls2, o2_scr, l2_scr)):
            dil = DIL_GROUPS[g][1]
            nu = ROW_CHUNK // dil
            for r in range(dil):
                dst_rows = pl.ds(rows.start + r, nu, stride=dil)
                for hh in range(SUB // HEAD_DIM):
                    sl = slice(hh * HEAD_DIM, (hh + 1) * HEAD_DIM)
                    o_dst[hh, dst_rows, :] = o_src[r, c * nu:(c + 1) * nu, sl].astype(F32)
                    l_dst[hh, dst_rows, :] = l_src[r, c * nu:(c + 1) * nu, sl]
        for hh in range(SUB // HEAD_DIM):
            sl = slice(hh * HEAD_DIM, (hh + 1) * HEAD_DIM)
            l0, l1, l2 = ls0[rows, sl], l1_scr[hh, rows, :], l2_scr[hh, rows, :]
            m = jnp.maximum(jnp.maximum(l0, l1), l2)
            e0, e1, e2 = jnp.exp2(l0 - m), jnp.exp2(l1 - m), jnp.exp2(l2 - m)
            num = e0 * oa0[rows, sl].astype(F32) + e1 * o1_scr[hh, rows, :] + e2 * o2_scr[hh, rows, :]
            oa_scr[rows, sl] = (num / (e0 + e1 + e2)).astype(oa_scr.dtype)

    def project(rows):
        hr = h_ref[rows, :]
        merged = None
        for wg, o_br, w_br in ((wg0, oa_scr, wa), (wg1, ob, wb), (wg2, oc, wc)):
            gate = jax.nn.sigmoid(jnp.dot(hr, wg[...], preferred_element_type=F32))
            term = gate * jnp.dot(o_br[rows, :], w_br[...], preferred_element_type=F32)
            merged = term if merged is None else merged + term
        out_ref[rows, :] = merged.astype(out_ref.dtype)

    j = pl.program_id(1)

    @pl.when(j == 0)
    def _():
        for c, rows in enumerate(_row_chunks(tm)):
            mix_groups(c, rows)
            project(rows)

    @pl.when(j > 0)
    def _():
        for rows in _row_chunks(tm):
            project(rows)


def _merge(h, w_in, li, oa, lse, ob, oc, wa, wb, wc, seq, tm=1024, tn=512):
    t = ob.shape[0]
    nj = D_MODEL // tn
    nseq = seq // tm

    def row(w):
        return pl.BlockSpec((tm, w), lambda i, j: (i, 0))

    def stream(g):
        dil = DIL_GROUPS[g][1]
        return pl.BlockSpec((None, dil, tm // dil, SUB), lambda i, j: (i // nseq, 0, i % nseq, 0))

    def gate_w(br):
        return pl.BlockSpec((None, D_MODEL, tn),
                            lambda i, j: (li, 0, (QKV_COLS + br * D_MODEL) // tn + j))

    def wspec(k):
        return pl.BlockSpec((None, k, tn), lambda i, j: (li, 0, j))

    return pl.pallas_call(
        functools.partial(_merge_kernel, tm=tm),
        grid=(t // tm, nj),
        in_specs=[row(D_MODEL), gate_w(0), gate_w(1), gate_w(2),
                  row(SUB), stream(1), stream(2), row(SUB), stream(1), stream(2),
                  row(B_QW), row(C_VW), wspec(SUB), wspec(B_QW), wspec(C_VW)],
        out_specs=pl.BlockSpec((tm, tn), lambda i, j: (i, j)),
        out_shape=jax.ShapeDtypeStruct((t, D_MODEL), BF16),
        scratch_shapes=[pltpu.VMEM((tm, SUB), BF16)] + [pltpu.VMEM((SUB // HEAD_DIM, tm, HEAD_DIM), F32)] * 4,
        compiler_params=_cparams(("parallel", "arbitrary")),
        name="merge",
    )(h, w_in, w_in, w_in, *oa, *lse, ob, oc, wa, wb, wc)


def _out_kernel(m_ref, w_ref, x_ref, gpost_ref, gnext_ref, x1_ref, h_ref):
    for rows in _row_chunks(m_ref.shape[0]):
        y = jnp.dot(m_ref[rows, :], w_ref[...], preferred_element_type=F32)
        x1 = x_ref[rows, :] + _rms_rows(y, gpost_ref[...])
        x1_ref[rows, :] = x1
        h_ref[rows, :] = _rms_rows(x1, gnext_ref[...]).astype(h_ref.dtype)


def _out_proj(merged, w_out, li, x, g_post, g_next, tm=512):
    t = x.shape[0]
    row = lambda i: (i, 0)
    const = lambda i: (0, 0)
    return pl.pallas_call(
        _out_kernel,
        grid=(t // tm,),
        in_specs=[pl.BlockSpec((tm, D_MODEL), row),
                  pl.BlockSpec((None, D_MODEL, D_MODEL), lambda i: (li, 0, 0)),
                  pl.BlockSpec((tm, D_MODEL), row),
                  pl.BlockSpec((1, D_MODEL), const),
                  pl.BlockSpec((1, D_MODEL), const)],
        out_specs=[pl.BlockSpec((tm, D_MODEL), row), pl.BlockSpec((tm, D_MODEL), row)],
        out_shape=[jax.ShapeDtypeStruct((t, D_MODEL), F32), jax.ShapeDtypeStruct((t, D_MODEL), BF16)],
        compiler_params=_cparams(("parallel",)),
        name="out_proj",
    )(merged, w_out, x, g_post.reshape(1, D_MODEL), g_next.reshape(1, D_MODEL))


def _mlp_kernel(h_ref, w1_ref, w2_ref, x_ref, gpost_ref, *rest, with_next):
    if with_next:
        gnext_ref, x2_ref, hn_ref, acc_ref = rest
    else:
        x2_ref, acc_ref = rest
    f = pl.program_id(1)

    @pl.when(f == 0)
    def _():
        acc_ref[...] = jnp.zeros_like(acc_ref)

    def partial_out(rows):
        u = jnp.dot(h_ref[rows, :], w1_ref[...], preferred_element_type=F32)
        u = jnp.square(jnp.maximum(u, 0.0)).astype(BF16)
        return acc_ref[rows, :] + jnp.dot(u, w2_ref[...], preferred_element_type=F32)

    last = pl.num_programs(1) - 1

    @pl.when(f < last)
    def _():
        acc_ref[...] = partial_out(slice(None))

    @pl.when(f == last)
    def _():
        for rows in _row_chunks(h_ref.shape[0]):
            x2 = x_ref[rows, :] + _rms_rows(partial_out(rows), gpost_ref[...])
            x2_ref[rows, :] = x2
            if with_next:
                hn_ref[rows, :] = _rms_rows(x2, gnext_ref[...]).astype(hn_ref.dtype)


def _mlp(h, w1, w2, li, x, g_post, g_next, tm=512, tf=1024):
    t = x.shape[0]
    with_next = g_next is not None
    row = lambda i, f: (i, 0)
    const = lambda i, f: (0, 0)
    in_specs = [pl.BlockSpec((tm, D_MODEL), row),
                pl.BlockSpec((None, D_MODEL, tf), lambda i, f: (li, 0, f)),
                pl.BlockSpec((None, tf, D_MODEL), lambda i, f: (li, f, 0)),
                pl.BlockSpec((tm, D_MODEL), row),
                pl.BlockSpec((1, D_MODEL), const)]
    args = [h, w1, w2, x, g_post.reshape(1, D_MODEL)]
    out_specs = [pl.BlockSpec((tm, D_MODEL), row)]
    out_shape = [jax.ShapeDtypeStruct((t, D_MODEL), F32)]
    if with_next:
        in_specs.append(pl.BlockSpec((1, D_MODEL), const))
        args.append(g_next.reshape(1, D_MODEL))
        out_specs.append(pl.BlockSpec((tm, D_MODEL), row))
        out_shape.append(jax.ShapeDtypeStruct((t, D_MODEL), BF16))
    res = pl.pallas_call(
        functools.partial(_mlp_kernel, with_next=with_next),
        grid=(t // tm, D_FF // tf),
        in_specs=in_specs,
        out_specs=out_specs,
        out_shape=out_shape,
        scratch_shapes=[pltpu.VMEM((tm, D_MODEL), F32)],
        compiler_params=_cparams(("parallel", "arbitrary")),
        name="mlp",
    )(*args)
    return (res[0], res[1]) if with_next else (res[0], None)


def _rope_tables(pos, dim):
    inv = 1.0 / (ROPE_THETA ** (jnp.arange(0, dim, 2, dtype=F32) / dim))
    ang = pos.astype(F32)[:, None] * inv[None, :]
    return jnp.cos(ang), jnp.sin(ang)


def _tables(seq):
    pos = jnp.arange(seq)
    ca, sa = _rope_tables(pos, HEAD_DIM)
    cos_a = jnp.concatenate([ca, ca], axis=1)
    sin_a = jnp.concatenate([-sa, sa], axis=1)
    z32 = jnp.zeros((seq, 32), F32)

    def half32(c, s):
        return jnp.concatenate([c, c], 1), jnp.concatenate([-s, z32], 1), jnp.concatenate([z32, s], 1)

    cr, sr = _rope_tables(pos // GRID_W, AXIAL_DIM)
    cc, sc = _rope_tables(pos % GRID_W, AXIAL_DIM)
    r_c, r_lo, r_hi = half32(cr, sr)
    c_c, c_lo, c_hi = half32(cc, sc)
    cos_b = jnp.concatenate([r_c, c_c], 1)
    sin_b_lo = jnp.concatenate([r_lo, c_lo], 1)
    sin_b_hi = jnp.concatenate([r_hi, c_hi], 1)
    c3, s3 = _rope_tables(pos, C_QK_DIM)
    m_c, m_lo, m_hi = half32(c3, s3)
    cos_c = jnp.concatenate([m_c, m_c], 1)
    sin_c_lo = jnp.concatenate([m_lo, m_lo], 1)
    sin_c_hi = jnp.concatenate([m_hi, m_hi], 1)
    return jnp.stack([cos_a, sin_a, cos_b, sin_b_lo, sin_b_hi, cos_c, sin_c_lo, sin_c_hi], axis=0)


def _trunk(x3, tabs, p):
    batch, seq, _ = x3.shape
    x = x3.reshape(batch * seq, D_MODEL)
    h = None
    for li in range(DEPTH):
        lam_init = 0.8 - 0.6 * math.exp(-0.3 * li)
        if li == 0:
            qkv, st1, st2, h = _qkv_proj(x, p["w_in"], li, tabs, p["b_q_norm"][li], p["b_k_norm"][li],
                                         batch, seq, g_pre=p["g_mix_pre"][li])
        else:
            qkv, st1, st2 = _qkv_proj(h, p["w_in"], li, tabs, p["b_q_norm"][li], p["b_k_norm"][li],
                                      batch, seq)
        oa, lse = zip(*[_attn_a_group(src, gi, batch, seq) for gi, src in enumerate((qkv, st1, st2))])
        ob = _attn_b(qkv, batch, seq)
        oc = _attn_c(qkv, p["c_lambda"][li], p["c_head_norm"][li], lam_init, batch, seq)
        merged = _merge(h, p["w_in"], li, oa, lse, ob, oc,
                        p["w_branch_a"], p["w_branch_b"], p["w_branch_c"], seq)
        x, hm = _out_proj(merged, p["w_out"], li, x, p["g_mix_post"][li], p["g_mlp_pre"][li])
        g_next = p["g_mix_pre"][li + 1] if li + 1 < DEPTH else None
        x, h = _mlp(hm, p["w_mlp_in"], p["w_mlp_out"], li, x, p["g_mlp_post"][li], g_next)
    return x.reshape(batch, seq, D_MODEL)


def kernel(x_prompt, x_sample, g_mix_pre, w_in, b_q_norm, b_k_norm, c_lambda, c_head_norm,
           w_branch_a, w_branch_b, w_branch_c, w_out, g_mix_post,
           g_mlp_pre, w_mlp_in, w_mlp_out, g_mlp_post):
    p = dict(g_mix_pre=g_mix_pre, w_in=w_in.astype(BF16), b_q_norm=b_q_norm, b_k_norm=b_k_norm,
             c_lambda=c_lambda, c_head_norm=c_head_norm,
             w_branch_a=w_branch_a.astype(BF16), w_branch_b=w_branch_b.astype(BF16),
             w_branch_c=w_branch_c.astype(BF16), w_out=w_out.astype(BF16), g_mix_post=g_mix_post,
             g_mlp_pre=g_mlp_pre, w_mlp_in=w_mlp_in.astype(BF16), w_mlp_out=w_mlp_out.astype(BF16),
             g_mlp_post=g_mlp_post)
    tabs = _tables(max(x_prompt.shape[1], x_sample.shape[1]))
    return (_trunk(x_prompt, tabs, p), _trunk(x_sample, tabs, p))
```

```python
import functools
import math

import jax
import jax.numpy as jnp
from jax import lax
from jax.experimental import pallas as pl
from jax.experimental.pallas import tpu as pltpu

F32 = jnp.float32
BF16 = jnp.bfloat16

D_MODEL = 2048
DEPTH = 2
HEAD_DIM = 128
GRID_W = 64
ROPE_THETA = 10000.0
EPS = 1e-6
NEG_INF = -1e30
LOG2E = math.log2(math.e)

DIL_GROUPS = ((128, 1), (512, 4), (2048, 16))
A_HALF = 64
A_W = 768
B_Q_HEADS = 6
B_KV_HEADS = 2
B_GROUP = B_Q_HEADS // B_KV_HEADS
B_QW = 768
B_KVW = 256
AXIAL_DIM = 64
C_HEADS = 4
C_QK_DIM = 64
C_QW = 512
C_VW = 512
N_BRANCH = 3
QKV_COLS = 3 * A_W + B_QW + 2 * B_KVW + 2 * C_QW + C_VW
GATE_COLS = N_BRANCH * D_MODEL
D_FF = 4 * D_MODEL

OFF_AQ, OFF_AK, OFF_AV = 0, A_W, 2 * A_W
OFF_BQ = 3 * A_W
OFF_BK, OFF_BV = OFF_BQ + B_QW, OFF_BQ + B_QW + B_KVW
OFF_CQ = OFF_BV + B_KVW
OFF_CK, OFF_CV = OFF_CQ + C_QW, OFF_CQ + 2 * C_QW

SUB = 2 * HEAD_DIM
_QKV_KINDS = (("aq",) * 3 + ("ak",) * 3 + ("v",) * 3 + ("bq",) * 3 + ("bk",) + ("v",)
              + ("cq",) * 2 + ("ck",) * 2 + ("v",) * 2)
N_TABLES = 8

V7X_VMEM_BYTES = 64 * 1024 * 1024
VMEM_LIMIT = V7X_VMEM_BYTES - 8 * 1024 * 1024


def _cparams(sem):
    return pltpu.CompilerParams(dimension_semantics=sem, vmem_limit_bytes=VMEM_LIMIT)


def _rms_rows(xf, g):
    ms = jnp.mean(xf * xf, axis=-1, keepdims=True)
    return xf * lax.rsqrt(ms + EPS) * g


def _rope_half64(x, cos, sin_signed):
    return x * cos + pltpu.roll(x, 64, axis=1) * sin_signed


def _rope_half32(x, cos, sin_lo, sin_hi):
    return x * cos + pltpu.roll(x, 96, axis=1) * sin_lo + pltpu.roll(x, 32, axis=1) * sin_hi


_STREAM_TILES = {1: (1, 4, 7), 2: (2, 5, 8)}


ROW_CHUNK = 256


def _row_chunks(n):
    return [slice(r, r + ROW_CHUNK) for r in range(0, n, ROW_CHUNK)]


def _qkv_kernel(*refs, tm, norm_input):
    if norm_input:
        x_ref, gpre_ref, w_ref, tab_ref, gq_ref, gk_ref, o_ref, s1_ref, s2_ref, h_ref, epi_scr = refs
        for rows in _row_chunks(tm):
            h_ref[rows, :] = _rms_rows(x_ref[rows, :], gpre_ref[...]).astype(h_ref.dtype)
    else:
        h_ref, w_ref, tab_ref, gq_ref, gk_ref, o_ref, s1_ref, s2_ref, epi_scr = refs

    def head_epilogue(x, kind, rows):
        if kind in ("aq", "ak"):
            y = _rope_half64(x, tab_ref[0, rows, :], tab_ref[1, rows, :])
            return y * (HEAD_DIM ** -0.5 * LOG2E) if kind == "aq" else y
        if kind in ("bq", "bk"):
            g = gq_ref[...] * (HEAD_DIM ** -0.5 * LOG2E) if kind == "bq" else gk_ref[...]
            return _rope_half32(_rms_rows(x, g), tab_ref[2, rows, :], tab_ref[3, rows, :], tab_ref[4, rows, :])
        if kind in ("cq", "ck"):
            y = _rope_half32(x, tab_ref[5, rows, :], tab_ref[6, rows, :], tab_ref[7, rows, :])
            return y * (C_QK_DIM ** -0.5 * LOG2E) if kind == "cq" else y
        return x

    n_heads = SUB // HEAD_DIM
    for jj, kind in enumerate(_QKV_KINDS):
        g = next((g for g in (1, 2) if jj in _STREAM_TILES[g]), None)
        for c, rows in enumerate(_row_chunks(tm)):
            acc = jnp.dot(h_ref[rows, :], w_ref[:, jj * SUB:(jj + 1) * SUB], preferred_element_type=F32)
            for hh in range(n_heads):
                sl = slice(hh * HEAD_DIM, (hh + 1) * HEAD_DIM)
                y = head_epilogue(acc[:, sl], kind, rows)
                col = jj * SUB + hh * HEAD_DIM
                o_ref[rows, col:col + HEAD_DIM] = y.astype(o_ref.dtype)
                if g is not None:
                    ref = (s1_ref, s2_ref)[g - 1]
                    dil = DIL_GROUPS[g][1]
                    nu = ROW_CHUNK // dil
                    slot = (jj % 2) * n_heads + hh
                    col0 = _STREAM_TILES[g].index(jj) * SUB + hh * HEAD_DIM
                    epi_scr[slot, rows, :] = y
                    for r in range(dil):
                        rows_r = epi_scr[slot, pl.ds(rows.start + r, nu, stride=dil), :]
                        ref[r, c * nu:(c + 1) * nu, col0:col0 + HEAD_DIM] = rows_r.astype(ref.dtype)


def _qkv_proj(h, w_in, li, tabs, gq, gk, batch, seq, g_pre=None, tm=512):
    t = h.shape[0]
    nseq = seq // tm
    d1, d2 = DIL_GROUPS[1][1], DIL_GROUPS[2][1]
    norm_input = g_pre is not None
    row = pl.BlockSpec((tm, D_MODEL), lambda i: (i, 0))
    vec = lambda n: pl.BlockSpec((1, n), lambda i: (0, 0))

    def stream_spec(dil):
        return pl.BlockSpec((None, dil, tm // dil, 3 * SUB), lambda i: (i // nseq, 0, i % nseq, 0))

    in_specs = [pl.BlockSpec((None, D_MODEL, QKV_COLS), lambda i: (li, 0, 0), pipeline_mode=pl.Buffered(1)),
                pl.BlockSpec((N_TABLES, tm, HEAD_DIM), lambda i: (0, i % nseq, 0)),
                vec(HEAD_DIM), vec(HEAD_DIM)]
    args = [w_in, tabs, gq.reshape(1, HEAD_DIM), gk.reshape(1, HEAD_DIM)]
    out_specs = [pl.BlockSpec((tm, QKV_COLS), lambda i: (i, 0)), stream_spec(d1), stream_spec(d2)]
    out_shape = [jax.ShapeDtypeStruct((t, QKV_COLS), BF16),
                 jax.ShapeDtypeStruct((batch, d1, seq // d1, 3 * SUB), BF16),
                 jax.ShapeDtypeStruct((batch, d2, seq // d2, 3 * SUB), BF16)]
    if norm_input:
        in_specs = [row, vec(D_MODEL)] + in_specs
        args = [h, g_pre.reshape(1, D_MODEL)] + args
        out_specs.append(row)
        out_shape.append(jax.ShapeDtypeStruct((t, D_MODEL), BF16))
    else:
        in_specs = [row] + in_specs
        args = [h] + args
    return pl.pallas_call(
        functools.partial(_qkv_kernel, tm=tm, norm_input=norm_input),
        grid=(t // tm,),
        in_specs=in_specs,
        out_specs=out_specs,
        out_shape=out_shape,
        scratch_shapes=[pltpu.VMEM((2 * (SUB // HEAD_DIM), tm, HEAD_DIM), F32)],
        compiler_params=_cparams(("parallel",)),
        name="qkv_proj",
    )(*args)


A_TQ = 128
A_TK = A_TQ + 2 * A_HALF


def _attn_a_kernel(q_ref, k_ref, v_ref, o_ref, lse_ref, *, length):
    nt = q_ref.shape[0] // A_TQ
    tiles_per_stream = length // A_TQ
    rows = lax.broadcasted_iota(jnp.int32, (A_TQ, A_TK), 0)
    cols = lax.broadcasted_iota(jnp.int32, (A_TQ, A_TK), 1)

    def body(i, carry):
        q0 = pl.multiple_of(i * A_TQ, A_TQ)
        stream0 = (i // tiles_per_stream) * length
        k0 = pl.multiple_of(jnp.clip(q0 - A_HALF, stream0, stream0 + length - A_TK), 16)
        valid = jnp.abs(rows - cols + (q0 - k0)) <= A_HALF
        for hh in range(2):
            sl = slice(hh * HEAD_DIM, (hh + 1) * HEAD_DIM)
            q = q_ref[pl.ds(q0, A_TQ), sl]
            k = k_ref[pl.ds(k0, A_TK), sl]
            v = v_ref[pl.ds(k0, A_TK), sl]
            s = lax.dot_general(q, k, (((1,), (1,)), ((), ())), preferred_element_type=F32)
            s = jnp.where(valid, s, NEG_INF)
            m = jnp.max(s, axis=1, keepdims=True)
            e = jnp.exp2(s - m)
            den = jnp.sum(e, axis=1, keepdims=True)
            o = jnp.dot(e.astype(BF16), v, preferred_element_type=F32) / den
            o_ref[pl.ds(q0, A_TQ), sl] = o.astype(o_ref.dtype)
            lse_ref[pl.ds(q0, A_TQ), sl] = jnp.broadcast_to(m + jnp.log2(den), (A_TQ, HEAD_DIM))
        return carry

    lax.fori_loop(0, nt, body, 0, unroll=min(nt, 4))


def _attn_a_group(src, gi, batch, seq):
    dil = DIL_GROUPS[gi][1]
    length = seq // dil
    if dil == 1:
        cols = (OFF_AQ // SUB + gi, OFF_AK // SUB + gi, OFF_AV // SUB + gi)
    else:
        src = src.reshape(batch * seq, 3 * SUB)
        cols = (0, 1, 2)
    out_spec = pl.BlockSpec((seq, SUB), lambda b: (b, 0))
    o, lse = pl.pallas_call(
        functools.partial(_attn_a_kernel, length=length),
        grid=(batch,),
        in_specs=[pl.BlockSpec((seq, SUB), functools.partial(lambda c, b: (b, c), c)) for c in cols],
        out_specs=[out_spec, out_spec],
        out_shape=[jax.ShapeDtypeStruct((batch * seq, SUB), BF16),
                   jax.ShapeDtypeStruct((batch * seq, SUB), F32)],
        compiler_params=_cparams(("parallel",)),
        name=f"attn_a{gi}",
    )(src, src, src)
    if dil == 1:
        return o, lse
    return o.reshape(batch, dil, length, SUB), lse.reshape(batch, dil, length, SUB)


FLASH_BLOCK = 256


def _transpose_values(v_ref, vt_scr, tk):
    for t in range(vt_scr.shape[0]):
        vt_scr[t] = v_ref[t * tk:(t + 1) * tk, :].astype(F32).T.astype(vt_scr.dtype)


N_SCORE_BUFS = 3


def _flash_scratch(seq, tk, rows):
    return ([pltpu.VMEM((seq // tk, HEAD_DIM, tk), BF16),
             pltpu.VMEM((N_SCORE_BUFS, tk, rows), F32),
             pltpu.VMEM((2, tk, rows), BF16),
             pltpu.VMEM((HEAD_DIM, rows), F32)])


def _flash_keymajor(q, k_ref, vt_scr, s_scr, p_scr, acc_scr, tk):
    rows = q.shape[0]
    nk = vt_scr.shape[0]
    q_blocks = [slice(c, c + FLASH_BLOCK) for c in range(0, rows, FLASH_BLOCK)]
    k_chunks = [slice(c, c + FLASH_BLOCK) for c in range(0, tk, FLASH_BLOCK)]
    bufs = [s_scr.at[b] for b in range(N_SCORE_BUFS)]

    def scores(t, qb, dst):
        k = k_ref[t * tk:(t + 1) * tk, :]
        dst[:, qb] = lax.dot_general(k, q[qb, :], (((1,), (1,)), ((), ())), preferred_element_type=F32)

    def update(m, l, src, t, qb):
        m_new = m
        for kc in k_chunks:
            m_new = jnp.maximum(m_new, jnp.max(src[kc, qb], axis=0, keepdims=True))
        alpha = jnp.exp2(m - m_new)
        l_tile = None
        p_buf = p_scr.at[t % 2]
        for kc in k_chunks:
            p = jnp.exp2(src[kc, qb] - m_new)
            p_sum = jnp.sum(p, axis=0, keepdims=True)
            l_tile = p_sum if l_tile is None else l_tile + p_sum
            p_buf[kc, qb] = p.astype(BF16)
        pv = jnp.dot(vt_scr[t], p_buf[:, qb], preferred_element_type=F32)
        acc_scr[:, qb] = pv if t == 0 else alpha * acc_scr[:, qb] + pv
        return m_new, alpha * l + l_tile

    m = [jnp.full((1, FLASH_BLOCK), NEG_INF, F32) for _ in q_blocks]
    l = [jnp.zeros((1, FLASH_BLOCK), F32) for _ in q_blocks]
    for qb in q_blocks:
        scores(0, qb, bufs[0])
    for t in range(nk):
        for bi, qb in enumerate(q_blocks):
            if t + 1 < nk:
                scores(t + 1, qb, bufs[(t + 1) % N_SCORE_BUFS])
            m[bi], l[bi] = update(m[bi], l[bi], bufs[t % N_SCORE_BUFS], t, qb)
    return acc_scr[...], jnp.concatenate(l, axis=1)


def _attn_b_kernel(q_ref, k_ref, v_ref, o_ref, vt_scr, s_scr, p_scr, acc_scr, *, tq, tk):
    @pl.when(pl.program_id(2) == 0)
    def _():
        _transpose_values(v_ref, vt_scr, tk)

    q = jnp.concatenate([q_ref[:, h * HEAD_DIM:(h + 1) * HEAD_DIM] for h in range(B_GROUP)], axis=0)
    acc, l = _flash_keymajor(q, k_ref, vt_scr, s_scr, p_scr, acc_scr, tk)
    o = acc * (1.0 / l)
    for h in range(B_GROUP):
        o_ref[:, h * HEAD_DIM:(h + 1) * HEAD_DIM] = o[:, h * tq:(h + 1) * tq].T.astype(o_ref.dtype)


def _attn_b(qkv, batch, seq, tq=512, tk=1024):
    nq = seq // tq
    gw = B_GROUP * HEAD_DIM
    rows = B_GROUP * tq
    return pl.pallas_call(
        functools.partial(_attn_b_kernel, tq=tq, tk=tk),
        grid=(batch, B_KV_HEADS, nq),
        in_specs=[pl.BlockSpec((tq, gw), lambda b, kh, i: (b * nq + i, OFF_BQ // gw + kh)),
                  pl.BlockSpec((seq, HEAD_DIM), lambda b, kh, i: (b, OFF_BK // HEAD_DIM + kh)),
                  pl.BlockSpec((seq, HEAD_DIM), lambda b, kh, i: (b, OFF_BV // HEAD_DIM + kh))],
        out_specs=pl.BlockSpec((tq, gw), lambda b, kh, i: (b * nq + i, kh)),
        out_shape=jax.ShapeDtypeStruct((batch * seq, B_QW), BF16),
        scratch_shapes=_flash_scratch(seq, tk, rows),
        compiler_params=_cparams(("parallel", "parallel", "arbitrary")),
        name="attn_b",
    )(qkv, qkv, qkv)


def _attn_c_kernel(lam_ref, g_ref, q_ref, k_ref, v_ref, o_ref, vt_scr, s_scr, p_scr, acc_scr,
                   *, tq, tk, lam_init):
    @pl.when(pl.program_id(2) == 0)
    def _():
        _transpose_values(v_ref, vt_scr, tk)

    lamv = lam_ref[...]
    lam = (jnp.exp(jnp.sum(lamv[0:1] * lamv[1:2], axis=1, keepdims=True))
           - jnp.exp(jnp.sum(lamv[2:3] * lamv[3:4], axis=1, keepdims=True)) + lam_init)
    q = q_ref[...]
    lane = lax.broadcasted_iota(jnp.int32, q.shape, 1)
    zero = jnp.zeros_like(q)
    q2 = jnp.concatenate([jnp.where(lane < C_QK_DIM, q, zero), jnp.where(lane >= C_QK_DIM, q, zero)], axis=0)
    acc, l = _flash_keymajor(q2, k_ref, vt_scr, s_scr, p_scr, acc_scr, tk)
    o = acc * (1.0 / l)
    oc = (o[:, :tq] - lam * o[:, tq:]).T
    o_ref[...] = (_rms_rows(oc, g_ref[...]) * (1.0 - lam_init)).astype(o_ref.dtype)


def _attn_c(qkv, c_lambda, c_head_norm, lam_init, batch, seq, tq=512, tk=2048):
    nq = seq // tq
    rows = 2 * tq
    return pl.pallas_call(
        functools.partial(_attn_c_kernel, tq=tq, tk=tk, lam_init=lam_init),
        grid=(batch, C_HEADS, nq),
        in_specs=[pl.BlockSpec((4, C_QK_DIM), lambda b, h, i: (0, 0)),
                  pl.BlockSpec((1, HEAD_DIM), lambda b, h, i: (0, 0)),
                  pl.BlockSpec((tq, HEAD_DIM), lambda b, h, i: (b * nq + i, OFF_CQ // HEAD_DIM + h)),
                  pl.BlockSpec((seq, HEAD_DIM), lambda b, h, i: (b, OFF_CK // HEAD_DIM + h)),
                  pl.BlockSpec((seq, HEAD_DIM), lambda b, h, i: (b, OFF_CV // HEAD_DIM + h))],
        out_specs=pl.BlockSpec((tq, HEAD_DIM), lambda b, h, i: (b * nq + i, h)),
        out_shape=jax.ShapeDtypeStruct((batch * seq, C_VW), BF16),
        scratch_shapes=_flash_scratch(seq, tk, rows),
        compiler_params=_cparams(("parallel", "parallel", "arbitrary")),
        name="attn_c",
    )(c_lambda, c_head_norm.reshape(1, HEAD_DIM), qkv, qkv, qkv)


def _merge_kernel(h_ref, wg0, wg1, wg2, oa0, oa1, oa2, ls0, ls1, ls2, ob, oc, wa, wb, wc, out_ref,
                  oa_scr, o1_scr, o2_scr, l1_scr, l2_scr, *, tm):
    def mix_groups(c, rows):
        for g, o_src, l_src, o_dst, l_dst in ((1, oa1, ls1, o1_scr, l1_scr), (2, oa2, ls2, o2_scr, l2_scr)):
            dil = DIL_GROUPS[g][1]
            nu = ROW_CHUNK // dil
            for r in range(dil):
                dst_rows = pl.ds(rows.start + r, nu, stride=dil)
                for hh in range(SUB // HEAD_DIM):
                    sl = slice(hh * HEAD_DIM, (hh + 1) * HEAD_DIM)
                    o_dst[hh, dst_rows, :] = o_src[r, c * nu:(c + 1) * nu, sl].astype(F32)
                    l_dst[hh, dst_rows, :] = l_src[r, c * nu:(c + 1) * nu, sl]
        for hh in range(SUB // HEAD_DIM):
            sl = slice(hh * HEAD_DIM, (hh + 1) * HEAD_DIM)
            l0, l1, l2 = ls0[rows, sl], l1_scr[hh, rows, :], l2_scr[hh, rows, :]
            m = jnp.maximum(jnp.maximum(l0, l1), l2)
            e0, e1, e2 = jnp.exp2(l0 - m), jnp.exp2(l1 - m), jnp.exp2(l2 - m)
            num = e0 * oa0[rows, sl].astype(F32) + e1 * o1_scr[hh, rows, :] + e2 * o2_scr[hh, rows, :]
            oa_scr[rows, sl] = (num / (e0 + e1 + e2)).astype(oa_scr.dtype)

    def project(rows):
        hr = h_ref[rows, :]
        merged = None
        for wg, o_br, w_br in ((wg0, oa_scr, wa), (wg1, ob, wb), (wg2, oc, wc)):
            gate = jax.nn.sigmoid(jnp.dot(hr, wg[...], preferred_element_type=F32))
            term = gate * jnp.dot(o_br[rows, :], w_br[...], preferred_element_type=F32)
            merged = term if merged is None else merged + term
        out_ref[rows, :] = merged.astype(out_ref.dtype)

    j = pl.program_id(1)

    @pl.when(j == 0)
    def _():
        for c, rows in enumerate(_row_chunks(tm)):
            mix_groups(c, rows)
            project(rows)

    @pl.when(j > 0)
    def _():
        for rows in _row_chunks(tm):
            project(rows)


def _merge(h, w_in, li, oa, lse, ob, oc, wa, wb, wc, seq, tm=1024, tn=512):
    t = ob.shape[0]
    nj = D_MODEL // tn
    nseq = seq // tm

    def row(w):
        return pl.BlockSpec((tm, w), lambda i, j: (i, 0))

    def stream(g):
        dil = DIL_GROUPS[g][1]
        return pl.BlockSpec((None, dil, tm // dil, SUB), lambda i, j: (i // nseq, 0, i % nseq, 0))

    def gate_w(br):
        return pl.BlockSpec((None, D_MODEL, tn),
                            lambda i, j: (li, 0, (QKV_COLS + br * D_MODEL) // tn + j))

    def wspec(k):
        return pl.BlockSpec((None, k, tn), lambda i, j: (li, 0, j))

    return pl.pallas_call(
        functools.partial(_merge_kernel, tm=tm),
        grid=(t // tm, nj),
        in_specs=[row(D_MODEL), gate_w(0), gate_w(1), gate_w(2),
                  row(SUB), stream(1), stream(2), row(SUB), stream(1), stream(2),
                  row(B_QW), row(C_VW), wspec(SUB), wspec(B_QW), wspec(C_VW)],
        out_specs=pl.BlockSpec((tm, tn), lambda i, j: (i, j)),
        out_shape=jax.ShapeDtypeStruct((t, D_MODEL), BF16),
        scratch_shapes=[pltpu.VMEM((tm, SUB), BF16)] + [pltpu.VMEM((SUB // HEAD_DIM, tm, HEAD_DIM), F32)] * 4,
        compiler_params=_cparams(("parallel", "arbitrary")),
        name="merge",
    )(h, w_in, w_in, w_in, *oa, *lse, ob, oc, wa, wb, wc)


def _out_kernel(m_ref, w_ref, x_ref, gpost_ref, gnext_ref, x1_ref, h_ref):
    for rows in _row_chunks(m_ref.shape[0]):
        y = jnp.dot(m_ref[rows, :], w_ref[...], preferred_element_type=F32)
        x1 = x_ref[rows, :] + _rms_rows(y, gpost_ref[...])
        x1_ref[rows, :] = x1
        h_ref[rows, :] = _rms_rows(x1, gnext_ref[...]).astype(h_ref.dtype)


def _out_proj(merged, w_out, li, x, g_post, g_next, tm=512):
    t = x.shape[0]
    row = lambda i: (i, 0)
    const = lambda i: (0, 0)
    return pl.pallas_call(
        _out_kernel,
        grid=(t // tm,),
        in_specs=[pl.BlockSpec((tm, D_MODEL), row),
                  pl.BlockSpec((None, D_MODEL, D_MODEL), lambda i: (li, 0, 0)),
                  pl.BlockSpec((tm, D_MODEL), row),
                  pl.BlockSpec((1, D_MODEL), const),
                  pl.BlockSpec((1, D_MODEL), const)],
        out_specs=[pl.BlockSpec((tm, D_MODEL), row), pl.BlockSpec((tm, D_MODEL), row)],
        out_shape=[jax.ShapeDtypeStruct((t, D_MODEL), F32), jax.ShapeDtypeStruct((t, D_MODEL), BF16)],
        compiler_params=_cparams(("parallel",)),
        name="out_proj",
    )(merged, w_out, x, g_post.reshape(1, D_MODEL), g_next.reshape(1, D_MODEL))


def _mlp_kernel(h_ref, w1_ref, w2_ref, x_ref, gpost_ref, *rest, with_next):
    if with_next:
        gnext_ref, x2_ref, hn_ref, acc_ref = rest
    else:
        x2_ref, acc_ref = rest
    f = pl.program_id(1)

    @pl.when(f == 0)
    def _():
        acc_ref[...] = jnp.zeros_like(acc_ref)

    def partial_out(rows):
        u = jnp.dot(h_ref[rows, :], w1_ref[...], preferred_element_type=F32)
        u = jnp.square(jnp.maximum(u, 0.0)).astype(BF16)
        return acc_ref[rows, :] + jnp.dot(u, w2_ref[...], preferred_element_type=F32)

    last = pl.num_programs(1) - 1

    @pl.when(f < last)
    def _():
        acc_ref[...] = partial_out(slice(None))

    @pl.when(f == last)
    def _():
        for rows in _row_chunks(h_ref.shape[0]):
            x2 = x_ref[rows, :] + _rms_rows(partial_out(rows), gpost_ref[...])
            x2_ref[rows, :] = x2
            if with_next:
                hn_ref[rows, :] = _rms_rows(x2, gnext_ref[...]).astype(hn_ref.dtype)


def _mlp(h, w1, w2, li, x, g_post, g_next, tm=512, tf=1024):
    t = x.shape[0]
    with_next = g_next is not None
    row = lambda i, f: (i, 0)
    const = lambda i, f: (0, 0)
    in_specs = [pl.BlockSpec((tm, D_MODEL), row),
                pl.BlockSpec((None, D_MODEL, tf), lambda i, f: (li, 0, f)),
                pl.BlockSpec((None, tf, D_MODEL), lambda i, f: (li, f, 0)),
                pl.BlockSpec((tm, D_MODEL), row),
                pl.BlockSpec((1, D_MODEL), const)]
    args = [h, w1, w2, x, g_post.reshape(1, D_MODEL)]
    out_specs = [pl.BlockSpec((tm, D_MODEL), row)]
    out_shape = [jax.ShapeDtypeStruct((t, D_MODEL), F32)]
    if with_next:
        in_specs.append(pl.BlockSpec((1, D_MODEL), const))
        args.append(g_next.reshape(1, D_MODEL))
        out_specs.append(pl.BlockSpec((tm, D_MODEL), row))
        out_shape.append(jax.ShapeDtypeStruct((t, D_MODEL), BF16))
    res = pl.pallas_call(
        functools.partial(_mlp_kernel, with_next=with_next),
        grid=(t // tm, D_FF // tf),
        in_specs=in_specs,
        out_specs=out_specs,
        out_shape=out_shape,
        scratch_shapes=[pltpu.VMEM((tm, D_MODEL), F32)],
        compiler_params=_cparams(("parallel", "arbitrary")),
        name="mlp",
    )(*args)
    return (res[0], res[1]) if with_next else (res[0], None)


def _rope_tables(pos, dim):
    inv = 1.0 / (ROPE_THETA ** (jnp.arange(0, dim, 2, dtype=F32) / dim))
    ang = pos.astype(F32)[:, None] * inv[None, :]
    return jnp.cos(ang), jnp.sin(ang)


def _tables(seq):
    pos = jnp.arange(seq)
    ca, sa = _rope_tables(pos, HEAD_DIM)
    cos_a = jnp.concatenate([ca, ca], axis=1)
    sin_a = jnp.concatenate([-sa, sa], axis=1)
    z32 = jnp.zeros((seq, 32), F32)

    def half32(c, s):
        return jnp.concatenate([c, c], 1), jnp.concatenate([-s, z32], 1), jnp.concatenate([z32, s], 1)

    cr, sr = _rope_tables(pos // GRID_W, AXIAL_DIM)
    cc, sc = _rope_tables(pos % GRID_W, AXIAL_DIM)
    r_c, r_lo, r_hi = half32(cr, sr)
    c_c, c_lo, c_hi = half32(cc, sc)
    cos_b = jnp.concatenate([r_c, c_c], 1)
    sin_b_lo = jnp.concatenate([r_lo, c_lo], 1)
    sin_b_hi = jnp.concatenate([r_hi, c_hi], 1)
    c3, s3 = _rope_tables(pos, C_QK_DIM)
    m_c, m_lo, m_hi = half32(c3, s3)
    cos_c = jnp.concatenate([m_c, m_c], 1)
    sin_c_lo = jnp.concatenate([m_lo, m_lo], 1)
    sin_c_hi = jnp.concatenate([m_hi, m_hi], 1)
    return jnp.stack([cos_a, sin_a, cos_b, sin_b_lo, sin_b_hi, cos_c, sin_c_lo, sin_c_hi], axis=0)


def _trunk(x3, tabs, p):
    batch, seq, _ = x3.shape
    x = x3.reshape(batch * seq, D_MODEL)
    h = None
    for li in range(DEPTH):
        lam_init = 0.8 - 0.6 * math.exp(-0.3 * li)
        if li == 0:
            qkv, st1, st2, h = _qkv_proj(x, p["w_in"], li, tabs, p["b_q_norm"][li], p["b_k_norm"][li],
                                         batch, seq, g_pre=p["g_mix_pre"][li])
        else:
            qkv, st1, st2 = _qkv_proj(h, p["w_in"], li, tabs, p["b_q_norm"][li], p["b_k_norm"][li],
                                      batch, seq)
        oa, lse = zip(*[_attn_a_group(src, gi, batch, seq) for gi, src in enumerate((qkv, st1, st2))])
        ob = _attn_b(qkv, batch, seq)
        oc = _attn_c(qkv, p["c_lambda"][li], p["c_head_norm"][li], lam_init, batch, seq)
        merged = _merge(h, p["w_in"], li, oa, lse, ob, oc,
                        p["w_branch_a"], p["w_branch_b"], p["w_branch_c"], seq)
        x, hm = _out_proj(merged, p["w_out"], li, x, p["g_mix_post"][li], p["g_mlp_pre"][li])
        g_next = p["g_mix_pre"][li + 1] if li + 1 < DEPTH else None
        x, h = _mlp(hm, p["w_mlp_in"], p["w_mlp_out"], li, x, p["g_mlp_post"][li], g_next)
    return x.reshape(batch, seq, D_MODEL)


def kernel(x_prompt, x_sample, g_mix_pre, w_in, b_q_norm, b_k_norm, c_lambda, c_head_norm,
           w_branch_a, w_branch_b, w_branch_c, w_out, g_mix_post,
           g_mlp_pre, w_mlp_in, w_mlp_out, g_mlp_post):
    p = dict(g_mix_pre=g_mix_pre, w_in=w_in.astype(BF16), b_q_norm=b_q_norm, b_k_norm=b_k_norm,
             c_lambda=c_lambda, c_head_norm=c_head_norm,
             w_branch_a=w_branch_a.astype(BF16), w_branch_b=w_branch_b.astype(BF16),
             w_branch_c=w_branch_c.astype(BF16), w_out=w_out.astype(BF16), g_mix_post=g_mix_post,
             g_mlp_pre=g_mlp_pre, w_mlp_in=w_mlp_in.astype(BF16), w_mlp_out=w_mlp_out.astype(BF16),
             g_mlp_post=g_mlp_post)
    tabs = _tables(max(x_prompt.shape[1], x_sample.shape[1]))
    return (_trunk(x_prompt, tabs, p), _trunk(x_sample, tabs, p))
```

```python
import functools
import math

import jax
import jax.numpy as jnp
from jax import lax
from jax.experimental import pallas as pl
from jax.experimental.pallas import tpu as pltpu

F32 = jnp.float32
BF16 = jnp.bfloat16

D_MODEL = 2048
DEPTH = 2
HEAD_DIM = 128
GRID_W = 64
ROPE_THETA = 10000.0
EPS = 1e-6
NEG_INF = -1e30
LOG2E = math.log2(math.e)

DIL_GROUPS = ((128, 1), (512, 4), (2048, 16))
A_HALF = 64
A_W = 768
B_Q_HEADS = 6
B_KV_HEADS = 2
B_GROUP = B_Q_HEADS // B_KV_HEADS
B_QW = 768
B_KVW = 256
AXIAL_DIM = 64
C_HEADS = 4
C_QK_DIM = 64
C_QW = 512
C_VW = 512
N_BRANCH = 3
QKV_COLS = 3 * A_W + B_QW + 2 * B_KVW + 2 * C_QW + C_VW
GATE_COLS = N_BRANCH * D_MODEL
D_FF = 4 * D_MODEL

OFF_AQ, OFF_AK, OFF_AV = 0, A_W, 2 * A_W
OFF_BQ = 3 * A_W
OFF_BK, OFF_BV = OFF_BQ + B_QW, OFF_BQ + B_QW + B_KVW
OFF_CQ = OFF_BV + B_KVW
OFF_CK, OFF_CV = OFF_CQ + C_QW, OFF_CQ + 2 * C_QW

SUB = 2 * HEAD_DIM
_QKV_KINDS = (("aq",) * 3 + ("ak",) * 3 + ("v",) * 3 + ("bq",) * 3 + ("bk",) + ("v",)
              + ("cq",) * 2 + ("ck",) * 2 + ("v",) * 2)
N_TABLES = 8

V7X_VMEM_BYTES = 64 * 1024 * 1024
VMEM_LIMIT = V7X_VMEM_BYTES - 8 * 1024 * 1024


def _cparams(sem):
    return pltpu.CompilerParams(dimension_semantics=sem, vmem_limit_bytes=VMEM_LIMIT)


def _rms_rows(xf, g):
    ms = jnp.mean(xf * xf, axis=-1, keepdims=True)
    return xf * lax.rsqrt(ms + EPS) * g


def _rope_half64(x, cos, sin_signed):
    return x * cos + pltpu.roll(x, 64, axis=1) * sin_signed


def _rope_half32(x, cos, sin_lo, sin_hi):
    return x * cos + pltpu.roll(x, 96, axis=1) * sin_lo + pltpu.roll(x, 32, axis=1) * sin_hi


_STREAM_TILES = {1: (1, 4, 7), 2: (2, 5, 8)}


ROW_CHUNK = 256


def _row_chunks(n):
    return [slice(r, r + ROW_CHUNK) for r in range(0, n, ROW_CHUNK)]


def _qkv_kernel(*refs, tm, norm_input):
    if norm_input:
        x_ref, gpre_ref, w_ref, tab_ref, gq_ref, gk_ref, o_ref, s1_ref, s2_ref, h_ref, epi_scr = refs
        for rows in _row_chunks(tm):
            h_ref[rows, :] = _rms_rows(x_ref[rows, :], gpre_ref[...]).astype(h_ref.dtype)
    else:
        h_ref, w_ref, tab_ref, gq_ref, gk_ref, o_ref, s1_ref, s2_ref, epi_scr = refs

    def head_epilogue(x, kind, rows):
        if kind in ("aq", "ak"):
            y = _rope_half64(x, tab_ref[0, rows, :], tab_ref[1, rows, :])
            return y * (HEAD_DIM ** -0.5 * LOG2E) if kind == "aq" else y
        if kind in ("bq", "bk"):
            g = gq_ref[...] * (HEAD_DIM ** -0.5 * LOG2E) if kind == "bq" else gk_ref[...]
            return _rope_half32(_rms_rows(x, g), tab_ref[2, rows, :], tab_ref[3, rows, :], tab_ref[4, rows, :])
        if kind in ("cq", "ck"):
            y = _rope_half32(x, tab_ref[5, rows, :], tab_ref[6, rows, :], tab_ref[7, rows, :])
            return y * (C_QK_DIM ** -0.5 * LOG2E) if kind == "cq" else y
        return x

    n_heads = SUB // HEAD_DIM
    for jj, kind in enumerate(_QKV_KINDS):
        g = next((g for g in (1, 2) if jj in _STREAM_TILES[g]), None)
        for c, rows in enumerate(_row_chunks(tm)):
            acc = jnp.dot(h_ref[rows, :], w_ref[:, jj * SUB:(jj + 1) * SUB], preferred_element_type=F32)
            for hh in range(n_heads):
                sl = slice(hh * HEAD_DIM, (hh + 1) * HEAD_DIM)
                y = head_epilogue(acc[:, sl], kind, rows)
                col = jj * SUB + hh * HEAD_DIM
                o_ref[rows, col:col + HEAD_DIM] = y.astype(o_ref.dtype)
                if g is not None:
                    ref = (s1_ref, s2_ref)[g - 1]
                    dil = DIL_GROUPS[g][1]
                    nu = ROW_CHUNK // dil
                    slot = (jj % 2) * n_heads + hh
                    col0 = _STREAM_TILES[g].index(jj) * SUB + hh * HEAD_DIM
                    epi_scr[slot, rows, :] = y
                    for r in range(dil):
                        rows_r = epi_scr[slot, pl.ds(rows.start + r, nu, stride=dil), :]
                        ref[r, c * nu:(c + 1) * nu, col0:col0 + HEAD_DIM] = rows_r.astype(ref.dtype)


def _qkv_proj(h, w_in, li, tabs, gq, gk, batch, seq, g_pre=None, tm=512):
    t = h.shape[0]
    nseq = seq // tm
    d1, d2 = DIL_GROUPS[1][1], DIL_GROUPS[2][1]
    norm_input = g_pre is not None
    row = pl.BlockSpec((tm, D_MODEL), lambda i: (i, 0))
    vec = lambda n: pl.BlockSpec((1, n), lambda i: (0, 0))

    def stream_spec(dil):
        return pl.BlockSpec((None, dil, tm // dil, 3 * SUB), lambda i: (i // nseq, 0, i % nseq, 0))

    in_specs = [pl.BlockSpec((None, D_MODEL, QKV_COLS), lambda i: (li, 0, 0), pipeline_mode=pl.Buffered(1)),
                pl.BlockSpec((N_TABLES, tm, HEAD_DIM), lambda i: (0, i % nseq, 0)),
                vec(HEAD_DIM), vec(HEAD_DIM)]
    args = [w_in, tabs, gq.reshape(1, HEAD_DIM), gk.reshape(1, HEAD_DIM)]
    out_specs = [pl.BlockSpec((tm, QKV_COLS), lambda i: (i, 0)), stream_spec(d1), stream_spec(d2)]
    out_shape = [jax.ShapeDtypeStruct((t, QKV_COLS), BF16),
                 jax.ShapeDtypeStruct((batch, d1, seq // d1, 3 * SUB), BF16),
                 jax.ShapeDtypeStruct((batch, d2, seq // d2, 3 * SUB), BF16)]
    if norm_input:
        in_specs = [row, vec(D_MODEL)] + in_specs
        args = [h, g_pre.reshape(1, D_MODEL)] + args
        out_specs.append(row)
        out_shape.append(jax.ShapeDtypeStruct((t, D_MODEL), BF16))
    else:
        in_specs = [row] + in_specs
        args = [h] + args
    return pl.pallas_call(
        functools.partial(_qkv_kernel, tm=tm, norm_input=norm_input),
        grid=(t // tm,),
        in_specs=in_specs,
        out_specs=out_specs,
        out_shape=out_shape,
        scratch_shapes=[pltpu.VMEM((2 * (SUB // HEAD_DIM), tm, HEAD_DIM), F32)],
        compiler_params=_cparams(("parallel",)),
        name="qkv_proj",
    )(*args)


A_TQ = 128
A_TK = A_TQ + 2 * A_HALF


def _attn_a_kernel(q_ref, k_ref, v_ref, o_ref, lse_ref, *, length):
    nt = q_ref.shape[0] // A_TQ
    tiles_per_stream = length // A_TQ
    rows = lax.broadcasted_iota(jnp.int32, (A_TQ, A_TK), 0)
    cols = lax.broadcasted_iota(jnp.int32, (A_TQ, A_TK), 1)

    def body(i, carry):
        q0 = pl.multiple_of(i * A_TQ, A_TQ)
        stream0 = (i // tiles_per_stream) * length
        k0 = pl.multiple_of(jnp.clip(q0 - A_HALF, stream0, stream0 + length - A_TK), 16)
        valid = jnp.abs(rows - cols + (q0 - k0)) <= A_HALF
        for hh in range(2):
            sl = slice(hh * HEAD_DIM, (hh + 1) * HEAD_DIM)
            q = q_ref[pl.ds(q0, A_TQ), sl]
            k = k_ref[pl.ds(k0, A_TK), sl]
            v = v_ref[pl.ds(k0, A_TK), sl]
            s = lax.dot_general(q, k, (((1,), (1,)), ((), ())), preferred_element_type=F32)
            s = jnp.where(valid, s, NEG_INF)
            m = jnp.max(s, axis=1, keepdims=True)
            e = jnp.exp2(s - m)
            den = jnp.sum(e, axis=1, keepdims=True)
            o = jnp.dot(e.astype(BF16), v, preferred_element_type=F32) / den
            o_ref[pl.ds(q0, A_TQ), sl] = o.astype(o_ref.dtype)
            lse_ref[pl.ds(q0, A_TQ), sl] = jnp.broadcast_to(m + jnp.log2(den), (A_TQ, HEAD_DIM))
        return carry

    lax.fori_loop(0, nt, body, 0, unroll=min(nt, 4))


def _attn_a_group(src, gi, batch, seq):
    dil = DIL_GROUPS[gi][1]
    length = seq // dil
    if dil == 1:
        cols = (OFF_AQ // SUB + gi, OFF_AK // SUB + gi, OFF_AV // SUB + gi)
    else:
        src = src.reshape(batch * seq, 3 * SUB)
        cols = (0, 1, 2)
    out_spec = pl.BlockSpec((seq, SUB), lambda b: (b, 0))
    o, lse = pl.pallas_call(
        functools.partial(_attn_a_kernel, length=length),
        grid=(batch,),
        in_specs=[pl.BlockSpec((seq, SUB), functools.partial(lambda c, b: (b, c), c)) for c in cols],
        out_specs=[out_spec, out_spec],
        out_shape=[jax.ShapeDtypeStruct((batch * seq, SUB), BF16),
                   jax.ShapeDtypeStruct((batch * seq, SUB), F32)],
        compiler_params=_cparams(("parallel",)),
        name=f"attn_a{gi}",
    )(src, src, src)
    if dil == 1:
        return o, lse
    return o.reshape(batch, dil, length, SUB), lse.reshape(batch, dil, length, SUB)


FLASH_BLOCK = 256


def _transpose_values(v_ref, vt_scr, tk):
    for t in range(vt_scr.shape[0]):
        vt_scr[t] = v_ref[t * tk:(t + 1) * tk, :].astype(F32).T.astype(vt_scr.dtype)


MAX_SCORE_BUFS = 3
FLASH_SCRATCH_BYTES = VMEM_LIMIT * 4 // 7


def _score_bufs(nk):
    return min(nk, MAX_SCORE_BUFS)


def _flash_key_tile(seq, rows):
    tk = seq
    while True:
        nk = seq // tk
        if (_score_bufs(nk) * 4 + min(nk, 2) * 2) * tk * rows <= FLASH_SCRATCH_BYTES:
            return tk
        tk //= 2


def _flash_scratch(seq, tk, rows):
    nk = seq // tk
    return ([pltpu.VMEM((nk, HEAD_DIM, tk), BF16),
             pltpu.VMEM((_score_bufs(nk), tk, rows), F32),
             pltpu.VMEM((min(nk, 2), tk, rows), BF16),
             pltpu.VMEM((HEAD_DIM, rows), F32)])


def _flash_keymajor(q, k_ref, vt_scr, s_scr, p_scr, acc_scr, tk):
    rows = q.shape[0]
    nk = vt_scr.shape[0]
    q_blocks = [slice(c, c + FLASH_BLOCK) for c in range(0, rows, FLASH_BLOCK)]
    k_chunks = [slice(c, c + FLASH_BLOCK) for c in range(0, tk, FLASH_BLOCK)]
    n_bufs, n_pbufs = s_scr.shape[0], p_scr.shape[0]
    bufs = [s_scr.at[b] for b in range(n_bufs)]

    def scores(t, qb, dst):
        k = k_ref[t * tk:(t + 1) * tk, :]
        dst[:, qb] = lax.dot_general(k, q[qb, :], (((1,), (1,)), ((), ())), preferred_element_type=F32)

    def update(m, l, src, t, qb):
        m_new = m
        for kc in k_chunks:
            m_new = jnp.maximum(m_new, jnp.max(src[kc, qb], axis=0, keepdims=True))
        alpha = jnp.exp2(m - m_new)
        l_tile = None
        p_buf = p_scr.at[t % n_pbufs]
        for kc in k_chunks:
            p = jnp.exp2(src[kc, qb] - m_new)
            p_sum = jnp.sum(p, axis=0, keepdims=True)
            l_tile = p_sum if l_tile is None else l_tile + p_sum
            p_buf[kc, qb] = p.astype(BF16)
        pv = jnp.dot(vt_scr[t], p_buf[:, qb], preferred_element_type=F32)
        acc_scr[:, qb] = pv if t == 0 else alpha * acc_scr[:, qb] + pv
        return m_new, alpha * l + l_tile

    m = [jnp.full((1, FLASH_BLOCK), NEG_INF, F32) for _ in q_blocks]
    l = [jnp.zeros((1, FLASH_BLOCK), F32) for _ in q_blocks]
    for qb in q_blocks:
        scores(0, qb, bufs[0])
    for t in range(nk):
        for bi, qb in enumerate(q_blocks):
            if t + 1 < nk:
                scores(t + 1, qb, bufs[(t + 1) % n_bufs])
            m[bi], l[bi] = update(m[bi], l[bi], bufs[t % n_bufs], t, qb)
    return acc_scr[...], jnp.concatenate(l, axis=1)


def _attn_b_kernel(q_ref, k_ref, v_ref, o_ref, vt_scr, s_scr, p_scr, acc_scr, *, tq, tk):
    @pl.when(pl.program_id(2) == 0)
    def _():
        _transpose_values(v_ref, vt_scr, tk)

    q = jnp.concatenate([q_ref[:, h * HEAD_DIM:(h + 1) * HEAD_DIM] for h in range(B_GROUP)], axis=0)
    acc, l = _flash_keymajor(q, k_ref, vt_scr, s_scr, p_scr, acc_scr, tk)
    o = acc * (1.0 / l)
    for h in range(B_GROUP):
        o_ref[:, h * HEAD_DIM:(h + 1) * HEAD_DIM] = o[:, h * tq:(h + 1) * tq].T.astype(o_ref.dtype)


def _attn_b(qkv, batch, seq, tq=512):
    nq = seq // tq
    gw = B_GROUP * HEAD_DIM
    rows = B_GROUP * tq
    tk = _flash_key_tile(seq, rows)
    return pl.pallas_call(
        functools.partial(_attn_b_kernel, tq=tq, tk=tk),
        grid=(batch, B_KV_HEADS, nq),
        in_specs=[pl.BlockSpec((tq, gw), lambda b, kh, i: (b * nq + i, OFF_BQ // gw + kh)),
                  pl.BlockSpec((seq, HEAD_DIM), lambda b, kh, i: (b, OFF_BK // HEAD_DIM + kh)),
                  pl.BlockSpec((seq, HEAD_DIM), lambda b, kh, i: (b, OFF_BV // HEAD_DIM + kh))],
        out_specs=pl.BlockSpec((tq, gw), lambda b, kh, i: (b * nq + i, kh)),
        out_shape=jax.ShapeDtypeStruct((batch * seq, B_QW), BF16),
        scratch_shapes=_flash_scratch(seq, tk, rows),
        compiler_params=_cparams(("parallel", "parallel", "arbitrary")),
        name="attn_b",
    )(qkv, qkv, qkv)


def _attn_c_kernel(lam_ref, g_ref, q_ref, k_ref, v_ref, o_ref, vt_scr, s_scr, p_scr, acc_scr,
                   *, tq, tk, lam_init):
    @pl.when(pl.program_id(2) == 0)
    def _():
        _transpose_values(v_ref, vt_scr, tk)

    lamv = lam_ref[...]
    lam = (jnp.exp(jnp.sum(lamv[0:1] * lamv[1:2], axis=1, keepdims=True))
           - jnp.exp(jnp.sum(lamv[2:3] * lamv[3:4], axis=1, keepdims=True)) + lam_init)
    q = q_ref[...]
    lane = lax.broadcasted_iota(jnp.int32, q.shape, 1)
    zero = jnp.zeros_like(q)
    q2 = jnp.concatenate([jnp.where(lane < C_QK_DIM, q, zero), jnp.where(lane >= C_QK_DIM, q, zero)], axis=0)
    acc, l = _flash_keymajor(q2, k_ref, vt_scr, s_scr, p_scr, acc_scr, tk)
    o = acc * (1.0 / l)
    oc = (o[:, :tq] - lam * o[:, tq:]).T
    o_ref[...] = (_rms_rows(oc, g_ref[...]) * (1.0 - lam_init)).astype(o_ref.dtype)


def _attn_c(qkv, c_lambda, c_head_norm, lam_init, batch, seq, tq=512):
    nq = seq // tq
    rows = 2 * tq
    tk = _flash_key_tile(seq, rows)
    return pl.pallas_call(
        functools.partial(_attn_c_kernel, tq=tq, tk=tk, lam_init=lam_init),
        grid=(batch, C_HEADS, nq),
        in_specs=[pl.BlockSpec((4, C_QK_DIM), lambda b, h, i: (0, 0)),
                  pl.BlockSpec((1, HEAD_DIM), lambda b, h, i: (0, 0)),
                  pl.BlockSpec((tq, HEAD_DIM), lambda b, h, i: (b * nq + i, OFF_CQ // HEAD_DIM + h)),
                  pl.BlockSpec((seq, HEAD_DIM), lambda b, h, i: (b, OFF_CK // HEAD_DIM + h)),
                  pl.BlockSpec((seq, HEAD_DIM), lambda b, h, i: (b, OFF_CV // HEAD_DIM + h))],
        out_specs=pl.BlockSpec((tq, HEAD_DIM), lambda b, h, i: (b * nq + i, h)),
        out_shape=jax.ShapeDtypeStruct((batch * seq, C_VW), BF16),
        scratch_shapes=_flash_scratch(seq, tk, rows),
        compiler_params=_cparams(("parallel", "parallel", "arbitrary")),
        name="attn_c",
    )(c_lambda, c_head_norm.reshape(1, HEAD_DIM), qkv, qkv, qkv)


def _merge_kernel(h_ref, wg0, wg1, wg2, oa0, oa1, oa2, ls0, ls1, ls2, ob, oc, wa, wb, wc, out_ref,
                  oa_scr, o1_scr, o2_scr, l1_scr, l2_scr, *, tm):
    def mix_groups(c, rows):
        for g, o_src, l_src, o_dst, l_dst in ((1, oa1, ls1, o1_scr, l1_scr), (2, oa2, ls2, o2_scr, l2_scr)):
            dil = DIL_GROUPS[g][1]
            nu = ROW_CHUNK // dil
            for r in range(dil):
                dst_rows = pl.ds(rows.start + r, nu, stride=dil)
                for hh in range(SUB // HEAD_DIM):
                    sl = slice(hh * HEAD_DIM, (hh + 1) * HEAD_DIM)
                    o_dst[hh, dst_rows, :] = o_src[r, c * nu:(c + 1) * nu, sl].astype(F32)
                    l_dst[hh, dst_rows, :] = l_src[r, c * nu:(c + 1) * nu, sl]
        for hh in range(SUB // HEAD_DIM):
            sl = slice(hh * HEAD_DIM, (hh + 1) * HEAD_DIM)
            l0, l1, l2 = ls0[rows, sl], l1_scr[hh, rows, :], l2_scr[hh, rows, :]
            m = jnp.maximum(jnp.maximum(l0, l1), l2)
            e0, e1, e2 = jnp.exp2(l0 - m), jnp.exp2(l1 - m), jnp.exp2(l2 - m)
            num = e0 * oa0[rows, sl].astype(F32) + e1 * o1_scr[hh, rows, :] + e2 * o2_scr[hh, rows, :]
            oa_scr[rows, sl] = (num / (e0 + e1 + e2)).astype(oa_scr.dtype)

    def project(rows):
        hr = h_ref[rows, :]
        merged = None
        for wg, o_br, w_br in ((wg0, oa_scr, wa), (wg1, ob, wb), (wg2, oc, wc)):
            gate = jax.nn.sigmoid(jnp.dot(hr, wg[...], preferred_element_type=F32))
            term = gate * jnp.dot(o_br[rows, :], w_br[...], preferred_element_type=F32)
            merged = term if merged is None else merged + term
        out_ref[rows, :] = merged.astype(out_ref.dtype)

    j = pl.program_id(1)

    @pl.when(j == 0)
    def _():
        for c, rows in enumerate(_row_chunks(tm)):
            mix_groups(c, rows)
            project(rows)

    @pl.when(j > 0)
    def _():
        for rows in _row_chunks(tm):
            project(rows)


def _merge(h, w_in, li, oa, lse, ob, oc, wa, wb, wc, seq, tm=1024, tn=512):
    t = ob.shape[0]
    nj = D_MODEL // tn
    nseq = seq // tm

    def row(w):
        return pl.BlockSpec((tm, w), lambda i, j: (i, 0))

    def stream(g):
        dil = DIL_GROUPS[g][1]
        return pl.BlockSpec((None, dil, tm // dil, SUB), lambda i, j: (i // nseq, 0, i % nseq, 0))

    def gate_w(br):
        return pl.BlockSpec((None, D_MODEL, tn),
                            lambda i, j: (li, 0, (QKV_COLS + br * D_MODEL) // tn + j))

    def wspec(k):
        return pl.BlockSpec((None, k, tn), lambda i, j: (li, 0, j))

    return pl.pallas_call(
        functools.partial(_merge_kernel, tm=tm),
        grid=(t // tm, nj),
        in_specs=[row(D_MODEL), gate_w(0), gate_w(1), gate_w(2),
                  row(SUB), stream(1), stream(2), row(SUB), stream(1), stream(2),
                  row(B_QW), row(C_VW), wspec(SUB), wspec(B_QW), wspec(C_VW)],
        out_specs=pl.BlockSpec((tm, tn), lambda i, j: (i, j)),
        out_shape=jax.ShapeDtypeStruct((t, D_MODEL), BF16),
        scratch_shapes=[pltpu.VMEM((tm, SUB), BF16)] + [pltpu.VMEM((SUB // HEAD_DIM, tm, HEAD_DIM), F32)] * 4,
        compiler_params=_cparams(("parallel", "arbitrary")),
        name="merge",
    )(h, w_in, w_in, w_in, *oa, *lse, ob, oc, wa, wb, wc)


def _out_kernel(m_ref, w_ref, x_ref, gpost_ref, gnext_ref, x1_ref, h_ref):
    for rows in _row_chunks(m_ref.shape[0]):
        y = jnp.dot(m_ref[rows, :], w_ref[...], preferred_element_type=F32)
        x1 = x_ref[rows, :] + _rms_rows(y, gpost_ref[...])
        x1_ref[rows, :] = x1
        h_ref[rows, :] = _rms_rows(x1, gnext_ref[...]).astype(h_ref.dtype)


def _out_proj(merged, w_out, li, x, g_post, g_next, tm=512):
    t = x.shape[0]
    row = lambda i: (i, 0)
    const = lambda i: (0, 0)
    return pl.pallas_call(
        _out_kernel,
        grid=(t // tm,),
        in_specs=[pl.BlockSpec((tm, D_MODEL), row),
                  pl.BlockSpec((None, D_MODEL, D_MODEL), lambda i: (li, 0, 0)),
                  pl.BlockSpec((tm, D_MODEL), row),
                  pl.BlockSpec((1, D_MODEL), const),
                  pl.BlockSpec((1, D_MODEL), const)],
        out_specs=[pl.BlockSpec((tm, D_MODEL), row), pl.BlockSpec((tm, D_MODEL), row)],
        out_shape=[jax.ShapeDtypeStruct((t, D_MODEL), F32), jax.ShapeDtypeStruct((t, D_MODEL), BF16)],
        compiler_params=_cparams(("parallel",)),
        name="out_proj",
    )(merged, w_out, x, g_post.reshape(1, D_MODEL), g_next.reshape(1, D_MODEL))


def _mlp_kernel(h_ref, w1_ref, w2_ref, x_ref, gpost_ref, *rest, with_next):
    if with_next:
        gnext_ref, x2_ref, hn_ref, acc_ref = rest
    else:
        x2_ref, acc_ref = rest
    f = pl.program_id(1)

    @pl.when(f == 0)
    def _():
        acc_ref[...] = jnp.zeros_like(acc_ref)

    def partial_out(rows):
        u = jnp.dot(h_ref[rows, :], w1_ref[...], preferred_element_type=F32)
        u = jnp.square(jnp.maximum(u, 0.0)).astype(BF16)
        return acc_ref[rows, :] + jnp.dot(u, w2_ref[...], preferred_element_type=F32)

    last = pl.num_programs(1) - 1

    @pl.when(f < last)
    def _():
        acc_ref[...] = partial_out(slice(None))

    @pl.when(f == last)
    def _():
        for rows in _row_chunks(h_ref.shape[0]):
            x2 = x_ref[rows, :] + _rms_rows(partial_out(rows), gpost_ref[...])
            x2_ref[rows, :] = x2
            if with_next:
                hn_ref[rows, :] = _rms_rows(x2, gnext_ref[...]).astype(hn_ref.dtype)


def _mlp(h, w1, w2, li, x, g_post, g_next, tm=512, tf=1024):
    t = x.shape[0]
    with_next = g_next is not None
    row = lambda i, f: (i, 0)
    const = lambda i, f: (0, 0)
    in_specs = [pl.BlockSpec((tm, D_MODEL), row),
                pl.BlockSpec((None, D_MODEL, tf), lambda i, f: (li, 0, f)),
                pl.BlockSpec((None, tf, D_MODEL), lambda i, f: (li, f, 0)),
                pl.BlockSpec((tm, D_MODEL), row),
                pl.BlockSpec((1, D_MODEL), const)]
    args = [h, w1, w2, x, g_post.reshape(1, D_MODEL)]
    out_specs = [pl.BlockSpec((tm, D_MODEL), row)]
    out_shape = [jax.ShapeDtypeStruct((t, D_MODEL), F32)]
    if with_next:
        in_specs.append(pl.BlockSpec((1, D_MODEL), const))
        args.append(g_next.reshape(1, D_MODEL))
        out_specs.append(pl.BlockSpec((tm, D_MODEL), row))
        out_shape.append(jax.ShapeDtypeStruct((t, D_MODEL), BF16))
    res = pl.pallas_call(
        functools.partial(_mlp_kernel, with_next=with_next),
        grid=(t // tm, D_FF // tf),
        in_specs=in_specs,
        out_specs=out_specs,
        out_shape=out_shape,
        scratch_shapes=[pltpu.VMEM((tm, D_MODEL), F32)],
        compiler_params=_cparams(("parallel", "arbitrary")),
        name="mlp",
    )(*args)
    return (res[0], res[1]) if with_next else (res[0], None)


def _rope_tables(pos, dim):
    inv = 1.0 / (ROPE_THETA ** (jnp.arange(0, dim, 2, dtype=F32) / dim))
    ang = pos.astype(F32)[:, None] * inv[None, :]
    return jnp.cos(ang), jnp.sin(ang)


def _tables(seq):
    pos = jnp.arange(seq)
    ca, sa = _rope_tables(pos, HEAD_DIM)
    cos_a = jnp.concatenate([ca, ca], axis=1)
    sin_a = jnp.concatenate([-sa, sa], axis=1)
    z32 = jnp.zeros((seq, 32), F32)

    def half32(c, s):
        return jnp.concatenate([c, c], 1), jnp.concatenate([-s, z32], 1), jnp.concatenate([z32, s], 1)

    cr, sr = _rope_tables(pos // GRID_W, AXIAL_DIM)
    cc, sc = _rope_tables(pos % GRID_W, AXIAL_DIM)
    r_c, r_lo, r_hi = half32(cr, sr)
    c_c, c_lo, c_hi = half32(cc, sc)
    cos_b = jnp.concatenate([r_c, c_c], 1)
    sin_b_lo = jnp.concatenate([r_lo, c_lo], 1)
    sin_b_hi = jnp.concatenate([r_hi, c_hi], 1)
    c3, s3 = _rope_tables(pos, C_QK_DIM)
    m_c, m_lo, m_hi = half32(c3, s3)
    cos_c = jnp.concatenate([m_c, m_c], 1)
    sin_c_lo = jnp.concatenate([m_lo, m_lo], 1)
    sin_c_hi = jnp.concatenate([m_hi, m_hi], 1)
    return jnp.stack([cos_a, sin_a, cos_b, sin_b_lo, sin_b_hi, cos_c, sin_c_lo, sin_c_hi], axis=0)


def _trunk(x3, tabs, p):
    batch, seq, _ = x3.shape
    x = x3.reshape(batch * seq, D_MODEL)
    h = None
    for li in range(DEPTH):
        lam_init = 0.8 - 0.6 * math.exp(-0.3 * li)
        if li == 0:
            qkv, st1, st2, h = _qkv_proj(x, p["w_in"], li, tabs, p["b_q_norm"][li], p["b_k_norm"][li],
                                         batch, seq, g_pre=p["g_mix_pre"][li])
        else:
            qkv, st1, st2 = _qkv_proj(h, p["w_in"], li, tabs, p["b_q_norm"][li], p["b_k_norm"][li],
                                      batch, seq)
        oa, lse = zip(*[_attn_a_group(src, gi, batch, seq) for gi, src in enumerate((qkv, st1, st2))])
        ob = _attn_b(qkv, batch, seq)
        oc = _attn_c(qkv, p["c_lambda"][li], p["c_head_norm"][li], lam_init, batch, seq)
        merged = _merge(h, p["w_in"], li, oa, lse, ob, oc,
                        p["w_branch_a"], p["w_branch_b"], p["w_branch_c"], seq)
        x, hm = _out_proj(merged, p["w_out"], li, x, p["g_mix_post"][li], p["g_mlp_pre"][li])
        g_next = p["g_mix_pre"][li + 1] if li + 1 < DEPTH else None
        x, h = _mlp(hm, p["w_mlp_in"], p["w_mlp_out"], li, x, p["g_mlp_post"][li], g_next)
    return x.reshape(batch, seq, D_MODEL)


def kernel(x_prompt, x_sample, g_mix_pre, w_in, b_q_norm, b_k_norm, c_lambda, c_head_norm,
           w_branch_a, w_branch_b, w_branch_c, w_out, g_mix_post,
           g_mlp_pre, w_mlp_in, w_mlp_out, g_mlp_post):
    p = dict(g_mix_pre=g_mix_pre, w_in=w_in.astype(BF16), b_q_norm=b_q_norm, b_k_norm=b_k_norm,
             c_lambda=c_lambda, c_head_norm=c_head_norm,
             w_branch_a=w_branch_a.astype(BF16), w_branch_b=w_branch_b.astype(BF16),
             w_branch_c=w_branch_c.astype(BF16), w_out=w_out.astype(BF16), g_mix_post=g_mix_post,
             g_mlp_pre=g_mlp_pre, w_mlp_in=w_mlp_in.astype(BF16), w_mlp_out=w_mlp_out.astype(BF16),
             g_mlp_post=g_mlp_post)
    tabs = _tables(max(x_prompt.shape[1], x_sample.shape[1]))
    return (_trunk(x_prompt, tabs, p), _trunk(x_sample, tabs, p))
```
